```python
import jax
import jax.numpy as jnp
from jax import lax
import numpy as np

D_MODEL = 2048
BATCH = 1
SEQ = 8192
DEPTH = 4

GRID_W = 64
CTX_LEN = 256
N_EVEN = (DEPTH + 1) // 2
N_ODD = DEPTH // 2
D_FF = ((8 * D_MODEL // 3 + 255) // 256) * 256
NORM_EPS = 1e-6
ROPE_THETA = 10000.0
NA_HD = 128
A_W = D_MODEL // 2
NA_HEADS = A_W // NA_HD
NA_ROWS_MAX = 8
NA_COLS = 16
HG_DK = 128
B_W = D_MODEL // 2
HG_HEADS = B_W // HG_DK
HG_CHUNK = 64
C_HD = 128
C_HEADS = D_MODEL // C_HD
C_KV_HEADS = C_HEADS // 4
C_GROUP = C_HEADS // C_KV_HEADS
C_WINDOW = 128
C_BLOCK = 128
C_QKV = (C_HEADS + 2 * C_KV_HEADS) * C_HD
EVEN_IN = 3 * A_W + 5 * B_W
EVEN_SPLITS = (A_W, 2 * A_W, 3 * A_W, 3 * A_W + B_W, 3 * A_W + 2 * B_W, 3 * A_W + 3 * B_W, 3 * A_W + 4 * B_W)
C_SPLITS = (C_HEADS * C_HD, (C_HEADS + C_KV_HEADS) * C_HD)

kernel_name = 'hybrid_na_hgrn2_swa_diffusion_trunk'


def rms_norm(x, g):
    x32 = x.astype(jnp.float32)
    y = x32 * lax.rsqrt(jnp.mean(x32 * x32, axis=-1, keepdims=True) + NORM_EPS)
    return (y * g.astype(jnp.float32)).astype(x.dtype)


def adaln(x, g, shift, scale):
    return rms_norm(x, g) * (1 + scale) + shift


def swiglu(u, w_in, w_out):
    gate, up = jnp.split(u @ w_in, 2, axis=-1)
    return (jax.nn.silu(gate) * up) @ w_out


def rope_1d(x, pos):
    d = x.shape[-1]
    inv = ROPE_THETA ** (-jnp.arange(0, d, 2, dtype=jnp.float32) / d)
    ang = pos.astype(jnp.float32)[:, None] * inv[None, :]
    cos = jnp.cos(ang)[None, :, None, :].astype(x.dtype)
    sin = jnp.sin(ang)[None, :, None, :].astype(x.dtype)
    x1, x2 = jnp.split(x, 2, axis=-1)
    return jnp.concatenate([x1 * cos - x2 * sin, x1 * sin + x2 * cos], axis=-1)


def axial_rope(x, rows, cols):
    xr, xc = jnp.split(x, 2, axis=-1)
    return jnp.concatenate([rope_1d(xr, rows), rope_1d(xc, cols)], axis=-1)


def context_attention(q, k, v, sink=None):
    bsz, seq, kvh, grp, hd = q.shape
    s = jnp.einsum('blkgd,bmkd->bkglm', q, k).astype(jnp.float32) * (hd ** -0.5)
    if sink is not None:
        s_sink = jnp.broadcast_to(sink.astype(jnp.float32).reshape(1, kvh, grp, 1, 1), s.shape[:-1] + (1,))
        p = jax.nn.softmax(jnp.concatenate([s, s_sink], axis=-1), axis=-1)[..., :-1]
    else:
        p = jax.nn.softmax(s, axis=-1)
    o = jnp.einsum('bkglm,bmkd->blkgd', p.astype(v.dtype), v)
    return o.reshape(bsz, seq, kvh * grp * hd)


def neighbourhood_attention(q, k, v, kc, vc, rpb):
    bsz, n, nh, hd = q.shape
    rows = n // GRID_W
    kr = min(NA_ROWS_MAX, rows)
    scale = hd ** -0.5
    kg = k.reshape(bsz, rows, GRID_W, nh, hd)
    vg = v.reshape(bsz, rows, GRID_W, nh, hd)
    q_rows = jnp.moveaxis(q.reshape(bsz, rows, GRID_W, nh, hd), 1, 0)
    col = jnp.arange(GRID_W)
    col_idx = jnp.clip(col - NA_COLS // 2, 0, GRID_W - NA_COLS)[:, None] + jnp.arange(NA_COLS)[None, :]
    rpb_cols = rpb[:, :, col_idx - col[:, None] + NA_COLS - 1]
    n_loc = kr * NA_COLS

    def one_row(args):
        q_r, r = args
        r0 = jnp.clip(r - kr // 2, 0, rows - kr)
        k_win = lax.dynamic_slice_in_dim(kg, r0, kr, axis=1)[:, :, col_idx]
        v_win = lax.dynamic_slice_in_dim(vg, r0, kr, axis=1)[:, :, col_idx]
        row_bias_idx = r0 + jnp.arange(kr) - r + NA_ROWS_MAX - 1
        bias = jnp.transpose(rpb_cols[:, row_bias_idx], (0, 2, 1, 3)).astype(jnp.float32)
        s_loc = jnp.einsum('bqhd,bkqjhd->bhqkj', q_r, k_win).astype(jnp.float32) * scale + bias[None]
        s_ctx = jnp.einsum('bqhd,blhd->bhql', q_r, kc).astype(jnp.float32) * scale
        s = jnp.concatenate([s_loc.reshape(bsz, nh, GRID_W, n_loc), s_ctx], axis=-1)
        p = jax.nn.softmax(s, axis=-1).astype(v.dtype)
        p_loc = p[..., :n_loc].reshape(bsz, nh, GRID_W, kr, NA_COLS)
        return (jnp.einsum('bhqkj,bkqjhd->bqhd', p_loc, v_win)
                + jnp.einsum('bhql,blhd->bqhd', p[..., n_loc:], vc))

    out = lax.map(one_row, (q_rows, jnp.arange(rows)))
    return jnp.moveaxis(out, 0, 1).reshape(bsz, n, nh * hd)


def hgrn_gates(fx, lb):
    fx32 = fx.astype(jnp.float32)
    lb = lb.reshape(HG_HEADS, HG_DK)
    log_f = jnp.logaddexp(jnp.log(lb), jnp.log1p(-lb) + jax.nn.log_sigmoid(fx32))
    k = (1.0 - lb) * jax.nn.sigmoid(-fx32)
    return log_f, k


def gated_chunk_scan(q, k, v, log_f, s0, with_out):
    bsz, t_len, nh, _ = k.shape
    nc = t_len // HG_CHUNK

    def chunks(t):
        return jnp.transpose(t.astype(jnp.float32).reshape(bsz, nc, HG_CHUNK, nh, t.shape[-1]), (1, 0, 3, 2, 4))

    tri = jnp.tril(jnp.ones((HG_CHUNK, HG_CHUNK), dtype=bool))

    def step(state, inp):
        kc, vc, gc = inp[:3]
        b = jnp.cumsum(gc, axis=2)
        b_last = b[:, :, -1:, :]
        new_state = (jnp.exp(b_last[:, :, 0, :])[..., None] * state
                     + jnp.einsum('bhcd,bhce->bhde', kc * jnp.exp(b_last - b), vc))
        if not with_out:
            return new_state, None
        qc = inp[3]
        o_inter = jnp.einsum('bhcd,bhde->bhce', qc * jnp.exp(b), state)
        diff = b[:, :, :, None, :] - b[:, :, None, :, :]
        decay = jnp.exp(jnp.where(tri[None, None, :, :, None], diff, -jnp.inf))
        attn = jnp.einsum('bhtsd,bhtd->bhts', decay * kc[:, :, None, :, :], qc)
        return new_state, o_inter + jnp.einsum('bhts,bhse->bhte', attn, vc)

    xs = (chunks(k), chunks(v), chunks(log_f)) + ((chunks(q),) if with_out else ())
    s_fin, o = lax.scan(step, s0, xs)
    if with_out:
        o = jnp.transpose(o, (1, 0, 3, 2, 4)).reshape(bsz, t_len, nh, -1)
    return s_fin, o


def hgrn_output(o, g, norm_g):
    bsz, t_len = o.shape[:2]
    o = o * lax.rsqrt(jnp.mean(o * o, axis=-1, keepdims=True) + NORM_EPS)
    o = o.reshape(bsz, t_len, -1) * norm_g.astype(jnp.float32)
    return (o * jax.nn.silu(g.astype(jnp.float32))).astype(g.dtype)


def hgrn2_bidirectional(lat, ctx_in, lb, norm_g, need_ctx):
    def heads(t):
        return t.reshape(t.shape[0], t.shape[1], HG_HEADS, HG_DK)
    q, f_fw, f_bw, i, g = lat
    qc, fc_fw, fc_bw, ic, gc = ctx_in
    qh, ih, qch, ich = heads(q), heads(i), heads(qc), heads(ic)
    bsz = q.shape[0]
    o_lat = 0.0
    o_ctx = 0.0
    for d, (f_l, f_c) in enumerate(((f_fw, fc_fw), (f_bw, fc_bw))):
        rev = (lambda t: t) if d == 0 else (lambda t: jnp.flip(t, axis=1))
        log_f, k = hgrn_gates(heads(f_l), lb[d])
        log_fc, kc = hgrn_gates(heads(f_c), lb[d])
        s0 = jnp.zeros((bsz, HG_HEADS, HG_DK, HG_DK), jnp.float32)
        s_ctx, oc_d = gated_chunk_scan(rev(qch), rev(kc), rev(ich), rev(log_fc), s0, need_ctx)
        _, o_d = gated_chunk_scan(rev(qh), rev(k), rev(ih), rev(log_f), s_ctx, True)
        o_lat = o_lat + rev(o_d)
        if need_ctx:
            o_ctx = o_ctx + rev(oc_d)
    y = hgrn_output(o_lat, g, norm_g)
    yc = hgrn_output(o_ctx, gc, norm_g) if need_ctx else None
    return y, yc


def parallel_na_hgrn(u, uc, w_in, w_out, rpb, lb, hg_norm_g, need_ctx):
    qa, ka, va, *lat_b = jnp.split(u @ w_in, EVEN_SPLITS, axis=-1)
    qa_c, ka_c, va_c, *ctx_b = jnp.split(uc @ w_in, EVEN_SPLITS, axis=-1)

    def heads(t):
        return t.reshape(t.shape[0], t.shape[1], NA_HEADS, NA_HD)
    ya = neighbourhood_attention(heads(qa), heads(ka), heads(va), heads(ka_c), heads(va_c), rpb)
    yb, yb_c = hgrn2_bidirectional(tuple(lat_b), tuple(ctx_b), lb, hg_norm_g, need_ctx)
    y = jnp.concatenate([ya, yb], axis=-1) @ w_out
    if not need_ctx:
        return y, None
    ya_c = context_attention(heads(qa_c)[:, :, :, None, :], heads(ka_c), heads(va_c))
    return y, jnp.concatenate([ya_c, yb_c], axis=-1) @ w_out


def banded_window_attention(q, k, v, kc, vc, sink):
    bsz, n, kvh, grp, hd = q.shape
    nb = n // C_BLOCK
    scale = hd ** -0.5
    qb = q.reshape(bsz, nb, C_BLOCK, kvh, grp, hd)

    def band(t):
        tp = jnp.pad(t, ((0, 0), (C_BLOCK, C_BLOCK), (0, 0), (0, 0))).reshape(bsz, nb + 2, C_BLOCK, kvh, hd)
        return jnp.concatenate([tp[:, :-2], tp[:, 1:-1], tp[:, 2:]], axis=2)

    kb, vb = band(k), band(v)
    blk = jnp.arange(nb)[:, None] * C_BLOCK
    qpos = blk + jnp.arange(C_BLOCK)[None, :]
    kpos = blk - C_BLOCK + jnp.arange(3 * C_BLOCK)[None, :]
    rel = kpos[:, None, :] - qpos[:, :, None]
    valid = (jnp.abs(rel) <= C_WINDOW) & (kpos[:, None, :] >= 0) & (kpos[:, None, :] < n)
    s_loc = jnp.einsum('bnqkgd,bnskd->bnkgqs', qb, kb).astype(jnp.float32) * scale
    s_loc = jnp.where(valid[None, :, None, None], s_loc, -jnp.inf)
    s_ctx = jnp.einsum('bnqkgd,blkd->bnkgql', qb, kc).astype(jnp.float32) * scale
    s_sink = jnp.broadcast_to(sink.astype(jnp.float32).reshape(1, 1, kvh, grp, 1, 1), s_loc.shape[:-1] + (1,))
    p = jax.nn.softmax(jnp.concatenate([s_loc, s_ctx, s_sink], axis=-1), axis=-1).astype(v.dtype)
    n_loc = 3 * C_BLOCK
    n_ctx = kc.shape[1]
    o = (jnp.einsum('bnkgqs,bnskd->bnqkgd', p[..., :n_loc], vb)
         + jnp.einsum('bnkgql,blkd->bnqkgd', p[..., n_loc:n_loc + n_ctx], vc))
    return o.reshape(bsz, n, kvh * grp * hd)


def windowed_gqa_sink(u, uc, w_qkv, w_o, sink, need_ctx):
    bsz, n, _ = u.shape
    n_ctx = uc.shape[1]
    q, k, v = jnp.split(u @ w_qkv, C_SPLITS, axis=-1)
    qc, kc, vc = jnp.split(uc @ w_qkv, C_SPLITS, axis=-1)
    t = jnp.arange(n)
    rows, cols = t // GRID_W, t % GRID_W
    q = axial_rope(q.reshape(bsz, n, C_HEADS, C_HD), rows, cols).reshape(bsz, n, C_KV_HEADS, C_GROUP, C_HD)
    k = axial_rope(k.reshape(bsz, n, C_KV_HEADS, C_HD), rows, cols)
    v = v.reshape(bsz, n, C_KV_HEADS, C_HD)
    kc = kc.reshape(bsz, n_ctx, C_KV_HEADS, C_HD)
    vc = vc.reshape(bsz, n_ctx, C_KV_HEADS, C_HD)
    y = banded_window_attention(q, k, v, kc, vc, sink) @ w_o
    if not need_ctx:
        return y, None
    yc = context_attention(qc.reshape(bsz, n_ctx, C_KV_HEADS, C_GROUP, C_HD), kc, vc, sink) @ w_o
    return y, yc


def setup_inputs(seed: int = 0) -> dict:
    key = jax.random.key(seed)
    ks = jax.random.split(key, 18)

    def nrm(k, shape, s):
        return jax.random.normal(k, shape, jnp.float32) * s

    return {
        'x': nrm(ks[0], (BATCH, SEQ, D_MODEL), 1.0),
        'c': nrm(ks[1], (BATCH, D_MODEL), 1.0),
        'ctx': nrm(ks[2], (BATCH, CTX_LEN, D_MODEL), 1.0),
        'c_ctx': nrm(ks[3], (D_MODEL,), 1.0),
        'w_mod': nrm(ks[4], (DEPTH, D_MODEL, 9 * D_MODEL), 0.5 * D_MODEL ** -0.5),
        'b_mod': nrm(ks[5], (DEPTH, 9 * D_MODEL), 0.02),
        'norm_g': 1.0 + nrm(ks[6], (DEPTH, 3, D_MODEL), 0.05),
        'w_ff_in': nrm(ks[7], (DEPTH, 2, D_MODEL, 2 * D_FF), D_MODEL ** -0.5),
        'w_ff_out': nrm(ks[8], (DEPTH, 2, D_FF, D_MODEL), D_FF ** -0.5),
        'w_in_even': nrm(ks[9], (N_EVEN, D_MODEL, EVEN_IN), D_MODEL ** -0.5),
        'w_out_even': nrm(ks[10], (N_EVEN, A_W + B_W, D_MODEL), (A_W + B_W) ** -0.5),
        'na_rpb': nrm(ks[11], (N_EVEN, NA_HEADS, 2 * NA_ROWS_MAX - 1, 2 * NA_COLS - 1), 0.1),
        'hg_lb_logits': nrm(ks[12], (2, N_EVEN, B_W), 1.0),
        'hg_norm_g': 1.0 + nrm(ks[13], (N_EVEN, B_W), 0.05),
        'w_qkv_odd': nrm(ks[14], (N_ODD, D_MODEL, C_QKV), D_MODEL ** -0.5),
        'w_o_odd': nrm(ks[15], (N_ODD, C_HEADS * C_HD, D_MODEL), (C_HEADS * C_HD) ** -0.5),
        'sink_odd': nrm(ks[16], (N_ODD, C_HEADS), 0.5),
        'final_norm_g': 1.0 + nrm(ks[17], (D_MODEL,), 0.05),
    }


def reference(x, c, ctx, c_ctx, w_mod, b_mod, norm_g, w_ff_in, w_ff_out, w_in_even, w_out_even,
              na_rpb, hg_lb_logits, hg_norm_g, w_qkv_odd, w_o_odd, sink_odd, final_norm_g):
    bsz = x.shape[0]
    h, hc = x, ctx
    lb_all = jnp.cumsum(jax.nn.softmax(hg_lb_logits.astype(jnp.float32), axis=1), axis=1)
    lb_all = lb_all - lb_all[:, :1]
    sc = jax.nn.silu(c)
    scc = jax.nn.silu(c_ctx)
    for l in range(DEPTH):
        need_ctx = l < DEPTH - 1
        mod = (sc @ w_mod[l] + b_mod[l]).reshape(bsz, 3, 3, D_MODEL)[:, :, :, None, :]
        modc = (scc @ w_mod[l] + b_mod[l]).reshape(3, 3, D_MODEL)
        h = h + 0.5 * mod[:, 0, 2] * swiglu(adaln(h, norm_g[l, 0], mod[:, 0, 0], mod[:, 0, 1]), w_ff_in[l, 0], w_ff_out[l, 0])
        hc = hc + 0.5 * modc[0, 2] * swiglu(adaln(hc, norm_g[l, 0], modc[0, 0], modc[0, 1]), w_ff_in[l, 0], w_ff_out[l, 0])
        u = adaln(h, norm_g[l, 1], mod[:, 1, 0], mod[:, 1, 1])
        uc = adaln(hc, norm_g[l, 1], modc[1, 0], modc[1, 1])
        if l % 2 == 0:
            e = l // 2
            y, yc = parallel_na_hgrn(u, uc, w_in_even[e], w_out_even[e], na_rpb[e], lb_all[:, e], hg_norm_g[e], need_ctx)
        else:
            o = l // 2
            y, yc = windowed_gqa_sink(u, uc, w_qkv_odd[o], w_o_odd[o], sink_odd[o], need_ctx)
        h = h + mod[:, 1, 2] * y
        h = h + 0.5 * mod[:, 2, 2] * swiglu(adaln(h, norm_g[l, 2], mod[:, 2, 0], mod[:, 2, 1]), w_ff_in[l, 1], w_ff_out[l, 1])
        if need_ctx:
            hc = hc + modc[1, 2] * yc
            hc = hc + 0.5 * modc[2, 2] * swiglu(adaln(hc, norm_g[l, 2], modc[2, 0], modc[2, 1]), w_ff_in[l, 1], w_ff_out[l, 1])
    return rms_norm(h, final_norm_g)
```

```python
import functools

import numpy as np
import jax
import jax.numpy as jnp
from jax import lax
from jax.experimental import pallas as pl
from jax.experimental.pallas import tpu as pltpu

F32 = jnp.float32
BF16 = jnp.bfloat16

GRID_W = 64
NORM_EPS = 1e-6
ROPE_THETA = 10000.0
HEAD_DIM = 128
NA_ROWS = 8
NA_COLS = 16
NA_Q_ROWS = 4
NA_K_ROWS = NA_Q_ROWS + NA_ROWS
ATT_BLOCK = NA_Q_ROWS * GRID_W
WINDOW = 128
HG_CHUNK = 256
MASKED = -1e30
VMEM_LIMIT = 56 * 1024 * 1024


def _params(*sem):
    return pltpu.CompilerParams(dimension_semantics=sem, vmem_limit_bytes=VMEM_LIMIT)


def _dot(a, b):
    return jnp.dot(a, b, preferred_element_type=F32)


def _dot_nt(a, b):
    return lax.dot_general(a, b, (((1,), (1,)), ((), ())), preferred_element_type=F32)


def _sigmoid(x):
    return 1.0 / (1.0 + jnp.exp(-x))


def _silu(x):
    return x * _sigmoid(x)


def _mod_kernel(c_ref, w_ref, b_ref, o_ref):
    s = _silu(c_ref[...]).astype(BF16)
    o_ref[0] = _dot(s, w_ref[0].astype(BF16)) + b_ref[0]


def _modulation(cc, w_mod, b_mod):
    depth, d, n = w_mod.shape
    tn = 1024
    return pl.pallas_call(
        _mod_kernel,
        grid=(depth, n // tn),
        in_specs=[
            pl.BlockSpec((8, d), lambda l, j: (0, 0)),
            pl.BlockSpec((1, d, tn), lambda l, j: (l, 0, j)),
            pl.BlockSpec((1, 1, tn), lambda l, j: (l, 0, j)),
        ],
        out_specs=pl.BlockSpec((1, 8, tn), lambda l, j: (l, 0, j)),
        out_shape=jax.ShapeDtypeStruct((depth, 8, n), F32),
        compiler_params=_params("arbitrary", "arbitrary"),
        name="modulation",
    )(cc, w_mod, b_mod.reshape(depth, 1, n))


def _mod_rows(mod_ref, sub, k, is_ctx):
    r = 3 * sub + k
    return jnp.where(is_ctx, mod_ref[1, r:r + 1, :], mod_ref[0, r:r + 1, :])


def _is_ctx(tile, tm, n_lat):
    row = tile * tm + lax.broadcasted_iota(jnp.int32, (tm, 1), 0)
    return row >= n_lat


def _adaln(h, g, shift, scale):
    y = h * lax.rsqrt(jnp.mean(h * h, axis=-1, keepdims=True) + NORM_EPS)
    return (y * g) * (1.0 + scale) + shift


def _ffn_kernel(h_ref, mod_ref, g_ref, wg_ref, wu_ref, wo_ref, fg_ref, o_ref, u_scr,
                *, sub, tm, n_lat, final_norm):
    i, j = pl.program_id(0), pl.program_id(1)
    is_ctx = _is_ctx(i, tm, n_lat)

    @pl.when(j == 0)
    def _():
        u = _adaln(h_ref[...], g_ref[...], _mod_rows(mod_ref, sub, 0, is_ctx),
                   _mod_rows(mod_ref, sub, 1, is_ctx))
        u_scr[...] = u.astype(BF16)
        o_ref[...] = jnp.zeros_like(o_ref)

    u = u_scr[...]
    gate = _dot(u, wg_ref[...])
    up = _dot(u, wu_ref[...])
    o_ref[...] += _dot((_silu(gate) * up).astype(BF16), wo_ref[...])

    @pl.when(j == pl.num_programs(1) - 1)
    def _():
        h = h_ref[...] + (0.5 * _mod_rows(mod_ref, sub, 2, is_ctx)) * o_ref[...]
        if final_norm:
            h = h * lax.rsqrt(jnp.mean(h * h, axis=-1, keepdims=True) + NORM_EPS) * fg_ref[...]
        o_ref[...] = h


def _ffn(h, mod, g, w_in, w_out, fg, *, sub, rows, tm, n_lat, final_norm=False):
    d = h.shape[1]
    f = w_out.shape[0]
    tf = 256
    nf = f // tf
    return pl.pallas_call(
        functools.partial(_ffn_kernel, sub=sub, tm=tm, n_lat=n_lat, final_norm=final_norm),
        grid=(rows // tm, nf),
        in_specs=[
            pl.BlockSpec((tm, d), lambda i, j: (i, 0)),
            pl.BlockSpec((2, 9, d), lambda i, j: (0, 0, 0)),
            pl.BlockSpec((1, d), lambda i, j: (0, 0)),
            pl.BlockSpec((d, tf), lambda i, j: (0, j)),
            pl.BlockSpec((d, tf), lambda i, j: (0, nf + j)),
            pl.BlockSpec((tf, d), lambda i, j: (j, 0)),
            pl.BlockSpec((1, d), lambda i, j: (0, 0)),
        ],
        out_specs=pl.BlockSpec((tm, d), lambda i, j: (i, 0)),
        out_shape=jax.ShapeDtypeStruct((rows, d), F32),
        scratch_shapes=[pltpu.VMEM((tm, d), BF16)],
        compiler_params=_params("arbitrary", "arbitrary"),
        name="ffn",
    )(h, mod, g, w_in, w_in, w_out, fg)


def _rope(x, cos, sin):
    lane = lax.broadcasted_iota(jnp.int32, x.shape, 1)
    first = (lane % 64) < 32
    swapped = jnp.where(first, pltpu.roll(x, 96, 1), pltpu.roll(x, 32, 1))
    return x * cos + swapped * sin


def _proj_kernel(h_ref, mod_ref, g_ref, w_ref, cos_ref, sin_ref, o_ref, u_scr,
                 *, tm, tn, n_lat, rope_tiles):
    i, j = pl.program_id(0), pl.program_id(1)

    @pl.when(j == 0)
    def _():
        is_ctx = _is_ctx(i, tm, n_lat)
        u = _adaln(h_ref[...], g_ref[...], _mod_rows(mod_ref, 1, 0, is_ctx),
                   _mod_rows(mod_ref, 1, 1, is_ctx))
        u_scr[...] = u.astype(BF16)

    y = _dot(u_scr[...], w_ref[...])
    if rope_tiles == 0:
        o_ref[...] = y
    else:
        @pl.when(j < rope_tiles)
        def _():
            cos, sin = cos_ref[...], sin_ref[...]
            for hd in range(tn // HEAD_DIM):
                sl = slice(hd * HEAD_DIM, (hd + 1) * HEAD_DIM)
                o_ref[:, sl] = _rope(y[:, sl], cos, sin)

        @pl.when(j >= rope_tiles)
        def _():
            o_ref[...] = y


def _proj(h, mod, g, w, cos, sin, *, tm, n_lat, rope_cols):
    rows, d = h.shape
    n = w.shape[1]
    tn = 512
    return pl.pallas_call(
        functools.partial(_proj_kernel, tm=tm, tn=tn, n_lat=n_lat, rope_tiles=rope_cols // tn),
        grid=(rows // tm, n // tn),
        in_specs=[
            pl.BlockSpec((tm, d), lambda i, j: (i, 0)),
            pl.BlockSpec((2, 9, d), lambda i, j: (0, 0, 0)),
            pl.BlockSpec((1, d), lambda i, j: (0, 0)),
            pl.BlockSpec((d, tn), lambda i, j: (0, j)),
            pl.BlockSpec((tm, HEAD_DIM), lambda i, j: (i, 0)),
            pl.BlockSpec((tm, HEAD_DIM), lambda i, j: (i, 0)),
        ],
        out_specs=pl.BlockSpec((tm, tn), lambda i, j: (i, j)),
        out_shape=jax.ShapeDtypeStruct((rows, n), F32),
        scratch_shapes=[pltpu.VMEM((tm, d), BF16)],
        compiler_params=_params("arbitrary", "arbitrary"),
        name="proj",
    )(h, mod, g, w, cos, sin)


def _rope_tables(n_lat, n_ctx):
    t = np.arange(n_lat)
    inv = ROPE_THETA ** (-jnp.arange(0, 64, 2, dtype=F32) / 64)
    ang_r = jnp.asarray(t // GRID_W, F32)[:, None] * inv[None, :]
    ang_c = jnp.asarray(t % GRID_W, F32)[:, None] * inv[None, :]
    cr, sr, cc, sc = jnp.cos(ang_r), jnp.sin(ang_r), jnp.cos(ang_c), jnp.sin(ang_c)
    cos = jnp.concatenate([cr, cr, cc, cc], axis=1)
    sin = jnp.concatenate([-sr, sr, -sc, sc], axis=1)
    cos = jnp.concatenate([cos, jnp.ones((n_ctx, HEAD_DIM), F32)], axis=0)
    sin = jnp.concatenate([sin, jnp.zeros((n_ctx, HEAD_DIM), F32)], axis=0)
    return cos, sin


def _oproj_kernel(a1_ref, a2_ref, w_ref, h_ref, mod_ref, o_ref, *, tm, n_lat, k1):
    is_ctx = _is_ctx(pl.program_id(0), tm, n_lat)
    y = _dot(a1_ref[...], w_ref[:k1, :]) + _dot(a2_ref[...], w_ref[k1:, :])
    o_ref[...] = h_ref[...] + _mod_rows(mod_ref, 1, 2, is_ctx) * y


def _oproj(a1, a2, col2, w, h, mod, *, tm, n_lat):
    rows, d = h.shape
    k1 = w.shape[0] // 2
    return pl.pallas_call(
        functools.partial(_oproj_kernel, tm=tm, n_lat=n_lat, k1=k1),
        grid=(rows // tm,),
        in_specs=[
            pl.BlockSpec((tm, k1), lambda i: (i, 0)),
            pl.BlockSpec((tm, k1), lambda i: (i, col2)),
            pl.BlockSpec(w.shape, lambda i: (0, 0)),
            pl.BlockSpec((tm, d), lambda i: (i, 0)),
            pl.BlockSpec((2, 9, d), lambda i: (0, 0, 0)),
        ],
        out_specs=pl.BlockSpec((tm, d), lambda i: (i, 0)),
        out_shape=jax.ShapeDtypeStruct((rows, d), F32),
        compiler_params=_params("arbitrary"),
        name="oproj",
    )(a1, a2, w, h, mod)


def _na_bias_index(n_rows):
    nb = n_rows // NA_Q_ROWS
    q = np.arange(ATT_BLOCK)
    k = np.arange(NA_K_ROWS * GRID_W)
    dr, c = q // GRID_W, q % GRID_W
    kr, kc = k // GRID_W, k % GRID_W
    c0 = np.clip(c - NA_COLS // 2, 0, GRID_W - NA_COLS)
    col_ok = (kc[None, :] >= c0[:, None]) & (kc[None, :] < c0[:, None] + NA_COLS)
    bj = kc[None, :] - c[:, None] + NA_COLS - 1
    ai, ok = [], []
    for b in (0, 1, nb - 1):
        r = b * NA_Q_ROWS + dr
        r0 = np.clip(r - NA_ROWS // 2, 0, n_rows - NA_ROWS)
        base = np.clip(b * NA_Q_ROWS - NA_ROWS // 2, 0, n_rows - NA_K_ROWS)
        kabs = base + kr
        row_ok = (kabs[None, :] >= r0[:, None]) & (kabs[None, :] < r0[:, None] + NA_ROWS)
        ai.append(kabs[None, :] - r[:, None] + NA_ROWS - 1)
        ok.append(row_ok & col_ok)
    ok = np.stack(ok)
    ai = np.where(ok, np.stack(ai), 0)
    bj = np.where(ok, np.broadcast_to(bj, ok.shape), 0)
    return ai, bj, ok


def _na_kernel(q_ref, k_ref, v_ref, bias_ref, o_ref, *, n_lat, n_ctx, nb):
    b = pl.program_id(1)
    scale = HEAD_DIM ** -0.5
    q = q_ref[...].astype(BF16)
    kc = k_ref[n_lat:n_lat + n_ctx, :].astype(BF16)
    vc = v_ref[n_lat:n_lat + n_ctx, :].astype(BF16)
    s_ctx = _dot_nt(q, kc) * scale
    m_ctx = jnp.max(s_ctx, axis=-1, keepdims=True)

    @pl.when(b < nb)
    def _():
        nk = NA_K_ROWS * GRID_W
        start = pl.multiple_of(jnp.clip(b - 1, 0, nb - 3) * ATT_BLOCK, ATT_BLOCK)
        kl = k_ref[pl.ds(start, nk), :].astype(BF16)
        vl = v_ref[pl.ds(start, nk), :].astype(BF16)
        s_loc = _dot_nt(q, kl) * scale + bias_ref[0, 0]
        m = jnp.maximum(jnp.max(s_loc, axis=-1, keepdims=True), m_ctx)
        p_loc = jnp.exp(s_loc - m)
        p_ctx = jnp.exp(s_ctx - m)
        den = jnp.sum(p_loc, axis=-1, keepdims=True) + jnp.sum(p_ctx, axis=-1, keepdims=True)
        o = _dot(p_loc.astype(BF16), vl) + _dot(p_ctx.astype(BF16), vc)
        o_ref[...] = (o / den).astype(o_ref.dtype)

    @pl.when(b >= nb)
    def _():
        p_ctx = jnp.exp(s_ctx - m_ctx)
        den = jnp.sum(p_ctx, axis=-1, keepdims=True)
        o_ref[...] = (_dot(p_ctx.astype(BF16), vc) / den).astype(o_ref.dtype)


def _na_attention(p, bias, *, n_lat, n_ctx, heads):
    rows = n_lat + n_ctx
    nb = n_lat // ATT_BLOCK
    nblk = rows // ATT_BLOCK

    def pattern(b):
        return jnp.where(b == 0, 0, jnp.where(b == nb - 1, 2, 1))

    return pl.pallas_call(
        functools.partial(_na_kernel, n_lat=n_lat, n_ctx=n_ctx, nb=nb),
        grid=(heads, nblk),
        in_specs=[
            pl.BlockSpec((ATT_BLOCK, HEAD_DIM), lambda h, b: (b, h)),
            pl.BlockSpec((rows, HEAD_DIM), lambda h, b: (0, heads + h)),
            pl.BlockSpec((rows, HEAD_DIM), lambda h, b: (0, 2 * heads + h)),
            pl.BlockSpec((1, 1, ATT_BLOCK, NA_K_ROWS * GRID_W), lambda h, b: (pattern(b), h, 0, 0)),
        ],
        out_specs=pl.BlockSpec((ATT_BLOCK, HEAD_DIM), lambda h, b: (b, h)),
        out_shape=jax.ShapeDtypeStruct((rows, heads * HEAD_DIM), BF16),
        compiler_params=_params("arbitrary", "arbitrary"),
        name="na_attention",
    )(p, p, p, bias)


def _hgrn_levels():
    levels, m = [], 1
    while m < HG_CHUNK:
        levels.append(m)
        m *= 2
    return levels


def _hgrn_consts():
    c = HG_CHUNK
    levels = _hgrn_levels()
    idx = np.arange(c)
    t, j = idx[:, None], idx[None, :]
    n = np.zeros((2, 2 + len(levels), c, c), np.float32)
    msk = np.zeros((2, 1 + len(levels), c, c), np.float32)
    n[0, 0], n[0, 1] = j <= t, j > t
    n[1, 0], n[1, 1] = j >= t, j < t
    msk[0, 0] = msk[1, 0] = np.eye(c)
    for li, m in enumerate(levels):
        seg = idx // (2 * m)
        right = ((idx % (2 * m)) >= m)[:, None]
        last_left = (seg * 2 * m + m - 1)[:, None]
        first_right = last_left + 1
        n[0, 2 + li] = np.where(right, (j > last_left) & (j <= t), (j > t) & (j <= last_left))
        n[1, 2 + li] = np.where(right, (j >= first_right) & (j < t), (j >= t) & (j < first_right))
        same = seg[:, None] == seg[None, :]
        msk[0, 1 + li] = same & right & ~right.T
        msk[1, 1 + li] = same & ~right & right.T
    return n, msk


def _split3(x):
    hi = x.astype(BF16)
    r = x - hi.astype(F32)
    mid = r.astype(BF16)
    lo = (r - mid.astype(F32)).astype(BF16)
    return hi, mid, lo


def _hgrn_kernel(q_ref, f_ref, i_ref, lbl_ref, n_ref, msk_ref, o_ref, s_scr, *, layer_e, n_levels):
    @pl.when(pl.program_id(2) == 0)
    def _():
        s_scr[...] = jnp.zeros_like(s_scr)

    lg = lbl_ref[0]
    ex = jnp.exp(lg - jnp.max(lg, axis=0, keepdims=True))
    num = jnp.zeros_like(ex[0:1])
    for e in range(1, layer_e + 1):
        num = num + ex[e:e + 1]
    lb = num / jnp.sum(ex, axis=0, keepdims=True)

    fx = f_ref[...]
    log_sig = jnp.minimum(fx, 0.0) - jnp.log1p(jnp.exp(-jnp.abs(fx)))
    a = jnp.log(lb)
    c = jnp.log1p(-lb) + log_sig
    log_f = jnp.maximum(a, c) + jnp.log1p(jnp.exp(-jnp.abs(a - c)))
    kk = (1.0 - lb) * _sigmoid(-fx)
    q = q_ref[...]
    vf = i_ref[...]
    v = vf.astype(BF16)

    g3 = _split3(log_f)

    def decay(r):
        nm = n_ref[0, r]
        return jnp.exp(_dot(nm, g3[0]) + _dot(nm, g3[1]) + _dot(nm, g3[2]))

    attn = _dot_nt(q.astype(BF16), kk.astype(BF16)) * msk_ref[0, 0]
    for li in range(n_levels):
        e = decay(2 + li)
        attn += _dot_nt((q * e).astype(BF16), (kk * e).astype(BF16)) * msk_ref[0, 1 + li]

    state_t = s_scr[...]
    o = _dot_nt((q * decay(0)).astype(BF16), state_t.astype(BF16)) + _dot(attn.astype(BF16), v)
    o_ref[0] = o

    total = jnp.sum(log_f, axis=0, keepdims=True)
    s_scr[...] = jnp.exp(total) * state_t + _dot(vf.T.astype(BF16), (kk * decay(1)).astype(BF16))


def _hgrn_scan(p, lb_logits, consts, *, layer_e, n_lat, n_ctx, heads, col0):
    rows = n_lat + n_ctx
    n_mat, msk = consts
    n_levels = msk.shape[1] - 1
    lat_chunks = n_lat // HG_CHUNK
    steps = 1 + lat_chunks
    c0 = col0 // HEAD_DIM

    def blk(d, s):
        return jnp.where(s == 0, lat_chunks, jnp.where(d == 0, s - 1, lat_chunks - s))

    return pl.pallas_call(
        functools.partial(_hgrn_kernel, layer_e=layer_e, n_levels=n_levels),
        grid=(heads, 2, steps),
        in_specs=[
            pl.BlockSpec((HG_CHUNK, HEAD_DIM), lambda h, d, s: (blk(d, s), c0 + h)),
            pl.BlockSpec((HG_CHUNK, HEAD_DIM), lambda h, d, s: (blk(d, s), c0 + (1 + d) * heads + h)),
            pl.BlockSpec((HG_CHUNK, HEAD_DIM), lambda h, d, s: (blk(d, s), c0 + 3 * heads + h)),
            pl.BlockSpec((1, lb_logits.shape[1], HEAD_DIM), lambda h, d, s: (d, 0, h)),
            pl.BlockSpec((1,) + n_mat.shape[1:], lambda h, d, s: (d, 0, 0, 0)),
            pl.BlockSpec((1,) + msk.shape[1:], lambda h, d, s: (d, 0, 0, 0)),
        ],
        out_specs=pl.BlockSpec((1, HG_CHUNK, HEAD_DIM), lambda h, d, s: (d, blk(d, s), h)),
        out_shape=jax.ShapeDtypeStruct((2, rows, heads * HEAD_DIM), F32),
        scratch_shapes=[pltpu.VMEM((HEAD_DIM, HEAD_DIM), F32)],
        compiler_params=_params("arbitrary", "arbitrary", "arbitrary"),
        name="hgrn_scan",
    )(p, p, p, lb_logits, n_mat, msk)


def _hgrn_out_kernel(o_ref, g_ref, ng_ref, y_ref):
    o = o_ref[0] + o_ref[1]
    o = o * lax.rsqrt(jnp.mean(o * o, axis=-1, keepdims=True) + NORM_EPS)
    y_ref[...] = (o * ng_ref[...] * _silu(g_ref[...])).astype(y_ref.dtype)


def _hgrn_out(o2, p, norm_g, *, heads, gcol0, tm):
    rows = o2.shape[1]
    return pl.pallas_call(
        _hgrn_out_kernel,
        grid=(rows // tm, heads),
        in_specs=[
            pl.BlockSpec((2, tm, HEAD_DIM), lambda i, h: (0, i, h)),
            pl.BlockSpec((tm, HEAD_DIM), lambda i, h: (i, gcol0 // HEAD_DIM + h)),
            pl.BlockSpec((1, HEAD_DIM), lambda i, h: (0, h)),
        ],
        out_specs=pl.BlockSpec((tm, HEAD_DIM), lambda i, h: (i, h)),
        out_shape=jax.ShapeDtypeStruct((rows, heads * HEAD_DIM), BF16),
        compiler_params=_params("arbitrary", "arbitrary"),
        name="hgrn_out",
    )(o2, p, norm_g)


def _swa_kernel(q_ref, k_ref, v_ref, sink_ref, o_ref, *, n_lat, n_ctx, nb, group, span):
    b = pl.program_id(1)
    scale = HEAD_DIM ** -0.5
    kc = k_ref[n_lat:n_lat + n_ctx, :].astype(BF16)
    vc = v_ref[n_lat:n_lat + n_ctx, :].astype(BF16)

    def finish(g, s_ctx, s_loc, vl):
        sink = sink_ref[0, g:g + 1, 0:1]
        m = jnp.maximum(jnp.max(s_ctx, axis=-1, keepdims=True), sink)
        if s_loc is not None:
            m = jnp.maximum(m, jnp.max(s_loc, axis=-1, keepdims=True))
        p_ctx = jnp.exp(s_ctx - m)
        den = jnp.sum(p_ctx, axis=-1, keepdims=True) + jnp.exp(sink - m)
        o = _dot(p_ctx.astype(BF16), vc)
        if s_loc is not None:
            p_loc = jnp.exp(s_loc - m)
            den = den + jnp.sum(p_loc, axis=-1, keepdims=True)
            o = o + _dot(p_loc.astype(BF16), vl)
        o_ref[:, g * HEAD_DIM:(g + 1) * HEAD_DIM] = (o / den).astype(o_ref.dtype)

    @pl.when(b < nb)
    def _():
        start = pl.multiple_of(jnp.clip(b * ATT_BLOCK - WINDOW, 0, n_lat - span), WINDOW)
        kl = k_ref[pl.ds(start, span), :].astype(BF16)
        vl = v_ref[pl.ds(start, span), :].astype(BF16)
        qpos = b * ATT_BLOCK + lax.broadcasted_iota(jnp.int32, (ATT_BLOCK, span), 0)
        kpos = start + lax.broadcasted_iota(jnp.int32, (ATT_BLOCK, span), 1)
        valid = jnp.abs(kpos - qpos) <= WINDOW
        for g in range(group):
            q = q_ref[:, g * HEAD_DIM:(g + 1) * HEAD_DIM].astype(BF16)
            s_loc = jnp.where(valid, _dot_nt(q, kl) * scale, MASKED)
            finish(g, _dot_nt(q, kc) * scale, s_loc, vl)

    @pl.when(b >= nb)
    def _():
        for g in range(group):
            q = q_ref[:, g * HEAD_DIM:(g + 1) * HEAD_DIM].astype(BF16)
            finish(g, _dot_nt(q, kc) * scale, None, None)


def _swa_attention(p, sink, *, n_lat, n_ctx, heads, kv_heads):
    rows = n_lat + n_ctx
    group = heads // kv_heads
    nb = n_lat // ATT_BLOCK
    span = ATT_BLOCK + 2 * WINDOW
    sink_tab = jnp.broadcast_to(sink.astype(F32).reshape(kv_heads, group, 1), (kv_heads, group, HEAD_DIM))
    return pl.pallas_call(
        functools.partial(_swa_kernel, n_lat=n_lat, n_ctx=n_ctx, nb=nb, group=group, span=span),
        grid=(kv_heads, rows // ATT_BLOCK),
        in_specs=[
            pl.BlockSpec((ATT_BLOCK, group * HEAD_DIM), lambda k, b: (b, k)),
            pl.BlockSpec((rows, HEAD_DIM), lambda k, b: (0, heads + k)),
            pl.BlockSpec((rows, HEAD_DIM), lambda k, b: (0, heads + kv_heads + k)),
            pl.BlockSpec((1, group, HEAD_DIM), lambda k, b: (k, 0, 0)),
        ],
        out_specs=pl.BlockSpec((ATT_BLOCK, group * HEAD_DIM), lambda k, b: (b, k)),
        out_shape=jax.ShapeDtypeStruct((rows, heads * HEAD_DIM), BF16),
        compiler_params=_params("arbitrary", "arbitrary"),
        name="swa_attention",
    )(p, p, p, sink_tab)


def kernel(x, c, ctx, c_ctx, w_mod, b_mod, norm_g, w_ff_in, w_ff_out, w_in_even, w_out_even,
           na_rpb, hg_lb_logits, hg_norm_g, w_qkv_odd, w_o_odd, sink_odd, final_norm_g):
    assert x.shape[0] == 1
    depth, d = w_mod.shape[0], x.shape[2]
    n_lat, n_ctx = x.shape[1], ctx.shape[1]
    rows = n_lat + n_ctx
    tm = 768
    assert rows % tm == 0 and n_lat % 512 == 0 and n_ctx == ATT_BLOCK
    na_heads = na_rpb.shape[1]
    a_w = na_heads * HEAD_DIM
    hg_heads = hg_norm_g.shape[1] // HEAD_DIM
    heads = sink_odd.shape[1]
    kv_heads = (w_qkv_odd.shape[2] // HEAD_DIM - heads) // 2

    h = jnp.concatenate([x[0], ctx[0]], axis=0)
    cc = jnp.zeros((8, d), F32).at[0].set(c[0]).at[1].set(c_ctx)
    mod_all = _modulation(cc, w_mod, b_mod)[:, :2].reshape(depth, 2, 9, d)

    cos, sin = _rope_tables(n_lat, n_ctx)
    ai, bj, ok = _na_bias_index(n_lat // GRID_W)
    hg_consts = _hgrn_consts()
    hg_consts = (jnp.asarray(hg_consts[0], BF16), jnp.asarray(hg_consts[1], F32))
    fg = final_norm_g.reshape(1, d)

    for l in range(depth):
        mod = mod_all[l]
        last = l == depth - 1
        h = _ffn(h, mod, norm_g[l, 0].reshape(1, d), w_ff_in[l, 0].astype(BF16),
                 w_ff_out[l, 0].astype(BF16), fg, sub=0, rows=rows, tm=tm, n_lat=n_lat)
        g1 = norm_g[l, 1].reshape(1, d)
        if l % 2 == 0:
            e = l // 2
            p = _proj(h, mod, g1, w_in_even[e].astype(BF16), cos, sin, tm=tm, n_lat=n_lat, rope_cols=0)
            bias = jnp.where(ok[:, None], na_rpb[e][:, ai, bj].transpose(1, 0, 2, 3), MASKED)
            ya = _na_attention(p, bias, n_lat=n_lat, n_ctx=n_ctx, heads=na_heads)
            o2 = _hgrn_scan(p, hg_lb_logits, hg_consts, layer_e=e, n_lat=n_lat, n_ctx=n_ctx,
                            heads=hg_heads, col0=3 * a_w)
            yb = _hgrn_out(o2, p, hg_norm_g[e].reshape(1, -1), heads=hg_heads,
                           gcol0=3 * a_w + 4 * hg_heads * HEAD_DIM, tm=tm)
            h = _oproj(ya, yb, 0, w_out_even[e].astype(BF16), h, mod, tm=tm, n_lat=n_lat)
        else:
            o = l // 2
            p = _proj(h, mod, g1, w_qkv_odd[o].astype(BF16), cos, sin, tm=tm, n_lat=n_lat,
                      rope_cols=(heads + kv_heads) * HEAD_DIM)
            y = _swa_attention(p, sink_odd[o], n_lat=n_lat, n_ctx=n_ctx, heads=heads, kv_heads=kv_heads)
            h = _oproj(y, y, 1, w_o_odd[o].astype(BF16), h, mod, tm=tm, n_lat=n_lat)
        g2 = norm_g[l, 2].reshape(1, d)
        if last:
            h = _ffn(h, mod, g2, w_ff_in[l, 1].astype(BF16), w_ff_out[l, 1].astype(BF16), fg,
                     sub=2, rows=n_lat, tm=512, n_lat=n_lat, final_norm=True)
        else:
            h = _ffn(h, mod, g2, w_ff_in[l, 1].astype(BF16), w_ff_out[l, 1].astype(BF16), fg,
                     sub=2, rows=rows, tm=tm, n_lat=n_lat)
    return h[None]
```

```python
import functools

import numpy as np
import jax
import jax.numpy as jnp
from jax import lax
from jax.experimental import pallas as pl
from jax.experimental.pallas import tpu as pltpu

F32 = jnp.float32
BF16 = jnp.bfloat16

GRID_W = 64
NORM_EPS = 1e-6
ROPE_THETA = 10000.0
HEAD_DIM = 128
NA_ROWS = 8
NA_COLS = 16
NA_Q_ROWS = 4
NA_K_ROWS = NA_Q_ROWS + NA_ROWS
ATT_BLOCK = NA_Q_ROWS * GRID_W
WINDOW = 128
HG_CHUNK = 256
HG_ROW_LEVEL = 8
MASKED = -1e30
VMEM_LIMIT = 56 * 1024 * 1024


def _params(*sem):
    return pltpu.CompilerParams(dimension_semantics=sem, vmem_limit_bytes=VMEM_LIMIT)


def _dot(a, b):
    return jnp.dot(a, b, preferred_element_type=F32)


def _dot_nt(a, b):
    return lax.dot_general(a, b, (((1,), (1,)), ((), ())), preferred_element_type=F32)


def _sigmoid(x):
    return 1.0 / (1.0 + jnp.exp(-x))


def _silu(x):
    return x * _sigmoid(x)


def _mod_kernel(c_ref, w_ref, b_ref, o_ref):
    s = _silu(c_ref[...]).astype(BF16)
    o_ref[0] = _dot(s, w_ref[0].astype(BF16)) + b_ref[0]


def _modulation(cc, w_mod, b_mod):
    depth, d, n = w_mod.shape
    tn = 1024
    return pl.pallas_call(
        _mod_kernel,
        grid=(depth, n // tn),
        in_specs=[
            pl.BlockSpec((8, d), lambda l, j: (0, 0)),
            pl.BlockSpec((1, d, tn), lambda l, j: (l, 0, j)),
            pl.BlockSpec((1, 1, tn), lambda l, j: (l, 0, j)),
        ],
        out_specs=pl.BlockSpec((1, 8, tn), lambda l, j: (l, 0, j)),
        out_shape=jax.ShapeDtypeStruct((depth, 8, n), F32),
        compiler_params=_params("arbitrary", "arbitrary"),
        name="modulation",
    )(cc, w_mod, b_mod.reshape(depth, 1, n))


def _mod_rows(mod_ref, sub, k, is_ctx):
    r = 3 * sub + k
    return jnp.where(is_ctx, mod_ref[1, r:r + 1, :], mod_ref[0, r:r + 1, :])


def _is_ctx(tile, tm, n_lat):
    row = tile * tm + lax.broadcasted_iota(jnp.int32, (tm, 1), 0)
    return row >= n_lat


def _adaln(h, g, shift, scale):
    y = h * lax.rsqrt(jnp.mean(h * h, axis=-1, keepdims=True) + NORM_EPS)
    return (y * g) * (1.0 + scale) + shift


def _ffn_kernel(h_ref, mod_ref, g_ref, wg_ref, wu_ref, wo_ref, fg_ref, o_ref, u_scr,
                *, sub, tm, n_lat, final_norm):
    i, j = pl.program_id(0), pl.program_id(1)
    is_ctx = _is_ctx(i, tm, n_lat)

    @pl.when(j == 0)
    def _():
        u = _adaln(h_ref[...], g_ref[...], _mod_rows(mod_ref, sub, 0, is_ctx),
                   _mod_rows(mod_ref, sub, 1, is_ctx))
        u_scr[...] = u.astype(BF16)
        o_ref[...] = jnp.zeros_like(o_ref)

    u = u_scr[...]
    gate = _dot(u, wg_ref[...])
    up = _dot(u, wu_ref[...])
    o_ref[...] += _dot((_silu(gate) * up).astype(BF16), wo_ref[...])

    @pl.when(j == pl.num_programs(1) - 1)
    def _():
        h = h_ref[...] + (0.5 * _mod_rows(mod_ref, sub, 2, is_ctx)) * o_ref[...]
        if final_norm:
            h = h * lax.rsqrt(jnp.mean(h * h, axis=-1, keepdims=True) + NORM_EPS) * fg_ref[...]
        o_ref[...] = h


def _ffn(h, mod, g, w_in, w_out, fg, *, layer, half, sub, rows, tm, n_lat, final_norm=False):
    d = h.shape[1]
    f = w_out.shape[2]
    tf = 256
    nf = f // tf
    return pl.pallas_call(
        functools.partial(_ffn_kernel, sub=sub, tm=tm, n_lat=n_lat, final_norm=final_norm),
        grid=(rows // tm, nf),
        in_specs=[
            pl.BlockSpec((tm, d), lambda i, j: (i, 0)),
            pl.BlockSpec((2, 9, d), lambda i, j: (0, 0, 0)),
            pl.BlockSpec((1, d), lambda i, j: (0, 0)),
            pl.BlockSpec((None, None, d, tf), lambda i, j: (layer, half, 0, j)),
            pl.BlockSpec((None, None, d, tf), lambda i, j: (layer, half, 0, nf + j)),
            pl.BlockSpec((None, None, tf, d), lambda i, j: (layer, half, j, 0)),
            pl.BlockSpec((1, d), lambda i, j: (0, 0)),
        ],
        out_specs=pl.BlockSpec((tm, d), lambda i, j: (i, 0)),
        out_shape=jax.ShapeDtypeStruct((rows, d), F32),
        scratch_shapes=[pltpu.VMEM((tm, d), BF16)],
        compiler_params=_params("arbitrary", "arbitrary"),
        name="ffn",
    )(h, mod, g, w_in, w_in, w_out, fg)


def _rope(x, cos, sin):
    lane = lax.broadcasted_iota(jnp.int32, x.shape, 1)
    first = (lane % 64) < 32
    swapped = jnp.where(first, pltpu.roll(x, 96, 1), pltpu.roll(x, 32, 1))
    return x * cos + swapped * sin


def _proj_kernel(h_ref, mod_ref, g_ref, w_ref, cos_ref, sin_ref, o_ref, u_scr,
                 *, tm, tn, n_lat, rope_tiles):
    i, j = pl.program_id(0), pl.program_id(1)

    @pl.when(j == 0)
    def _():
        is_ctx = _is_ctx(i, tm, n_lat)
        u = _adaln(h_ref[...], g_ref[...], _mod_rows(mod_ref, 1, 0, is_ctx),
                   _mod_rows(mod_ref, 1, 1, is_ctx))
        u_scr[...] = u.astype(BF16)

    y = _dot(u_scr[...], w_ref[...])
    if rope_tiles == 0:
        o_ref[...] = y
    else:
        @pl.when(j < rope_tiles)
        def _():
            cos, sin = cos_ref[...], sin_ref[...]
            for hd in range(tn // HEAD_DIM):
                sl = slice(hd * HEAD_DIM, (hd + 1) * HEAD_DIM)
                o_ref[:, sl] = _rope(y[:, sl], cos, sin)

        @pl.when(j >= rope_tiles)
        def _():
            o_ref[...] = y


def _proj(h, mod, g, w, cos, sin, *, layer, tm, n_lat, rope_cols):
    rows, d = h.shape
    n = w.shape[2]
    tn = 512
    return pl.pallas_call(
        functools.partial(_proj_kernel, tm=tm, tn=tn, n_lat=n_lat, rope_tiles=rope_cols // tn),
        grid=(rows // tm, n // tn),
        in_specs=[
            pl.BlockSpec((tm, d), lambda i, j: (i, 0)),
            pl.BlockSpec((2, 9, d), lambda i, j: (0, 0, 0)),
            pl.BlockSpec((1, d), lambda i, j: (0, 0)),
            pl.BlockSpec((None, d, tn), lambda i, j: (layer, 0, j)),
            pl.BlockSpec((tm, HEAD_DIM), lambda i, j: (i, 0)),
            pl.BlockSpec((tm, HEAD_DIM), lambda i, j: (i, 0)),
        ],
        out_specs=pl.BlockSpec((tm, tn), lambda i, j: (i, j)),
        out_shape=jax.ShapeDtypeStruct((rows, n), F32),
        scratch_shapes=[pltpu.VMEM((tm, d), BF16)],
        compiler_params=_params("arbitrary", "arbitrary"),
        name="proj",
    )(h, mod, g, w, cos, sin)


def _rope_tables(n_lat, n_ctx):
    t = np.arange(n_lat)
    inv = ROPE_THETA ** (-jnp.arange(0, 64, 2, dtype=F32) / 64)
    ang_r = jnp.asarray(t // GRID_W, F32)[:, None] * inv[None, :]
    ang_c = jnp.asarray(t % GRID_W, F32)[:, None] * inv[None, :]
    cr, sr, cc, sc = jnp.cos(ang_r), jnp.sin(ang_r), jnp.cos(ang_c), jnp.sin(ang_c)
    cos = jnp.concatenate([cr, cr, cc, cc], axis=1)
    sin = jnp.concatenate([-sr, sr, -sc, sc], axis=1)
    cos = jnp.concatenate([cos, jnp.ones((n_ctx, HEAD_DIM), F32)], axis=0)
    sin = jnp.concatenate([sin, jnp.zeros((n_ctx, HEAD_DIM), F32)], axis=0)
    return cos, sin


def _oproj_kernel(a1_ref, a2_ref, w_ref, h_ref, mod_ref, o_ref, *, tm, n_lat, k1):
    is_ctx = _is_ctx(pl.program_id(0), tm, n_lat)
    y = _dot(a1_ref[...], w_ref[:k1, :]) + _dot(a2_ref[...], w_ref[k1:, :])
    o_ref[...] = h_ref[...] + _mod_rows(mod_ref, 1, 2, is_ctx) * y


def _oproj(a1, a2, col2, w, h, mod, *, layer, tm, n_lat):
    rows, d = h.shape
    k1 = w.shape[1] // 2
    return pl.pallas_call(
        functools.partial(_oproj_kernel, tm=tm, n_lat=n_lat, k1=k1),
        grid=(rows // tm,),
        in_specs=[
            pl.BlockSpec((tm, k1), lambda i: (i, 0)),
            pl.BlockSpec((tm, k1), lambda i: (i, col2)),
            pl.BlockSpec((None,) + w.shape[1:], lambda i: (layer, 0, 0)),
            pl.BlockSpec((tm, d), lambda i: (i, 0)),
            pl.BlockSpec((2, 9, d), lambda i: (0, 0, 0)),
        ],
        out_specs=pl.BlockSpec((tm, d), lambda i: (i, 0)),
        out_shape=jax.ShapeDtypeStruct((rows, d), F32),
        compiler_params=_params("arbitrary"),
        name="oproj",
    )(a1, a2, w, h, mod)


def _na_bias(rpb, n_rows):
    nh, n_a, n_b = rpb.shape
    nb = n_rows // NA_Q_ROWS
    w = jnp.full((nh, n_a, 128), MASKED, F32)
    w = w.at[..., :NA_COLS].set(rpb[..., NA_COLS - 1:]).at[..., 128 - (NA_COLS - 1):].set(rpb[..., :NA_COLS - 1])
    toep = jnp.tile(w, (1, 1, GRID_W))[..., :GRID_W * 127].reshape(nh, n_a, GRID_W, 127)[..., :GRID_W]
    toep = jnp.pad(toep, ((0, 0), (NA_K_ROWS, NA_K_ROWS), (0, 0), (0, 0)), constant_values=MASKED)

    q = np.arange(ATT_BLOCK)
    k = np.arange(NA_K_ROWS * GRID_W)
    dr, c = q // GRID_W, q % GRID_W
    kr, kc = k // GRID_W, k % GRID_W
    c0 = np.clip(c - NA_COLS // 2, 0, GRID_W - NA_COLS)
    col_ok = (kc[None, :] >= c0[:, None]) & (kc[None, :] < c0[:, None] + NA_COLS)
    tables, ok = [], []
    for b in (0, 1, nb - 1):
        r = b * NA_Q_ROWS + dr
        r0 = np.clip(r - NA_ROWS // 2, 0, n_rows - NA_ROWS)
        base = int(np.clip(b * NA_Q_ROWS - NA_ROWS // 2, 0, n_rows - NA_K_ROWS))
        kabs = base + kr
        ok.append((kabs[None, :] >= r0[:, None]) & (kabs[None, :] < r0[:, None] + NA_ROWS) & col_ok)
        per_row = []
        for j in range(NA_Q_ROWS):
            lo = base - (b * NA_Q_ROWS + j) + NA_ROWS - 1 + NA_K_ROWS
            blk = toep[:, lo:lo + NA_K_ROWS]
            per_row.append(blk.transpose(0, 2, 1, 3).reshape(nh, GRID_W, NA_K_ROWS * GRID_W))
        tables.append(jnp.stack(per_row, axis=1).reshape(nh, ATT_BLOCK, NA_K_ROWS * GRID_W))
    return jnp.where(np.stack(ok)[:, None], jnp.stack(tables), MASKED)


def _na_kernel(q_ref, k_ref, v_ref, bias_ref, o_ref, *, n_lat, n_ctx, nb):
    b = pl.program_id(1)
    scale = HEAD_DIM ** -0.5
    q = q_ref[...].astype(BF16)
    kc = k_ref[n_lat:n_lat + n_ctx, :].astype(BF16)
    vc = v_ref[n_lat:n_lat + n_ctx, :].astype(BF16)
    s_ctx = _dot_nt(q, kc) * scale
    m_ctx = jnp.max(s_ctx, axis=-1, keepdims=True)

    @pl.when(b < nb)
    def _():
        nk = NA_K_ROWS * GRID_W
        start = pl.multiple_of(jnp.clip(b - 1, 0, nb - 3) * ATT_BLOCK, ATT_BLOCK)
        kl = k_ref[pl.ds(start, nk), :].astype(BF16)
        vl = v_ref[pl.ds(start, nk), :].astype(BF16)
        s_loc = _dot_nt(q, kl) * scale + bias_ref[0, 0]
        m = jnp.maximum(jnp.max(s_loc, axis=-1, keepdims=True), m_ctx)
        p_loc = jnp.exp(s_loc - m)
        p_ctx = jnp.exp(s_ctx - m)
        den = jnp.sum(p_loc, axis=-1, keepdims=True) + jnp.sum(p_ctx, axis=-1, keepdims=True)
        o = _dot(p_loc.astype(BF16), vl) + _dot(p_ctx.astype(BF16), vc)
        o_ref[...] = (o / den).astype(o_ref.dtype)

    @pl.when(b >= nb)
    def _():
        p_ctx = jnp.exp(s_ctx - m_ctx)
        den = jnp.sum(p_ctx, axis=-1, keepdims=True)
        o_ref[...] = (_dot(p_ctx.astype(BF16), vc) / den).astype(o_ref.dtype)


def _na_attention(p, bias, *, n_lat, n_ctx, heads):
    rows = n_lat + n_ctx
    nb = n_lat // ATT_BLOCK
    nblk = rows // ATT_BLOCK

    def pattern(b):
        return jnp.where(b == 0, 0, jnp.where(b == nb - 1, 2, 1))

    return pl.pallas_call(
        functools.partial(_na_kernel, n_lat=n_lat, n_ctx=n_ctx, nb=nb),
        grid=(heads, nblk),
        in_specs=[
            pl.BlockSpec((ATT_BLOCK, HEAD_DIM), lambda h, b: (b, h)),
            pl.BlockSpec((rows, HEAD_DIM), lambda h, b: (0, heads + h)),
            pl.BlockSpec((rows, HEAD_DIM), lambda h, b: (0, 2 * heads + h)),
            pl.BlockSpec((1, 1, ATT_BLOCK, NA_K_ROWS * GRID_W), lambda h, b: (pattern(b), h, 0, 0)),
        ],
        out_specs=pl.BlockSpec((ATT_BLOCK, HEAD_DIM), lambda h, b: (b, h)),
        out_shape=jax.ShapeDtypeStruct((rows, heads * HEAD_DIM), BF16),
        compiler_params=_params("arbitrary", "arbitrary"),
        name="na_attention",
    )(p, p, p, bias)


def _hgrn_levels():
    levels, m = [], 1
    while m < HG_CHUNK:
        levels.append(m)
        m *= 2
    return levels


def _hgrn_consts():
    c = HG_CHUNK
    levels = _hgrn_levels()
    fine = [m for m in levels if m < HG_ROW_LEVEL]
    idx = np.arange(c)
    t, j = idx[:, None], idx[None, :]
    n = np.zeros((2, 1 + len(fine), c, c), np.float32)
    msk = np.zeros((2, 1 + len(levels), c, c), np.float32)
    n[0, 0], n[1, 0] = j <= t, j >= t
    msk[0, 0] = msk[1, 0] = np.eye(c)
    for li, m in enumerate(levels):
        seg = idx // (2 * m)
        right = ((idx % (2 * m)) >= m)[:, None]
        last_left = (seg * 2 * m + m - 1)[:, None]
        first_right = last_left + 1
        if m < HG_ROW_LEVEL:
            n[0, 1 + li] = np.where(right, (j > last_left) & (j <= t), (j > t) & (j <= last_left))
            n[1, 1 + li] = np.where(right, (j >= first_right) & (j < t), (j >= t) & (j < first_right))
        same = seg[:, None] == seg[None, :]
        msk[0, 1 + li] = same & right & ~right.T
        msk[1, 1 + li] = same & ~right & right.T
    return n.reshape(2, -1, c), msk


def _hgrn_chunk(d, q, fx, vf, lb, n_ref, msk_ref, s_scr, b_scr):
    c = HG_CHUNK
    log_sig = jnp.minimum(fx, 0.0) - jnp.log1p(jnp.exp(-jnp.abs(fx)))
    la = jnp.log(lb)
    lc = jnp.log1p(-lb) + log_sig
    log_f = jnp.maximum(la, lc) + jnp.log1p(jnp.exp(-jnp.abs(la - lc)))
    kk = (1.0 - lb) * _sigmoid(-fx)
    v = vf.astype(BF16)

    hi = log_f.astype(BF16)
    mid = (log_f - hi.astype(F32)).astype(BF16)
    dd = _dot(n_ref[d], jnp.concatenate([hi, mid], axis=1))
    dd = dd[:, :HEAD_DIM] + dd[:, HEAD_DIM:]
    b = dd[:c]
    b_scr[d] = b

    attn = _dot_nt(q.astype(BF16), kk.astype(BF16)) * msk_ref[d, 0]
    for li, m in enumerate(_hgrn_levels()):
        if m < HG_ROW_LEVEL:
            ex = dd[(1 + li) * c:(2 + li) * c]
        else:
            row = m - 1 + d
            ref = jnp.concatenate(
                [jnp.broadcast_to(b_scr[d, s + row:s + row + 1, :], (2 * m, HEAD_DIM)) for s in range(0, c, 2 * m)],
                axis=0)
            ex = -jnp.abs(b - ref)
        e = jnp.exp(ex)
        attn += _dot_nt((q * e).astype(BF16), (kk * e).astype(BF16)) * msk_ref[d, 1 + li]

    total = b[c - 1:c] if d == 0 else b[0:1]
    state_t = s_scr[d]
    o = _dot_nt((q * jnp.exp(b)).astype(BF16), state_t.astype(BF16)) + _dot(attn.astype(BF16), v)
    s_scr[d] = jnp.exp(total) * state_t + _dot(vf.T.astype(BF16), (kk * jnp.exp(total - b)).astype(BF16))
    return o


def _hgrn_kernel(qf_ref, qb_ref, ff_ref, fb_ref, if_ref, ib_ref, lbl_ref, n_ref, msk_ref,
                 of_ref, ob_ref, s_scr, b_scr, *, layer_e):
    @pl.when(pl.program_id(1) == 0)
    def _():
        s_scr[...] = jnp.zeros_like(s_scr)

    for d, (q_ref, f_ref, i_ref, o_ref) in enumerate(((qf_ref, ff_ref, if_ref, of_ref),
                                                      (qb_ref, fb_ref, ib_ref, ob_ref))):
        lg = lbl_ref[d]
        ex = jnp.exp(lg - jnp.max(lg, axis=0, keepdims=True))
        num = jnp.zeros_like(ex[0:1])
        for e in range(1, layer_e + 1):
            num = num + ex[e:e + 1]
        lb = num / jnp.sum(ex, axis=0, keepdims=True)
        o_ref[...] = _hgrn_chunk(d, q_ref[...], f_ref[...], i_ref[...], lb, n_ref, msk_ref, s_scr, b_scr)


def _hgrn_scan(p, lb_logits, consts, *, layer_e, n_lat, n_ctx, heads, col0):
    rows = n_lat + n_ctx
    n_mat, msk = consts
    lat_chunks = n_lat // HG_CHUNK
    c0 = col0 // HEAD_DIM

    def fwd(s):
        return jnp.where(s == 0, lat_chunks, s - 1)

    def bwd(s):
        return lat_chunks - s

    def spec(blk, col):
        return pl.BlockSpec((HG_CHUNK, HEAD_DIM), lambda h, s: (blk(s), col + h))

    out = jax.ShapeDtypeStruct((rows, heads * HEAD_DIM), F32)
    return pl.pallas_call(
        functools.partial(_hgrn_kernel, layer_e=layer_e),
        grid=(heads, 1 + lat_chunks),
        in_specs=[
            spec(fwd, c0), spec(bwd, c0),
            spec(fwd, c0 + heads), spec(bwd, c0 + 2 * heads),
            spec(fwd, c0 + 3 * heads), spec(bwd, c0 + 3 * heads),
            pl.BlockSpec((2, lb_logits.shape[1], HEAD_DIM), lambda h, s: (0, 0, h)),
            pl.BlockSpec(n_mat.shape, lambda h, s: (0, 0, 0)),
            pl.BlockSpec(msk.shape, lambda h, s: (0, 0, 0, 0)),
        ],
        out_specs=[spec(fwd, 0), spec(bwd, 0)],
        out_shape=[out, out],
        scratch_shapes=[pltpu.VMEM((2, HEAD_DIM, HEAD_DIM), F32), pltpu.VMEM((2, HG_CHUNK, HEAD_DIM), F32)],
        compiler_params=_params("arbitrary", "arbitrary"),
        name="hgrn_scan",
    )(p, p, p, p, p, p, lb_logits, n_mat, msk)


def _hgrn_out_kernel(of_ref, ob_ref, g_ref, ng_ref, y_ref):
    o = of_ref[...] + ob_ref[...]
    o = o * lax.rsqrt(jnp.mean(o * o, axis=-1, keepdims=True) + NORM_EPS)
    y_ref[...] = (o * ng_ref[...] * _silu(g_ref[...])).astype(y_ref.dtype)


def _hgrn_out(o_f, o_b, p, norm_g, *, heads, gcol0, tm):
    rows = o_f.shape[0]
    return pl.pallas_call(
        _hgrn_out_kernel,
        grid=(rows // tm, heads),
        in_specs=[
            pl.BlockSpec((tm, HEAD_DIM), lambda i, h: (i, h)),
            pl.BlockSpec((tm, HEAD_DIM), lambda i, h: (i, h)),
            pl.BlockSpec((tm, HEAD_DIM), lambda i, h: (i, gcol0 // HEAD_DIM + h)),
            pl.BlockSpec((1, HEAD_DIM), lambda i, h: (0, h)),
        ],
        out_specs=pl.BlockSpec((tm, HEAD_DIM), lambda i, h: (i, h)),
        out_shape=jax.ShapeDtypeStruct((rows, heads * HEAD_DIM), BF16),
        compiler_params=_params("arbitrary", "arbitrary"),
        name="hgrn_out",
    )(o_f, o_b, p, norm_g)


def _swa_kernel(q_ref, k_ref, v_ref, sink_ref, o_ref, *, n_lat, n_ctx, nb, group, span):
    b = pl.program_id(1)
    scale = HEAD_DIM ** -0.5
    kc = k_ref[n_lat:n_lat + n_ctx, :].astype(BF16)
    vc = v_ref[n_lat:n_lat + n_ctx, :].astype(BF16)

    def finish(g, s_ctx, s_loc, vl):
        sink = sink_ref[0, g:g + 1, 0:1]
        m = jnp.maximum(jnp.max(s_ctx, axis=-1, keepdims=True), sink)
        if s_loc is not None:
            m = jnp.maximum(m, jnp.max(s_loc, axis=-1, keepdims=True))
        p_ctx = jnp.exp(s_ctx - m)
        den = jnp.sum(p_ctx, axis=-1, keepdims=True) + jnp.exp(sink - m)
        o = _dot(p_ctx.astype(BF16), vc)
        if s_loc is not None:
            p_loc = jnp.exp(s_loc - m)
            den = den + jnp.sum(p_loc, axis=-1, keepdims=True)
            o = o + _dot(p_loc.astype(BF16), vl)
        o_ref[:, g * HEAD_DIM:(g + 1) * HEAD_DIM] = (o / den).astype(o_ref.dtype)

    @pl.when(b < nb)
    def _():
        start = pl.multiple_of(jnp.clip(b * ATT_BLOCK - WINDOW, 0, n_lat - span), WINDOW)
        kl = k_ref[pl.ds(start, span), :].astype(BF16)
        vl = v_ref[pl.ds(start, span), :].astype(BF16)
        qpos = b * ATT_BLOCK + lax.broadcasted_iota(jnp.int32, (ATT_BLOCK, span), 0)
        kpos = start + lax.broadcasted_iota(jnp.int32, (ATT_BLOCK, span), 1)
        valid = jnp.abs(kpos - qpos) <= WINDOW
        for g in range(group):
            q = q_ref[:, g * HEAD_DIM:(g + 1) * HEAD_DIM].astype(BF16)
            s_loc = jnp.where(valid, _dot_nt(q, kl) * scale, MASKED)
            finish(g, _dot_nt(q, kc) * scale, s_loc, vl)

    @pl.when(b >= nb)
    def _():
        for g in range(group):
            q = q_ref[:, g * HEAD_DIM:(g + 1) * HEAD_DIM].astype(BF16)
            finish(g, _dot_nt(q, kc) * scale, None, None)


def _swa_attention(p, sink, *, n_lat, n_ctx, heads, kv_heads):
    rows = n_lat + n_ctx
    group = heads // kv_heads
    nb = n_lat // ATT_BLOCK
    span = ATT_BLOCK + 2 * WINDOW
    sink_tab = jnp.broadcast_to(sink.astype(F32).reshape(kv_heads, group, 1), (kv_heads, group, HEAD_DIM))
    return pl.pallas_call(
        functools.partial(_swa_kernel, n_lat=n_lat, n_ctx=n_ctx, nb=nb, group=group, span=span),
        grid=(kv_heads, rows // ATT_BLOCK),
        in_specs=[
            pl.BlockSpec((ATT_BLOCK, group * HEAD_DIM), lambda k, b: (b, k)),
            pl.BlockSpec((rows, HEAD_DIM), lambda k, b: (0, heads + k)),
            pl.BlockSpec((rows, HEAD_DIM), lambda k, b: (0, heads + kv_heads + k)),
            pl.BlockSpec((1, group, HEAD_DIM), lambda k, b: (k, 0, 0)),
        ],
        out_specs=pl.BlockSpec((ATT_BLOCK, group * HEAD_DIM), lambda k, b: (b, k)),
        out_shape=jax.ShapeDtypeStruct((rows, heads * HEAD_DIM), BF16),
        compiler_params=_params("arbitrary", "arbitrary"),
        name="swa_attention",
    )(p, p, p, sink_tab)


def kernel(x, c, ctx, c_ctx, w_mod, b_mod, norm_g, w_ff_in, w_ff_out, w_in_even, w_out_even,
           na_rpb, hg_lb_logits, hg_norm_g, w_qkv_odd, w_o_odd, sink_odd, final_norm_g):
    assert x.shape[0] == 1
    depth, d = w_mod.shape[0], x.shape[2]
    n_lat, n_ctx = x.shape[1], ctx.shape[1]
    rows = n_lat + n_ctx
    tm = 768
    assert rows % tm == 0 and n_lat % 512 == 0 and n_ctx == ATT_BLOCK
    na_heads = na_rpb.shape[1]
    a_w = na_heads * HEAD_DIM
    hg_heads = hg_norm_g.shape[1] // HEAD_DIM
    heads = sink_odd.shape[1]
    kv_heads = (w_qkv_odd.shape[2] // HEAD_DIM - heads) // 2

    h = jnp.concatenate([x[0], ctx[0]], axis=0)
    cc = jnp.zeros((8, d), F32).at[0].set(c[0]).at[1].set(c_ctx)
    mod_all = _modulation(cc, w_mod, b_mod)[:, :2].reshape(depth, 2, 9, d)

    cos, sin = _rope_tables(n_lat, n_ctx)
    hg_consts = _hgrn_consts()
    hg_consts = (jnp.asarray(hg_consts[0], BF16), jnp.asarray(hg_consts[1], F32))
    fg = final_norm_g.reshape(1, d)
    w_ff_in, w_ff_out = w_ff_in.astype(BF16), w_ff_out.astype(BF16)
    w_in_even, w_out_even = w_in_even.astype(BF16), w_out_even.astype(BF16)
    w_qkv_odd, w_o_odd = w_qkv_odd.astype(BF16), w_o_odd.astype(BF16)

    for l in range(depth):
        mod = mod_all[l]
        last = l == depth - 1
        h = _ffn(h, mod, norm_g[l, 0].reshape(1, d), w_ff_in, w_ff_out, fg, layer=l, half=0,
                 sub=0, rows=rows, tm=tm, n_lat=n_lat)
        g1 = norm_g[l, 1].reshape(1, d)
        if l % 2 == 0:
            e = l // 2
            p = _proj(h, mod, g1, w_in_even, cos, sin, layer=e, tm=tm, n_lat=n_lat, rope_cols=0)
            ya = _na_attention(p, _na_bias(na_rpb[e], n_lat // GRID_W), n_lat=n_lat, n_ctx=n_ctx, heads=na_heads)
            o_f, o_b = _hgrn_scan(p, hg_lb_logits, hg_consts, layer_e=e, n_lat=n_lat, n_ctx=n_ctx,
                                  heads=hg_heads, col0=3 * a_w)
            yb = _hgrn_out(o_f, o_b, p, hg_norm_g[e].reshape(1, -1), heads=hg_heads,
                           gcol0=3 * a_w + 4 * hg_heads * HEAD_DIM, tm=tm)
            h = _oproj(ya, yb, 0, w_out_even, h, mod, layer=e, tm=tm, n_lat=n_lat)
        else:
            o = l // 2
            p = _proj(h, mod, g1, w_qkv_odd, cos, sin, layer=o, tm=tm, n_lat=n_lat,
                      rope_cols=(heads + kv_heads) * HEAD_DIM)
            y = _swa_attention(p, sink_odd[o], n_lat=n_lat, n_ctx=n_ctx, heads=heads, kv_heads=kv_heads)
            h = _oproj(y, y, 1, w_o_odd, h, mod, layer=o, tm=tm, n_lat=n_lat)
        g2 = norm_g[l, 2].reshape(1, d)
        if last:
            h = _ffn(h, mod, g2, w_ff_in, w_ff_out, fg, layer=l, half=1,
                     sub=2, rows=n_lat, tm=512, n_lat=n_lat, final_norm=True)
        else:
            h = _ffn(h, mod, g2, w_ff_in, w_ff_out, fg, layer=l, half=1,
                     sub=2, rows=rows, tm=tm, n_lat=n_lat)
    return h[None]
```

```python
import functools

import numpy as np
import jax
import jax.numpy as jnp
from jax import lax
from jax.experimental import pallas as pl
from jax.experimental.pallas import tpu as pltpu

F32 = jnp.float32
BF16 = jnp.bfloat16

GRID_W = 64
NORM_EPS = 1e-6
ROPE_THETA = 10000.0
HEAD_DIM = 128
NA_ROWS = 8
NA_COLS = 16
NA_Q_ROWS = 4
NA_K_ROWS = NA_Q_ROWS + NA_ROWS
ATT_BLOCK = NA_Q_ROWS * GRID_W
WINDOW = 128
HG_CHUNK = 256
ROW_CHUNK = 16
ROW_UNROLL = 4
HG_ROW_LEVEL = 8
MASKED = -1e30
VMEM_LIMIT = 56 * 1024 * 1024


def _params(*sem):
    return pltpu.CompilerParams(dimension_semantics=sem, vmem_limit_bytes=VMEM_LIMIT)


def _dot(a, b):
    return jnp.dot(a, b, preferred_element_type=F32)


def _dot_nt(a, b):
    return lax.dot_general(a, b, (((1,), (1,)), ((), ())), preferred_element_type=F32)


def _sigmoid(x):
    return 1.0 / (1.0 + jnp.exp(-x))


def _silu(x):
    return x * _sigmoid(x)


def _mod_kernel(c_ref, w_ref, b_ref, o_ref):
    s = _silu(c_ref[...]).astype(BF16)
    o_ref[0] = _dot(s, w_ref[0].astype(BF16)) + b_ref[0]


def _modulation(cc, w_mod, b_mod):
    depth, d, n = w_mod.shape
    tn = 1024
    return pl.pallas_call(
        _mod_kernel,
        grid=(depth, n // tn),
        in_specs=[
            pl.BlockSpec((8, d), lambda l, j: (0, 0)),
            pl.BlockSpec((1, d, tn), lambda l, j: (l, 0, j)),
            pl.BlockSpec((1, 1, tn), lambda l, j: (l, 0, j)),
        ],
        out_specs=pl.BlockSpec((1, 8, tn), lambda l, j: (l, 0, j)),
        out_shape=jax.ShapeDtypeStruct((depth, 8, n), F32),
        compiler_params=_params("arbitrary", "arbitrary"),
        name="modulation",
    )(cc, w_mod, b_mod.reshape(depth, 1, n))


def _mod_rows(mod_ref, sub, k, is_ctx):
    r = 3 * sub + k
    return jnp.where(is_ctx, mod_ref[1, r:r + 1, :], mod_ref[0, r:r + 1, :])


def _is_ctx(tile, tm, n_lat):
    row = tile * tm + lax.broadcasted_iota(jnp.int32, (tm, 1), 0)
    return row >= n_lat


def _row_chunks(tile, tm, n_lat, body):
    def step(c, carry):
        r0 = pl.multiple_of(c * ROW_CHUNK, ROW_CHUNK)
        body(pl.ds(r0, ROW_CHUNK), (tile * tm + r0 >= n_lat).astype(jnp.int32))
        return carry

    lax.fori_loop(0, tm // ROW_CHUNK, step, 0, unroll=ROW_UNROLL)


def _adaln_rows(h_ref, mod_ref, g_ref, u_scr, gs_scr, *, sub, tile, tm, n_lat):
    for which in range(2):
        gs_scr[which] = g_ref[...] * (1.0 + mod_ref[which, 3 * sub + 1:3 * sub + 2, :])

    def body(rows, which):
        x = h_ref[rows, :]
        r = lax.rsqrt(jnp.mean(x * x, axis=-1, keepdims=True) + NORM_EPS)
        u_scr[rows, :] = ((x * r) * gs_scr[which] + mod_ref[which, 3 * sub:3 * sub + 1, :]).astype(BF16)

    _row_chunks(tile, tm, n_lat, body)


def _ffn_kernel(h_ref, mod_ref, g_ref, wg_ref, wu_ref, wo_ref, fg_ref, o_ref, u_scr, gs_scr,
                *, sub, tm, n_lat, final_norm):
    i, j = pl.program_id(0), pl.program_id(1)

    @pl.when(j == 0)
    def _():
        _adaln_rows(h_ref, mod_ref, g_ref, u_scr, gs_scr, sub=sub, tile=i, tm=tm, n_lat=n_lat)
        o_ref[...] = jnp.zeros_like(o_ref)

    u = u_scr[...]
    gate = _dot(u, wg_ref[...])
    up = _dot(u, wu_ref[...])
    o_ref[...] += _dot((_silu(gate) * up).astype(BF16), wo_ref[...])

    @pl.when(j == pl.num_programs(1) - 1)
    def _():
        def body(rows, which):
            h = h_ref[rows, :] + (0.5 * mod_ref[which, 3 * sub + 2:3 * sub + 3, :]) * o_ref[rows, :]
            if final_norm:
                h = h * lax.rsqrt(jnp.mean(h * h, axis=-1, keepdims=True) + NORM_EPS) * fg_ref[...]
            o_ref[rows, :] = h

        _row_chunks(i, tm, n_lat, body)


def _ffn(h, mod, g, w_in, w_out, fg, *, layer, half, sub, rows, tm, n_lat, final_norm=False):
    d = h.shape[1]
    f = w_out.shape[2]
    tf = 256
    nf = f // tf
    return pl.pallas_call(
        functools.partial(_ffn_kernel, sub=sub, tm=tm, n_lat=n_lat, final_norm=final_norm),
        grid=(rows // tm, nf),
        in_specs=[
            pl.BlockSpec((tm, d), lambda i, j: (i, 0)),
            pl.BlockSpec((2, 9, d), lambda i, j: (0, 0, 0)),
            pl.BlockSpec((1, d), lambda i, j: (0, 0)),
            pl.BlockSpec((None, None, d, tf), lambda i, j: (layer, half, 0, j)),
            pl.BlockSpec((None, None, d, tf), lambda i, j: (layer, half, 0, nf + j)),
            pl.BlockSpec((None, None, tf, d), lambda i, j: (layer, half, j, 0)),
            pl.BlockSpec((1, d), lambda i, j: (0, 0)),
        ],
        out_specs=pl.BlockSpec((tm, d), lambda i, j: (i, 0)),
        out_shape=jax.ShapeDtypeStruct((rows, d), F32),
        scratch_shapes=[pltpu.VMEM((tm, d), BF16), pltpu.VMEM((2, 1, d), F32)],
        compiler_params=_params("arbitrary", "arbitrary"),
        name="ffn",
    )(h, mod, g, w_in, w_in, w_out, fg)


def _rope(x, cos, sin):
    lane = lax.broadcasted_iota(jnp.int32, x.shape, 1)
    first = (lane % 64) < 32
    swapped = jnp.where(first, pltpu.roll(x, 96, 1), pltpu.roll(x, 32, 1))
    return x * cos + swapped * sin


def _proj_kernel(h_ref, mod_ref, g_ref, w_ref, cos_ref, sin_ref, *rest, tm, tn, n_lat, rope_tiles, n16, has32):
    o16_ref = rest[0]
    o32_ref = rest[1] if has32 else None
    u_scr, gs_scr = rest[-2:]
    i, j = pl.program_id(0), pl.program_id(1)

    @pl.when(j == 0)
    def _():
        _adaln_rows(h_ref, mod_ref, g_ref, u_scr, gs_scr, sub=1, tile=i, tm=tm, n_lat=n_lat)

    y = _dot(u_scr[...], w_ref[...])

    def write16():
        if rope_tiles == 0:
            o16_ref[...] = y.astype(BF16)
            return

        @pl.when(j < rope_tiles)
        def _():
            cos, sin = cos_ref[...], sin_ref[...]
            for hd in range(tn // HEAD_DIM):
                sl = slice(hd * HEAD_DIM, (hd + 1) * HEAD_DIM)
                o16_ref[:, sl] = _rope(y[:, sl], cos, sin).astype(BF16)

        @pl.when(j >= rope_tiles)
        def _():
            o16_ref[...] = y.astype(BF16)

    if has32:
        pl.when(j < n16)(write16)

        @pl.when(j >= n16)
        def _():
            o32_ref[...] = y
    else:
        write16()


def _proj(h, mod, g, w, cos, sin, *, layer, tm, tn, n_lat, rope_cols, cols16):
    rows, d = h.shape
    n = w.shape[2]
    n16 = cols16 // tn
    has32 = n > cols16
    out_specs = [pl.BlockSpec((tm, tn), lambda i, j: (i, jnp.minimum(j, n16 - 1)))]
    out_shape = [jax.ShapeDtypeStruct((rows, cols16), BF16)]
    if has32:
        out_specs.append(pl.BlockSpec((tm, tn), lambda i, j: (i, jnp.maximum(j - n16, 0))))
        out_shape.append(jax.ShapeDtypeStruct((rows, n - cols16), F32))
    return pl.pallas_call(
        functools.partial(_proj_kernel, tm=tm, tn=tn, n_lat=n_lat, rope_tiles=rope_cols // tn, n16=n16,
                          has32=has32),
        grid=(rows // tm, n // tn),
        in_specs=[
            pl.BlockSpec((tm, d), lambda i, j: (i, 0)),
            pl.BlockSpec((2, 9, d), lambda i, j: (0, 0, 0)),
            pl.BlockSpec((1, d), lambda i, j: (0, 0)),
            pl.BlockSpec((None, d, tn), lambda i, j: (layer, 0, j)),
            pl.BlockSpec((tm, HEAD_DIM), lambda i, j: (i, 0)),
            pl.BlockSpec((tm, HEAD_DIM), lambda i, j: (i, 0)),
        ],
        out_specs=out_specs,
        out_shape=out_shape,
        scratch_shapes=[pltpu.VMEM((tm, d), BF16), pltpu.VMEM((2, 1, d), F32)],
        compiler_params=_params("arbitrary", "arbitrary"),
        name="proj",
    )(h, mod, g, w, cos, sin)


def _rope_tables(n_lat, n_ctx):
    t = np.arange(n_lat)
    inv = ROPE_THETA ** (-jnp.arange(0, 64, 2, dtype=F32) / 64)
    ang_r = jnp.asarray(t // GRID_W, F32)[:, None] * inv[None, :]
    ang_c = jnp.asarray(t % GRID_W, F32)[:, None] * inv[None, :]
    cr, sr, cc, sc = jnp.cos(ang_r), jnp.sin(ang_r), jnp.cos(ang_c), jnp.sin(ang_c)
    cos = jnp.concatenate([cr, cr, cc, cc], axis=1)
    sin = jnp.concatenate([-sr, sr, -sc, sc], axis=1)
    cos = jnp.concatenate([cos, jnp.ones((n_ctx, HEAD_DIM), F32)], axis=0)
    sin = jnp.concatenate([sin, jnp.zeros((n_ctx, HEAD_DIM), F32)], axis=0)
    return cos, sin


def _oproj_kernel(a1_ref, a2_ref, w_ref, h_ref, mod_ref, o_ref, *, tm, n_lat, k1):
    is_ctx = _is_ctx(pl.program_id(0), tm, n_lat)
    y = _dot(a1_ref[...], w_ref[:k1, :]) + _dot(a2_ref[...], w_ref[k1:, :])
    o_ref[...] = h_ref[...] + _mod_rows(mod_ref, 1, 2, is_ctx) * y


def _oproj(a1, a2, col2, w, h, mod, *, layer, tm, n_lat):
    rows, d = h.shape
    k1 = w.shape[1] // 2
    return pl.pallas_call(
        functools.partial(_oproj_kernel, tm=tm, n_lat=n_lat, k1=k1),
        grid=(rows // tm,),
        in_specs=[
            pl.BlockSpec((tm, k1), lambda i: (i, 0)),
            pl.BlockSpec((tm, k1), lambda i: (i, col2)),
            pl.BlockSpec((None,) + w.shape[1:], lambda i: (layer, 0, 0)),
            pl.BlockSpec((tm, d), lambda i: (i, 0)),
            pl.BlockSpec((2, 9, d), lambda i: (0, 0, 0)),
        ],
        out_specs=pl.BlockSpec((tm, d), lambda i: (i, 0)),
        out_shape=jax.ShapeDtypeStruct((rows, d), F32),
        compiler_params=_params("arbitrary"),
        name="oproj",
    )(a1, a2, w, h, mod)


def _na_bias(rpb, n_rows):
    nh, n_a, n_b = rpb.shape
    nb = n_rows // NA_Q_ROWS
    w = jnp.full((nh, n_a, 128), MASKED, F32)
    w = w.at[..., :NA_COLS].set(rpb[..., NA_COLS - 1:]).at[..., 128 - (NA_COLS - 1):].set(rpb[..., :NA_COLS - 1])
    toep = jnp.tile(w, (1, 1, GRID_W))[..., :GRID_W * 127].reshape(nh, n_a, GRID_W, 127)[..., :GRID_W]
    toep = jnp.pad(toep, ((0, 0), (NA_K_ROWS, NA_K_ROWS), (0, 0), (0, 0)), constant_values=MASKED)

    q = np.arange(ATT_BLOCK)
    k = np.arange(NA_K_ROWS * GRID_W)
    dr, c = q // GRID_W, q % GRID_W
    kr, kc = k // GRID_W, k % GRID_W
    c0 = np.clip(c - NA_COLS // 2, 0, GRID_W - NA_COLS)
    col_ok = (kc[None, :] >= c0[:, None]) & (kc[None, :] < c0[:, None] + NA_COLS)
    tables, ok = [], []
    for b in (0, 1, nb - 1):
        r = b * NA_Q_ROWS + dr
        r0 = np.clip(r - NA_ROWS // 2, 0, n_rows - NA_ROWS)
        base = int(np.clip(b * NA_Q_ROWS - NA_ROWS // 2, 0, n_rows - NA_K_ROWS))
        kabs = base + kr
        ok.append((kabs[None, :] >= r0[:, None]) & (kabs[None, :] < r0[:, None] + NA_ROWS) & col_ok)
        per_row = []
        for j in range(NA_Q_ROWS):
            lo = base - (b * NA_Q_ROWS + j) + NA_ROWS - 1 + NA_K_ROWS
            blk = toep[:, lo:lo + NA_K_ROWS]
            per_row.append(blk.transpose(0, 2, 1, 3).reshape(nh, GRID_W, NA_K_ROWS * GRID_W))
        tables.append(jnp.stack(per_row, axis=1).reshape(nh, ATT_BLOCK, NA_K_ROWS * GRID_W))
    return jnp.where(np.stack(ok)[:, None], jnp.stack(tables), MASKED)


def _na_kernel(q_ref, k_ref, v_ref, bias_ref, o_ref, *, n_lat, n_ctx, nb):
    b = pl.program_id(1)
    scale = HEAD_DIM ** -0.5
    q = q_ref[...]
    kc = k_ref[n_lat:n_lat + n_ctx, :]
    vc = v_ref[n_lat:n_lat + n_ctx, :]
    s_ctx = _dot_nt(q, kc) * scale
    m_ctx = jnp.max(s_ctx, axis=-1, keepdims=True)

    @pl.when(b < nb)
    def _():
        nk = NA_K_ROWS * GRID_W
        start = pl.multiple_of(jnp.clip(b - 1, 0, nb - 3) * ATT_BLOCK, ATT_BLOCK)
        kl = k_ref[pl.ds(start, nk), :]
        vl = v_ref[pl.ds(start, nk), :]
        s_loc = _dot_nt(q, kl) * scale + bias_ref[0, 0]
        m = jnp.maximum(jnp.max(s_loc, axis=-1, keepdims=True), m_ctx)
        p_loc = jnp.exp(s_loc - m)
        p_ctx = jnp.exp(s_ctx - m)
        den = jnp.sum(p_loc, axis=-1, keepdims=True) + jnp.sum(p_ctx, axis=-1, keepdims=True)
        o = _dot(p_loc.astype(BF16), vl) + _dot(p_ctx.astype(BF16), vc)
        o_ref[...] = (o / den).astype(o_ref.dtype)

    @pl.when(b >= nb)
    def _():
        p_ctx = jnp.exp(s_ctx - m_ctx)
        den = jnp.sum(p_ctx, axis=-1, keepdims=True)
        o_ref[...] = (_dot(p_ctx.astype(BF16), vc) / den).astype(o_ref.dtype)


def _na_attention(p, bias, *, n_lat, n_ctx, heads):
    rows = n_lat + n_ctx
    nb = n_lat // ATT_BLOCK
    nblk = rows // ATT_BLOCK

    def pattern(b):
        return jnp.where(b == 0, 0, jnp.where(b == nb - 1, 2, 1))

    return pl.pallas_call(
        functools.partial(_na_kernel, n_lat=n_lat, n_ctx=n_ctx, nb=nb),
        grid=(heads, nblk),
        in_specs=[
            pl.BlockSpec((ATT_BLOCK, HEAD_DIM), lambda h, b: (b, h)),
            pl.BlockSpec((rows, HEAD_DIM), lambda h, b: (0, heads + h)),
            pl.BlockSpec((rows, HEAD_DIM), lambda h, b: (0, 2 * heads + h)),
            pl.BlockSpec((1, 1, ATT_BLOCK, NA_K_ROWS * GRID_W), lambda h, b: (pattern(b), h, 0, 0)),
        ],
        out_specs=pl.BlockSpec((ATT_BLOCK, HEAD_DIM), lambda h, b: (b, h)),
        out_shape=jax.ShapeDtypeStruct((rows, heads * HEAD_DIM), BF16),
        compiler_params=_params("arbitrary", "arbitrary"),
        name="na_attention",
    )(p, p, p, bias)


def _hgrn_levels():
    levels, m = [], 1
    while m < HG_CHUNK:
        levels.append(m)
        m *= 2
    return levels


def _hgrn_consts():
    c = HG_CHUNK
    levels = _hgrn_levels()
    fine = [m for m in levels if m < HG_ROW_LEVEL]
    idx = np.arange(c)
    t, j = idx[:, None], idx[None, :]
    n = np.zeros((2, 1 + len(fine), c, c), np.float32)
    msk = np.zeros((2, 1 + len(levels), c, c), np.float32)
    n[0, 0], n[1, 0] = j <= t, j >= t
    msk[0, 0] = msk[1, 0] = np.eye(c)
    for li, m in enumerate(levels):
        seg = idx // (2 * m)
        right = ((idx % (2 * m)) >= m)[:, None]
        last_left = (seg * 2 * m + m - 1)[:, None]
        first_right = last_left + 1
        if m < HG_ROW_LEVEL:
            n[0, 1 + li] = np.where(right, (j > last_left) & (j <= t), (j > t) & (j <= last_left))
            n[1, 1 + li] = np.where(right, (j >= first_right) & (j < t), (j >= t) & (j < first_right))
        same = seg[:, None] == seg[None, :]
        msk[0, 1 + li] = same & right & ~right.T
        msk[1, 1 + li] = same & ~right & right.T
    return n.reshape(2, -1, c), msk


def _hgrn_chunk(d, q, fx, v, lb, n_ref, msk_ref, s_scr, b_scr):
    c = HG_CHUNK
    log_sig = jnp.minimum(fx, 0.0) - jnp.log1p(jnp.exp(-jnp.abs(fx)))
    la = jnp.log(lb)
    lc = jnp.log1p(-lb) + log_sig
    log_f = jnp.maximum(la, lc) + jnp.log1p(jnp.exp(-jnp.abs(la - lc)))
    kk = (1.0 - lb) * _sigmoid(-fx)
    qf = q.astype(F32)

    hi = log_f.astype(BF16)
    mid = (log_f - hi.astype(F32)).astype(BF16)
    dd = _dot(n_ref[d], jnp.concatenate([hi, mid], axis=1))
    dd = dd[:, :HEAD_DIM] + dd[:, HEAD_DIM:]
    b = dd[:c]
    b_scr[d] = b

    attn = _dot_nt(q, kk.astype(BF16)) * msk_ref[d, 0]
    for li, m in enumerate(_hgrn_levels()):
        if m < HG_ROW_LEVEL:
            ex = dd[(1 + li) * c:(2 + li) * c]
        else:
            row = m - 1 + d
            ref = jnp.concatenate(
                [jnp.broadcast_to(b_scr[d, s + row:s + row + 1, :], (2 * m, HEAD_DIM)) for s in range(0, c, 2 * m)],
                axis=0)
            ex = -jnp.abs(b - ref)
        e = jnp.exp(ex)
        attn += _dot_nt((qf * e).astype(BF16), (kk * e).astype(BF16)) * msk_ref[d, 1 + li]

    total = b[c - 1:c] if d == 0 else b[0:1]
    state_t = s_scr[d]
    o = _dot_nt((qf * jnp.exp(b)).astype(BF16), state_t.astype(BF16)) + _dot(attn.astype(BF16), v)
    v_t = v.astype(F32).T.astype(BF16)
    s_scr[d] = jnp.exp(total) * state_t + _dot(v_t, (kk * jnp.exp(total - b)).astype(BF16))
    return o


def _hgrn_kernel(qf_ref, qb_ref, ff_ref, fb_ref, if_ref, ib_ref, lbl_ref, n_ref, msk_ref,
                 of_ref, ob_ref, s_scr, b_scr, *, layer_e):
    @pl.when(pl.program_id(1) == 0)
    def _():
        s_scr[...] = jnp.zeros_like(s_scr)

    for d, (q_ref, f_ref, i_ref, o_ref) in enumerate(((qf_ref, ff_ref, if_ref, of_ref),
                                                      (qb_ref, fb_ref, ib_ref, ob_ref))):
        lg = lbl_ref[d]
        ex = jnp.exp(lg - jnp.max(lg, axis=0, keepdims=True))
        num = jnp.zeros_like(ex[0:1])
        for e in range(1, layer_e + 1):
            num = num + ex[e:e + 1]
        lb = num / jnp.sum(ex, axis=0, keepdims=True)
        o_ref[...] = _hgrn_chunk(d, q_ref[...], f_ref[...], i_ref[...], lb, n_ref, msk_ref, s_scr, b_scr)


def _hgrn_scan(p16, p32, lb_logits, consts, *, layer_e, n_lat, n_ctx, heads, qcol, icol):
    rows = n_lat + n_ctx
    n_mat, msk = consts
    lat_chunks = n_lat // HG_CHUNK
    qc, ic = qcol // HEAD_DIM, icol // HEAD_DIM

    def fwd(s):
        return jnp.where(s == 0, lat_chunks, s - 1)

    def bwd(s):
        return lat_chunks - s

    def spec(blk, col):
        return pl.BlockSpec((HG_CHUNK, HEAD_DIM), lambda h, s: (blk(s), col + h))

    out = jax.ShapeDtypeStruct((rows, heads * HEAD_DIM), F32)
    return pl.pallas_call(
        functools.partial(_hgrn_kernel, layer_e=layer_e),
        grid=(heads, 1 + lat_chunks),
        in_specs=[
            spec(fwd, qc), spec(bwd, qc),
            spec(fwd, 0), spec(bwd, heads),
            spec(fwd, ic), spec(bwd, ic),
            pl.BlockSpec((2, lb_logits.shape[1], HEAD_DIM), lambda h, s: (0, 0, h)),
            pl.BlockSpec(n_mat.shape, lambda h, s: (0, 0, 0)),
            pl.BlockSpec(msk.shape, lambda h, s: (0, 0, 0, 0)),
        ],
        out_specs=[spec(fwd, 0), spec(bwd, 0)],
        out_shape=[out, out],
        scratch_shapes=[pltpu.VMEM((2, HEAD_DIM, HEAD_DIM), F32), pltpu.VMEM((2, HG_CHUNK, HEAD_DIM), F32)],
        compiler_params=_params("arbitrary", "arbitrary"),
        name="hgrn_scan",
    )(p16, p16, p32, p32, p16, p16, lb_logits, n_mat, msk)


def _hgrn_out_kernel(of_ref, ob_ref, g_ref, ng_ref, y_ref):
    o = of_ref[...] + ob_ref[...]
    o = o * lax.rsqrt(jnp.mean(o * o, axis=-1, keepdims=True) + NORM_EPS)
    y_ref[...] = (o * ng_ref[...] * _silu(g_ref[...])).astype(y_ref.dtype)


def _hgrn_out(o_f, o_b, p, norm_g, *, heads, gcol0, tm):
    rows = o_f.shape[0]
    return pl.pallas_call(
        _hgrn_out_kernel,
        grid=(rows // tm, heads),
        in_specs=[
            pl.BlockSpec((tm, HEAD_DIM), lambda i, h: (i, h)),
            pl.BlockSpec((tm, HEAD_DIM), lambda i, h: (i, h)),
            pl.BlockSpec((tm, HEAD_DIM), lambda i, h: (i, gcol0 // HEAD_DIM + h)),
            pl.BlockSpec((1, HEAD_DIM), lambda i, h: (0, h)),
        ],
        out_specs=pl.BlockSpec((tm, HEAD_DIM), lambda i, h: (i, h)),
        out_shape=jax.ShapeDtypeStruct((rows, heads * HEAD_DIM), BF16),
        compiler_params=_params("arbitrary", "arbitrary"),
        name="hgrn_out",
    )(o_f, o_b, p, norm_g)


def _swa_kernel(q_ref, k_ref, v_ref, sink_ref, o_ref, *, n_lat, n_ctx, nb, group, span):
    b = pl.program_id(1)
    scale = HEAD_DIM ** -0.5
    kc = k_ref[n_lat:n_lat + n_ctx, :]
    vc = v_ref[n_lat:n_lat + n_ctx, :]

    def finish(g, s_ctx, s_loc, vl):
        sink = sink_ref[0, g:g + 1, 0:1]
        m = jnp.maximum(jnp.max(s_ctx, axis=-1, keepdims=True), sink)
        if s_loc is not None:
            m = jnp.maximum(m, jnp.max(s_loc, axis=-1, keepdims=True))
        p_ctx = jnp.exp(s_ctx - m)
        den = jnp.sum(p_ctx, axis=-1, keepdims=True) + jnp.exp(sink - m)
        o = _dot(p_ctx.astype(BF16), vc)
        if s_loc is not None:
            p_loc = jnp.exp(s_loc - m)
            den = den + jnp.sum(p_loc, axis=-1, keepdims=True)
            o = o + _dot(p_loc.astype(BF16), vl)
        o_ref[:, g * HEAD_DIM:(g + 1) * HEAD_DIM] = (o / den).astype(o_ref.dtype)

    @pl.when(b < nb)
    def _():
        start = pl.multiple_of(jnp.clip(b * ATT_BLOCK - WINDOW, 0, n_lat - span), WINDOW)
        kl = k_ref[pl.ds(start, span), :]
        vl = v_ref[pl.ds(start, span), :]
        qpos = b * ATT_BLOCK + lax.broadcasted_iota(jnp.int32, (ATT_BLOCK, span), 0)
        kpos = start + lax.broadcasted_iota(jnp.int32, (ATT_BLOCK, span), 1)
        valid = jnp.abs(kpos - qpos) <= WINDOW
        for g in range(group):
            q = q_ref[:, g * HEAD_DIM:(g + 1) * HEAD_DIM]
            s_loc = jnp.where(valid, _dot_nt(q, kl) * scale, MASKED)
            finish(g, _dot_nt(q, kc) * scale, s_loc, vl)

    @pl.when(b >= nb)
    def _():
        for g in range(group):
            q = q_ref[:, g * HEAD_DIM:(g + 1) * HEAD_DIM]
            finish(g, _dot_nt(q, kc) * scale, None, None)


def _swa_attention(p, sink, *, n_lat, n_ctx, heads, kv_heads):
    rows = n_lat + n_ctx
    group = heads // kv_heads
    nb = n_lat // ATT_BLOCK
    span = ATT_BLOCK + 2 * WINDOW
    sink_tab = jnp.broadcast_to(sink.astype(F32).reshape(kv_heads, group, 1), (kv_heads, group, HEAD_DIM))
    return pl.pallas_call(
        functools.partial(_swa_kernel, n_lat=n_lat, n_ctx=n_ctx, nb=nb, group=group, span=span),
        grid=(kv_heads, rows // ATT_BLOCK),
        in_specs=[
            pl.BlockSpec((ATT_BLOCK, group * HEAD_DIM), lambda k, b: (b, k)),
            pl.BlockSpec((rows, HEAD_DIM), lambda k, b: (0, heads + k)),
            pl.BlockSpec((rows, HEAD_DIM), lambda k, b: (0, heads + kv_heads + k)),
            pl.BlockSpec((1, group, HEAD_DIM), lambda k, b: (k, 0, 0)),
        ],
        out_specs=pl.BlockSpec((ATT_BLOCK, group * HEAD_DIM), lambda k, b: (b, k)),
        out_shape=jax.ShapeDtypeStruct((rows, heads * HEAD_DIM), BF16),
        compiler_params=_params("arbitrary", "arbitrary"),
        name="swa_attention",
    )(p, p, p, sink_tab)


def kernel(x, c, ctx, c_ctx, w_mod, b_mod, norm_g, w_ff_in, w_ff_out, w_in_even, w_out_even,
           na_rpb, hg_lb_logits, hg_norm_g, w_qkv_odd, w_o_odd, sink_odd, final_norm_g):
    assert x.shape[0] == 1
    depth, d = w_mod.shape[0], x.shape[2]
    n_lat, n_ctx = x.shape[1], ctx.shape[1]
    rows = n_lat + n_ctx
    tm = 768
    assert rows % tm == 0 and n_lat % 512 == 0 and n_ctx == ATT_BLOCK
    na_heads = na_rpb.shape[1]
    a_w = na_heads * HEAD_DIM
    hg_heads = hg_norm_g.shape[1] // HEAD_DIM
    heads = sink_odd.shape[1]
    kv_heads = (w_qkv_odd.shape[2] // HEAD_DIM - heads) // 2

    h = jnp.concatenate([x[0], ctx[0]], axis=0)
    cc = jnp.zeros((8, d), F32).at[0].set(c[0]).at[1].set(c_ctx)
    mod_all = _modulation(cc, w_mod, b_mod)[:, :2].reshape(depth, 2, 9, d)

    cos, sin = _rope_tables(n_lat, n_ctx)
    hg_consts = _hgrn_consts()
    hg_consts = (jnp.asarray(hg_consts[0], BF16), jnp.asarray(hg_consts[1], F32))
    fg = final_norm_g.reshape(1, d)
    w_ff_in, w_ff_out = w_ff_in.astype(BF16), w_ff_out.astype(BF16)
    b_w = hg_heads * HEAD_DIM
    c_f, c_i, c_g = 3 * a_w + b_w, 3 * a_w + 3 * b_w, 3 * a_w + 4 * b_w
    w_in_even = jnp.concatenate([w_in_even[:, :, :c_f], w_in_even[:, :, c_i:c_g], w_in_even[:, :, c_f:c_i],
                                 w_in_even[:, :, c_g:]], axis=2).astype(BF16)
    w_out_even = w_out_even.astype(BF16)
    w_qkv_odd, w_o_odd = w_qkv_odd.astype(BF16), w_o_odd.astype(BF16)

    for l in range(depth):
        mod = mod_all[l]
        last = l == depth - 1
        h = _ffn(h, mod, norm_g[l, 0].reshape(1, d), w_ff_in, w_ff_out, fg, layer=l, half=0,
                 sub=0, rows=rows, tm=tm, n_lat=n_lat)
        g1 = norm_g[l, 1].reshape(1, d)
        if l % 2 == 0:
            e = l // 2
            p16, p32 = _proj(h, mod, g1, w_in_even, cos, sin, layer=e, tm=tm, tn=1024, n_lat=n_lat,
                             rope_cols=0, cols16=3 * a_w + 2 * b_w)
            ya = _na_attention(p16, _na_bias(na_rpb[e], n_lat // GRID_W), n_lat=n_lat, n_ctx=n_ctx, heads=na_heads)
            o_f, o_b = _hgrn_scan(p16, p32, hg_lb_logits, hg_consts, layer_e=e, n_lat=n_lat, n_ctx=n_ctx,
                                  heads=hg_heads, qcol=3 * a_w, icol=3 * a_w + b_w)
            yb = _hgrn_out(o_f, o_b, p32, hg_norm_g[e].reshape(1, -1), heads=hg_heads, gcol0=2 * b_w, tm=tm)
            h = _oproj(ya, yb, 0, w_out_even, h, mod, layer=e, tm=tm, n_lat=n_lat)
        else:
            o = l // 2
            p16, = _proj(h, mod, g1, w_qkv_odd, cos, sin, layer=o, tm=tm, tn=512, n_lat=n_lat,
                         rope_cols=(heads + kv_heads) * HEAD_DIM, cols16=w_qkv_odd.shape[2])
            y = _swa_attention(p16, sink_odd[o], n_lat=n_lat, n_ctx=n_ctx, heads=heads, kv_heads=kv_heads)
            h = _oproj(y, y, 1, w_o_odd, h, mod, layer=o, tm=tm, n_lat=n_lat)
        g2 = norm_g[l, 2].reshape(1, d)
        if last:
            h = _ffn(h, mod, g2, w_ff_in, w_ff_out, fg, layer=l, half=1,
                     sub=2, rows=n_lat, tm=512, n_lat=n_lat, final_norm=True)
        else:
            h = _ffn(h, mod, g2, w_ff_in, w_ff_out, fg, layer=l, half=1,
                     sub=2, rows=rows, tm=tm, n_lat=n_lat)
    return h[None]
```

```python
import functools

import numpy as np
import jax
import jax.numpy as jnp
from jax import lax
from jax.experimental import pallas as pl
from jax.experimental.pallas import tpu as pltpu

F32 = jnp.float32
BF16 = jnp.bfloat16

GRID_W = 64
NORM_EPS = 1e-6
ROPE_THETA = 10000.0
HEAD_DIM = 128
NA_ROWS = 8
NA_COLS = 16
NA_Q_ROWS = 4
NA_K_ROWS = NA_Q_ROWS + NA_ROWS
ATT_BLOCK = NA_Q_ROWS * GRID_W
WINDOW = 128
HG_CHUNK = 256
ROW_CHUNK = 16
FFN_OUT_COLS = 512
ROW_UNROLL = 4
HG_ROW_LEVEL = 8
MASKED = -1e30
VMEM_LIMIT = 56 * 1024 * 1024


def _params(*sem):
    return pltpu.CompilerParams(dimension_semantics=sem, vmem_limit_bytes=VMEM_LIMIT)


def _dot(a, b):
    return jnp.dot(a, b, preferred_element_type=F32)


def _dot_nt(a, b):
    return lax.dot_general(a, b, (((1,), (1,)), ((), ())), preferred_element_type=F32)


def _sigmoid(x):
    return 1.0 / (1.0 + jnp.exp(-x))


def _silu(x):
    return x * _sigmoid(x)


def _mod_kernel(c_ref, w_ref, b_ref, o_ref):
    s = _silu(c_ref[...]).astype(BF16)
    o_ref[0] = _dot(s, w_ref[0].astype(BF16)) + b_ref[0]


def _modulation(cc, w_mod, b_mod):
    depth, d, n = w_mod.shape
    tn = 1024
    return pl.pallas_call(
        _mod_kernel,
        grid=(depth, n // tn),
        in_specs=[
            pl.BlockSpec((8, d), lambda l, j: (0, 0)),
            pl.BlockSpec((1, d, tn), lambda l, j: (l, 0, j)),
            pl.BlockSpec((1, 1, tn), lambda l, j: (l, 0, j)),
        ],
        out_specs=pl.BlockSpec((1, 8, tn), lambda l, j: (l, 0, j)),
        out_shape=jax.ShapeDtypeStruct((depth, 8, n), F32),
        compiler_params=_params("arbitrary", "arbitrary"),
        name="modulation",
    )(cc, w_mod, b_mod.reshape(depth, 1, n))


def _mod_rows(mod_ref, sub, k, is_ctx):
    r = 3 * sub + k
    return jnp.where(is_ctx, mod_ref[1, r:r + 1, :], mod_ref[0, r:r + 1, :])


def _row_max(*blocks):
    acc = None
    for s in blocks:
        for c0 in range(0, s.shape[1], HEAD_DIM):
            t = s[:, c0:c0 + HEAD_DIM]
            acc = t if acc is None else jnp.maximum(acc, t)
    return jnp.max(acc, axis=-1, keepdims=True)


def _with_ones(v):
    return jnp.concatenate([v, jnp.ones_like(v)], axis=1)


def _is_ctx(tile, tm, n_lat):
    row = tile * tm + lax.broadcasted_iota(jnp.int32, (tm, 1), 0)
    return row >= n_lat


def _row_chunks(tile, tm, n_lat, body):
    def step(c, carry):
        r0 = pl.multiple_of(c * ROW_CHUNK, ROW_CHUNK)
        body(pl.ds(r0, ROW_CHUNK), (tile * tm + r0 >= n_lat).astype(jnp.int32))
        return carry

    lax.fori_loop(0, tm // ROW_CHUNK, step, 0, unroll=ROW_UNROLL)


def _adaln_rows(h_ref, mod_ref, g_ref, u_scr, gs_scr, *, sub, tile, tm, n_lat):
    for which in range(2):
        gs_scr[which] = g_ref[...] * (1.0 + mod_ref[which, 3 * sub + 1:3 * sub + 2, :])

    def body(rows, which):
        x = h_ref[rows, :]
        r = lax.rsqrt(jnp.mean(x * x, axis=-1, keepdims=True) + NORM_EPS)
        u_scr[rows, :] = ((x * r) * gs_scr[which] + mod_ref[which, 3 * sub:3 * sub + 1, :]).astype(BF16)

    _row_chunks(tile, tm, n_lat, body)


def _ffn_kernel(h_ref, mod_ref, g_ref, wg_ref, wu_ref, wo_ref, fg_ref, o_ref, u_scr, gs_scr,
                *, sub, tm, n_lat, final_norm):
    i, j = pl.program_id(0), pl.program_id(1)

    @pl.when(j == 0)
    def _():
        _adaln_rows(h_ref, mod_ref, g_ref, u_scr, gs_scr, sub=sub, tile=i, tm=tm, n_lat=n_lat)
        o_ref[...] = jnp.zeros_like(o_ref)

    u = u_scr[...]
    gate = _dot(u, wg_ref[...])
    up = _dot(u, wu_ref[...])
    a = (_silu(gate) * up).astype(BF16)
    d = o_ref.shape[1]
    for c0 in range(0, d, FFN_OUT_COLS):
        o_ref[:, c0:c0 + FFN_OUT_COLS] += _dot(a, wo_ref[:, c0:c0 + FFN_OUT_COLS])

    @pl.when(j == pl.num_programs(1) - 1)
    def _():
        def body(rows, which):
            h = h_ref[rows, :] + (0.5 * mod_ref[which, 3 * sub + 2:3 * sub + 3, :]) * o_ref[rows, :]
            if final_norm:
                h = h * lax.rsqrt(jnp.mean(h * h, axis=-1, keepdims=True) + NORM_EPS) * fg_ref[...]
            o_ref[rows, :] = h

        _row_chunks(i, tm, n_lat, body)


def _ffn(h, mod, g, w_in, w_out, fg, *, layer, half, sub, rows, tm, n_lat, final_norm=False):
    d = h.shape[1]
    f = w_out.shape[2]
    tf = 512
    nf = f // tf
    return pl.pallas_call(
        functools.partial(_ffn_kernel, sub=sub, tm=tm, n_lat=n_lat, final_norm=final_norm),
        grid=(rows // tm, nf),
        in_specs=[
            pl.BlockSpec((tm, d), lambda i, j: (i, 0)),
            pl.BlockSpec((2, 9, d), lambda i, j: (0, 0, 0)),
            pl.BlockSpec((1, d), lambda i, j: (0, 0)),
            pl.BlockSpec((None, None, d, tf), lambda i, j: (layer, half, 0, j)),
            pl.BlockSpec((None, None, d, tf), lambda i, j: (layer, half, 0, nf + j)),
            pl.BlockSpec((None, None, tf, d), lambda i, j: (layer, half, j, 0)),
            pl.BlockSpec((1, d), lambda i, j: (0, 0)),
        ],
        out_specs=pl.BlockSpec((tm, d), lambda i, j: (i, 0)),
        out_shape=jax.ShapeDtypeStruct((rows, d), F32),
        scratch_shapes=[pltpu.VMEM((tm, d), BF16), pltpu.VMEM((2, 1, d), F32)],
        compiler_params=_params("arbitrary", "arbitrary"),
        name="ffn",
    )(h, mod, g, w_in, w_in, w_out, fg)


def _rope(x, cos, sin):
    lane = lax.broadcasted_iota(jnp.int32, x.shape, 1)
    first = (lane % 64) < 32
    swapped = jnp.where(first, pltpu.roll(x, 96, 1), pltpu.roll(x, 32, 1))
    return x * cos + swapped * sin


def _proj_kernel(h_ref, mod_ref, g_ref, w_ref, cos_ref, sin_ref, *rest, tm, tn, n_lat, rope_tiles, n16, has32):
    o16_ref = rest[0]
    o32_ref = rest[1] if has32 else None
    u_scr, gs_scr = rest[-2:]
    i, j = pl.program_id(0), pl.program_id(1)

    @pl.when(j == 0)
    def _():
        _adaln_rows(h_ref, mod_ref, g_ref, u_scr, gs_scr, sub=1, tile=i, tm=tm, n_lat=n_lat)

    y = _dot(u_scr[...], w_ref[...])

    def write16():
        if rope_tiles == 0:
            o16_ref[...] = y.astype(BF16)
            return

        @pl.when(j < rope_tiles)
        def _():
            cos, sin = cos_ref[...], sin_ref[...]
            for hd in range(tn // HEAD_DIM):
                sl = slice(hd * HEAD_DIM, (hd + 1) * HEAD_DIM)
                o16_ref[:, sl] = _rope(y[:, sl], cos, sin).astype(BF16)

        @pl.when(j >= rope_tiles)
        def _():
            o16_ref[...] = y.astype(BF16)

    if has32:
        pl.when(j < n16)(write16)

        @pl.when(j >= n16)
        def _():
            o32_ref[...] = y
    else:
        write16()


def _proj(h, mod, g, w, cos, sin, *, layer, tm, tn, n_lat, rope_cols, cols16):
    rows, d = h.shape
    n = w.shape[2]
    n16 = cols16 // tn
    has32 = n > cols16
    out_specs = [pl.BlockSpec((tm, tn), lambda i, j: (i, jnp.minimum(j, n16 - 1)))]
    out_shape = [jax.ShapeDtypeStruct((rows, cols16), BF16)]
    if has32:
        out_specs.append(pl.BlockSpec((tm, tn), lambda i, j: (i, jnp.maximum(j - n16, 0))))
        out_shape.append(jax.ShapeDtypeStruct((rows, n - cols16), F32))
    return pl.pallas_call(
        functools.partial(_proj_kernel, tm=tm, tn=tn, n_lat=n_lat, rope_tiles=rope_cols // tn, n16=n16,
                          has32=has32),
        grid=(rows // tm, n // tn),
        in_specs=[
            pl.BlockSpec((tm, d), lambda i, j: (i, 0)),
            pl.BlockSpec((2, 9, d), lambda i, j: (0, 0, 0)),
            pl.BlockSpec((1, d), lambda i, j: (0, 0)),
            pl.BlockSpec((None, d, tn), lambda i, j: (layer, 0, j)),
            pl.BlockSpec((tm, HEAD_DIM), lambda i, j: (i, 0)),
            pl.BlockSpec((tm, HEAD_DIM), lambda i, j: (i, 0)),
        ],
        out_specs=out_specs,
        out_shape=out_shape,
        scratch_shapes=[pltpu.VMEM((tm, d), BF16), pltpu.VMEM((2, 1, d), F32)],
        compiler_params=_params("arbitrary", "arbitrary"),
        name="proj",
    )(h, mod, g, w, cos, sin)


def _rope_tables(n_lat, n_ctx):
    t = np.arange(n_lat)
    inv = ROPE_THETA ** (-jnp.arange(0, 64, 2, dtype=F32) / 64)
    ang_r = jnp.asarray(t // GRID_W, F32)[:, None] * inv[None, :]
    ang_c = jnp.asarray(t % GRID_W, F32)[:, None] * inv[None, :]
    cr, sr, cc, sc = jnp.cos(ang_r), jnp.sin(ang_r), jnp.cos(ang_c), jnp.sin(ang_c)
    cos = jnp.concatenate([cr, cr, cc, cc], axis=1)
    sin = jnp.concatenate([-sr, sr, -sc, sc], axis=1)
    cos = jnp.concatenate([cos, jnp.ones((n_ctx, HEAD_DIM), F32)], axis=0)
    sin = jnp.concatenate([sin, jnp.zeros((n_ctx, HEAD_DIM), F32)], axis=0)
    return cos, sin


def _oproj_kernel(a1_ref, a2_ref, w_ref, h_ref, mod_ref, o_ref, *, tm, n_lat, k1):
    is_ctx = _is_ctx(pl.program_id(0), tm, n_lat)
    y = _dot(a1_ref[...], w_ref[:k1, :]) + _dot(a2_ref[...], w_ref[k1:, :])
    o_ref[...] = h_ref[...] + _mod_rows(mod_ref, 1, 2, is_ctx) * y


def _oproj(a1, a2, col2, w, h, mod, *, layer, tm, n_lat):
    rows, d = h.shape
    k1 = w.shape[1] // 2
    return pl.pallas_call(
        functools.partial(_oproj_kernel, tm=tm, n_lat=n_lat, k1=k1),
        grid=(rows // tm,),
        in_specs=[
            pl.BlockSpec((tm, k1), lambda i: (i, 0)),
            pl.BlockSpec((tm, k1), lambda i: (i, col2)),
            pl.BlockSpec((None,) + w.shape[1:], lambda i: (layer, 0, 0)),
            pl.BlockSpec((tm, d), lambda i: (i, 0)),
            pl.BlockSpec((2, 9, d), lambda i: (0, 0, 0)),
        ],
        out_specs=pl.BlockSpec((tm, d), lambda i: (i, 0)),
        out_shape=jax.ShapeDtypeStruct((rows, d), F32),
        compiler_params=_params("arbitrary"),
        name="oproj",
    )(a1, a2, w, h, mod)


def _na_bias(rpb, n_rows):
    nh, n_a, n_b = rpb.shape
    nb = n_rows // NA_Q_ROWS
    w = jnp.full((nh, n_a, 128), MASKED, F32)
    w = w.at[..., :NA_COLS].set(rpb[..., NA_COLS - 1:]).at[..., 128 - (NA_COLS - 1):].set(rpb[..., :NA_COLS - 1])
    toep = jnp.tile(w, (1, 1, GRID_W))[..., :GRID_W * 127].reshape(nh, n_a, GRID_W, 127)[..., :GRID_W]
    toep = jnp.pad(toep.transpose(0, 2, 1, 3), ((0, 0), (0, 0), (NA_K_ROWS, NA_K_ROWS), (0, 0)),
                   constant_values=MASKED)

    q = np.arange(ATT_BLOCK)
    k = np.arange(NA_K_ROWS * GRID_W)
    dr, c = q // GRID_W, q % GRID_W
    kr, kc = k // GRID_W, k % GRID_W
    c0 = np.clip(c - NA_COLS // 2, 0, GRID_W - NA_COLS)
    col_ok = (kc[None, :] >= c0[:, None]) & (kc[None, :] < c0[:, None] + NA_COLS)
    tables, ok = [], []
    for b in (0, 1, nb - 1):
        r = b * NA_Q_ROWS + dr
        r0 = np.clip(r - NA_ROWS // 2, 0, n_rows - NA_ROWS)
        base = int(np.clip(b * NA_Q_ROWS - NA_ROWS // 2, 0, n_rows - NA_K_ROWS))
        kabs = base + kr
        ok.append((kabs[None, :] >= r0[:, None]) & (kabs[None, :] < r0[:, None] + NA_ROWS) & col_ok)
        per_row = []
        for j in range(NA_Q_ROWS):
            lo = base - (b * NA_Q_ROWS + j) + NA_ROWS - 1 + NA_K_ROWS
            per_row.append(toep[:, :, lo:lo + NA_K_ROWS].reshape(nh, GRID_W, NA_K_ROWS * GRID_W))
        tables.append(jnp.stack(per_row, axis=1).reshape(nh, ATT_BLOCK, NA_K_ROWS * GRID_W))
    return jnp.where(np.stack(ok)[:, None], jnp.stack(tables), MASKED)


def _na_kernel(q_ref, k_ref, v_ref, bias_ref, o_ref, *, n_lat, n_ctx, nb):
    b = pl.program_id(1)
    scale = HEAD_DIM ** -0.5
    q = q_ref[...]
    kc = k_ref[n_lat:n_lat + n_ctx, :]
    vc = v_ref[n_lat:n_lat + n_ctx, :]
    s_ctx = _dot_nt(q, kc) * scale
    vc1 = _with_ones(vc)

    @pl.when(b < nb)
    def _():
        nk = NA_K_ROWS * GRID_W
        start = pl.multiple_of(jnp.clip(b - 1, 0, nb - 3) * ATT_BLOCK, ATT_BLOCK)
        kl = k_ref[pl.ds(start, nk), :]
        vl1 = _with_ones(v_ref[pl.ds(start, nk), :])
        s_loc = _dot_nt(q, kl) * scale + bias_ref[0, 0]
        m = _row_max(s_loc, s_ctx)
        acc = _dot(jnp.exp(s_loc - m).astype(BF16), vl1) + _dot(jnp.exp(s_ctx - m).astype(BF16), vc1)
        o_ref[...] = (acc[:, :HEAD_DIM] / acc[:, HEAD_DIM:]).astype(o_ref.dtype)

    @pl.when(b >= nb)
    def _():
        acc = _dot(jnp.exp(s_ctx - _row_max(s_ctx)).astype(BF16), vc1)
        o_ref[...] = (acc[:, :HEAD_DIM] / acc[:, HEAD_DIM:]).astype(o_ref.dtype)


def _na_attention(p, bias, *, n_lat, n_ctx, heads):
    rows = n_lat + n_ctx
    nb = n_lat // ATT_BLOCK
    nblk = rows // ATT_BLOCK

    def pattern(b):
        return jnp.where(b == 0, 0, jnp.where(b == nb - 1, 2, 1))

    return pl.pallas_call(
        functools.partial(_na_kernel, n_lat=n_lat, n_ctx=n_ctx, nb=nb),
        grid=(heads, nblk),
        in_specs=[
            pl.BlockSpec((ATT_BLOCK, HEAD_DIM), lambda h, b: (b, h)),
            pl.BlockSpec((rows, HEAD_DIM), lambda h, b: (0, heads + h)),
            pl.BlockSpec((rows, HEAD_DIM), lambda h, b: (0, 2 * heads + h)),
            pl.BlockSpec((1, 1, ATT_BLOCK, NA_K_ROWS * GRID_W), lambda h, b: (pattern(b), h, 0, 0)),
        ],
        out_specs=pl.BlockSpec((ATT_BLOCK, HEAD_DIM), lambda h, b: (b, h)),
        out_shape=jax.ShapeDtypeStruct((rows, heads * HEAD_DIM), BF16),
        compiler_params=_params("arbitrary", "arbitrary"),
        name="na_attention",
    )(p, p, p, bias)


def _hgrn_levels():
    levels, m = [], 1
    while m < HG_CHUNK:
        levels.append(m)
        m *= 2
    return levels


def _hgrn_consts():
    c = HG_CHUNK
    levels = _hgrn_levels()
    fine = [m for m in levels if m < HG_ROW_LEVEL]
    idx = np.arange(c)
    t, j = idx[:, None], idx[None, :]
    n = np.zeros((2, 1 + len(fine), c, c), np.float32)
    msk = np.zeros((2, 1 + len(levels), c, c), np.float32)
    n[0, 0], n[1, 0] = j <= t, j >= t
    msk[0, 0] = msk[1, 0] = np.eye(c)
    for li, m in enumerate(levels):
        seg = idx // (2 * m)
        right = ((idx % (2 * m)) >= m)[:, None]
        last_left = (seg * 2 * m + m - 1)[:, None]
        first_right = last_left + 1
        if m < HG_ROW_LEVEL:
            n[0, 1 + li] = np.where(right, (j > last_left) & (j <= t), (j > t) & (j <= last_left))
            n[1, 1 + li] = np.where(right, (j >= first_right) & (j < t), (j >= t) & (j < first_right))
        same = seg[:, None] == seg[None, :]
        msk[0, 1 + li] = same & right & ~right.T
        msk[1, 1 + li] = same & ~right & right.T
    return n.reshape(2, -1, c), msk


def _hgrn_chunk(d, q, fx, v, lb, n_ref, msk_ref, s_scr, b_scr):
    c = HG_CHUNK
    log_sig = jnp.minimum(fx, 0.0) - jnp.log1p(jnp.exp(-jnp.abs(fx)))
    la = jnp.log(lb)
    lc = jnp.log1p(-lb) + log_sig
    log_f = jnp.maximum(la, lc) + jnp.log1p(jnp.exp(-jnp.abs(la - lc)))
    kk = (1.0 - lb) * _sigmoid(-fx)
    qf = q.astype(F32)

    hi = log_f.astype(BF16)
    mid = (log_f - hi.astype(F32)).astype(BF16)
    dd = _dot(n_ref[d], jnp.concatenate([hi, mid], axis=1))
    dd = dd[:, :HEAD_DIM] + dd[:, HEAD_DIM:]
    b = dd[:c]
    b_scr[d] = b

    attn = _dot_nt(q, kk.astype(BF16)) * msk_ref[d, 0]
    for li, m in enumerate(_hgrn_levels()):
        if m < HG_ROW_LEVEL:
            ex = dd[(1 + li) * c:(2 + li) * c]
        else:
            row = m - 1 + d
            ref = jnp.concatenate(
                [jnp.broadcast_to(b_scr[d, s + row:s + row + 1, :], (2 * m, HEAD_DIM)) for s in range(0, c, 2 * m)],
                axis=0)
            ex = -jnp.abs(b - ref)
        e = jnp.exp(ex)
        attn += _dot_nt((qf * e).astype(BF16), (kk * e).astype(BF16)) * msk_ref[d, 1 + li]

    total = b[c - 1:c] if d == 0 else b[0:1]
    state_t = s_scr[d]
    o = _dot_nt((qf * jnp.exp(b)).astype(BF16), state_t.astype(BF16)) + _dot(attn.astype(BF16), v)
    v_t = v.astype(F32).T.astype(BF16)
    s_scr[d] = jnp.exp(total) * state_t + _dot(v_t, (kk * jnp.exp(total - b)).astype(BF16))
    return o


def _hgrn_kernel(qf_ref, qb_ref, ff_ref, fb_ref, if_ref, ib_ref, lbl_ref, n_ref, msk_ref,
                 of_ref, ob_ref, s_scr, b_scr, *, layer_e):
    @pl.when(pl.program_id(1) == 0)
    def _():
        s_scr[...] = jnp.zeros_like(s_scr)

    for d, (q_ref, f_ref, i_ref, o_ref) in enumerate(((qf_ref, ff_ref, if_ref, of_ref),
                                                      (qb_ref, fb_ref, ib_ref, ob_ref))):
        lg = lbl_ref[d]
        ex = jnp.exp(lg - jnp.max(lg, axis=0, keepdims=True))
        num = jnp.zeros_like(ex[0:1])
        for e in range(1, layer_e + 1):
            num = num + ex[e:e + 1]
        lb = num / jnp.sum(ex, axis=0, keepdims=True)
        o_ref[...] = _hgrn_chunk(d, q_ref[...], f_ref[...], i_ref[...], lb, n_ref, msk_ref, s_scr, b_scr)


def _hgrn_scan(p16, p32, lb_logits, consts, *, layer_e, n_lat, n_ctx, heads, qcol, icol):
    rows = n_lat + n_ctx
    n_mat, msk = consts
    lat_chunks = n_lat // HG_CHUNK
    qc, ic = qcol // HEAD_DIM, icol // HEAD_DIM

    def fwd(s):
        return jnp.where(s == 0, lat_chunks, s - 1)

    def bwd(s):
        return lat_chunks - s

    def spec(blk, col):
        return pl.BlockSpec((HG_CHUNK, HEAD_DIM), lambda h, s: (blk(s), col + h))

    out = jax.ShapeDtypeStruct((rows, heads * HEAD_DIM), F32)
    return pl.pallas_call(
        functools.partial(_hgrn_kernel, layer_e=layer_e),
        grid=(heads, 1 + lat_chunks),
        in_specs=[
            spec(fwd, qc), spec(bwd, qc),
            spec(fwd, 0), spec(bwd, heads),
            spec(fwd, ic), spec(bwd, ic),
            pl.BlockSpec((2, lb_logits.shape[1], HEAD_DIM), lambda h, s: (0, 0, h)),
            pl.BlockSpec(n_mat.shape, lambda h, s: (0, 0, 0)),
            pl.BlockSpec(msk.shape, lambda h, s: (0, 0, 0, 0)),
        ],
        out_specs=[spec(fwd, 0), spec(bwd, 0)],
        out_shape=[out, out],
        scratch_shapes=[pltpu.VMEM((2, HEAD_DIM, HEAD_DIM), F32), pltpu.VMEM((2, HG_CHUNK, HEAD_DIM), F32)],
        compiler_params=_params("arbitrary", "arbitrary"),
        name="hgrn_scan",
    )(p16, p16, p32, p32, p16, p16, lb_logits, n_mat, msk)


def _hgrn_out_kernel(of_ref, ob_ref, g_ref, ng_ref, y_ref):
    o = of_ref[...] + ob_ref[...]
    o = o * lax.rsqrt(jnp.mean(o * o, axis=-1, keepdims=True) + NORM_EPS)
    y_ref[...] = (o * ng_ref[...] * _silu(g_ref[...])).astype(y_ref.dtype)


def _hgrn_out(o_f, o_b, p, norm_g, *, heads, gcol0, tm):
    rows = o_f.shape[0]
    return pl.pallas_call(
        _hgrn_out_kernel,
        grid=(rows // tm, heads),
        in_specs=[
            pl.BlockSpec((tm, HEAD_DIM), lambda i, h: (i, h)),
            pl.BlockSpec((tm, HEAD_DIM), lambda i, h: (i, h)),
            pl.BlockSpec((tm, HEAD_DIM), lambda i, h: (i, gcol0 // HEAD_DIM + h)),
            pl.BlockSpec((1, HEAD_DIM), lambda i, h: (0, h)),
        ],
        out_specs=pl.BlockSpec((tm, HEAD_DIM), lambda i, h: (i, h)),
        out_shape=jax.ShapeDtypeStruct((rows, heads * HEAD_DIM), BF16),
        compiler_params=_params("arbitrary", "arbitrary"),
        name="hgrn_out",
    )(o_f, o_b, p, norm_g)


def _swa_kernel(q_ref, k_ref, v_ref, sink_ref, o_ref, *, n_lat, n_ctx, nb, group, span):
    b = pl.program_id(1)
    scale = HEAD_DIM ** -0.5
    kc = k_ref[n_lat:n_lat + n_ctx, :]
    vc1 = _with_ones(v_ref[n_lat:n_lat + n_ctx, :])

    def finish(g, s_ctx, s_loc, vl1):
        sink = sink_ref[0, g:g + 1, 0:1]
        blocks = (s_ctx,) if s_loc is None else (s_ctx, s_loc)
        m = jnp.maximum(_row_max(*blocks), sink)
        acc = _dot(jnp.exp(s_ctx - m).astype(BF16), vc1)
        if s_loc is not None:
            acc = acc + _dot(jnp.exp(s_loc - m).astype(BF16), vl1)
        den = acc[:, HEAD_DIM:] + jnp.exp(sink - m)
        o_ref[:, g * HEAD_DIM:(g + 1) * HEAD_DIM] = (acc[:, :HEAD_DIM] / den).astype(o_ref.dtype)

    @pl.when(b < nb)
    def _():
        start = pl.multiple_of(jnp.clip(b * ATT_BLOCK - WINDOW, 0, n_lat - span), WINDOW)
        kl = k_ref[pl.ds(start, span), :]
        vl = _with_ones(v_ref[pl.ds(start, span), :])
        qpos = b * ATT_BLOCK + lax.broadcasted_iota(jnp.int32, (ATT_BLOCK, span), 0)
        kpos = start + lax.broadcasted_iota(jnp.int32, (ATT_BLOCK, span), 1)
        valid = jnp.abs(kpos - qpos) <= WINDOW
        for g in range(group):
            q = q_ref[:, g * HEAD_DIM:(g + 1) * HEAD_DIM]
            s_loc = jnp.where(valid, _dot_nt(q, kl) * scale, MASKED)
            finish(g, _dot_nt(q, kc) * scale, s_loc, vl)

    @pl.when(b >= nb)
    def _():
        for g in range(group):
            q = q_ref[:, g * HEAD_DIM:(g + 1) * HEAD_DIM]
            finish(g, _dot_nt(q, kc) * scale, None, None)


def _swa_attention(p, sink, *, n_lat, n_ctx, heads, kv_heads):
    rows = n_lat + n_ctx
    group = heads // kv_heads
    nb = n_lat // ATT_BLOCK
    span = ATT_BLOCK + 2 * WINDOW
    sink_tab = jnp.broadcast_to(sink.astype(F32).reshape(kv_heads, group, 1), (kv_heads, group, HEAD_DIM))
    return pl.pallas_call(
        functools.partial(_swa_kernel, n_lat=n_lat, n_ctx=n_ctx, nb=nb, group=group, span=span),
        grid=(kv_heads, rows // ATT_BLOCK),
        in_specs=[
            pl.BlockSpec((ATT_BLOCK, group * HEAD_DIM), lambda k, b: (b, k)),
            pl.BlockSpec((rows, HEAD_DIM), lambda k, b: (0, heads + k)),
            pl.BlockSpec((rows, HEAD_DIM), lambda k, b: (0, heads + kv_heads + k)),
            pl.BlockSpec((1, group, HEAD_DIM), lambda k, b: (k, 0, 0)),
        ],
        out_specs=pl.BlockSpec((ATT_BLOCK, group * HEAD_DIM), lambda k, b: (b, k)),
        out_shape=jax.ShapeDtypeStruct((rows, heads * HEAD_DIM), BF16),
        compiler_params=_params("arbitrary", "arbitrary"),
        name="swa_attention",
    )(p, p, p, sink_tab)


def kernel(x, c, ctx, c_ctx, w_mod, b_mod, norm_g, w_ff_in, w_ff_out, w_in_even, w_out_even,
           na_rpb, hg_lb_logits, hg_norm_g, w_qkv_odd, w_o_odd, sink_odd, final_norm_g):
    assert x.shape[0] == 1
    depth, d = w_mod.shape[0], x.shape[2]
    n_lat, n_ctx = x.shape[1], ctx.shape[1]
    rows = n_lat + n_ctx
    tm = 768
    assert rows % tm == 0 and n_lat % 1024 == 0 and n_ctx == ATT_BLOCK
    na_heads = na_rpb.shape[1]
    a_w = na_heads * HEAD_DIM
    hg_heads = hg_norm_g.shape[1] // HEAD_DIM
    heads = sink_odd.shape[1]
    kv_heads = (w_qkv_odd.shape[2] // HEAD_DIM - heads) // 2

    h = jnp.concatenate([x[0], ctx[0]], axis=0)
    cc = jnp.zeros((8, d), F32).at[0].set(c[0]).at[1].set(c_ctx)
    mod_all = _modulation(cc, w_mod, b_mod)[:, :2].reshape(depth, 2, 9, d)

    cos, sin = _rope_tables(n_lat, n_ctx)
    hg_consts = _hgrn_consts()
    hg_consts = (jnp.asarray(hg_consts[0], BF16), jnp.asarray(hg_consts[1], F32))
    fg = final_norm_g.reshape(1, d)
    w_ff_in, w_ff_out = w_ff_in.astype(BF16), w_ff_out.astype(BF16)
    b_w = hg_heads * HEAD_DIM
    c_f, c_i, c_g = 3 * a_w + b_w, 3 * a_w + 3 * b_w, 3 * a_w + 4 * b_w
    w_in_even = jnp.concatenate([w_in_even[:, :, :c_f], w_in_even[:, :, c_i:c_g], w_in_even[:, :, c_f:c_i],
                                 w_in_even[:, :, c_g:]], axis=2).astype(BF16)
    w_out_even = w_out_even.astype(BF16)
    w_qkv_odd, w_o_odd = w_qkv_odd.astype(BF16), w_o_odd.astype(BF16)

    for l in range(depth):
        mod = mod_all[l]
        last = l == depth - 1
        h = _ffn(h, mod, norm_g[l, 0].reshape(1, d), w_ff_in, w_ff_out, fg, layer=l, half=0,
                 sub=0, rows=rows, tm=tm, n_lat=n_lat)
        g1 = norm_g[l, 1].reshape(1, d)
        if l % 2 == 0:
            e = l // 2
            p16, p32 = _proj(h, mod, g1, w_in_even, cos, sin, layer=e, tm=tm, tn=1024, n_lat=n_lat,
                             rope_cols=0, cols16=3 * a_w + 2 * b_w)
            ya = _na_attention(p16, _na_bias(na_rpb[e], n_lat // GRID_W), n_lat=n_lat, n_ctx=n_ctx, heads=na_heads)
            o_f, o_b = _hgrn_scan(p16, p32, hg_lb_logits, hg_consts, layer_e=e, n_lat=n_lat, n_ctx=n_ctx,
                                  heads=hg_heads, qcol=3 * a_w, icol=3 * a_w + b_w)
            yb = _hgrn_out(o_f, o_b, p32, hg_norm_g[e].reshape(1, -1), heads=hg_heads, gcol0=2 * b_w, tm=tm)
            h = _oproj(ya, yb, 0, w_out_even, h, mod, layer=e, tm=tm, n_lat=n_lat)
        else:
            o = l // 2
            p16, = _proj(h, mod, g1, w_qkv_odd, cos, sin, layer=o, tm=tm, tn=512, n_lat=n_lat,
                         rope_cols=(heads + kv_heads) * HEAD_DIM, cols16=w_qkv_odd.shape[2])
            y = _swa_attention(p16, sink_odd[o], n_lat=n_lat, n_ctx=n_ctx, heads=heads, kv_heads=kv_heads)
            h = _oproj(y, y, 1, w_o_odd, h, mod, layer=o, tm=tm, n_lat=n_lat)
        g2 = norm_g[l, 2].reshape(1, d)
        if last:
            h = _ffn(h, mod, g2, w_ff_in, w_ff_out, fg, layer=l, half=1,
                     sub=2, rows=n_lat, tm=1024, n_lat=n_lat, final_norm=True)
        else:
            h = _ffn(h, mod, g2, w_ff_in, w_ff_out, fg, layer=l, half=1,
                     sub=2, rows=rows, tm=tm, n_lat=n_lat)
    return h[None]
```

```python
import functools

import numpy as np
import jax
import jax.numpy as jnp
from jax import lax
from jax.experimental import pallas as pl
from jax.experimental.pallas import tpu as pltpu

F32 = jnp.float32
BF16 = jnp.bfloat16

GRID_W = 64
NORM_EPS = 1e-6
ROPE_THETA = 10000.0
HEAD_DIM = 128
NA_ROWS = 8
NA_COLS = 16
NA_Q_ROWS = 4
NA_K_ROWS = NA_Q_ROWS + NA_ROWS
ATT_BLOCK = NA_Q_ROWS * GRID_W
NA_HEADS_PER_STEP = 2
WINDOW = 128
HG_CHUNK = 256
ROW_CHUNK = 16
FFN_OUT_COLS = 512
ROW_UNROLL = 4
HG_ROW_LEVEL = 8
MASKED = -1e30
VMEM_LIMIT = 56 * 1024 * 1024


def _params(*sem):
    return pltpu.CompilerParams(dimension_semantics=sem, vmem_limit_bytes=VMEM_LIMIT)


def _dot(a, b):
    return jnp.dot(a, b, preferred_element_type=F32)


def _dot_nt(a, b):
    return lax.dot_general(a, b, (((1,), (1,)), ((), ())), preferred_element_type=F32)


def _sigmoid(x):
    return 1.0 / (1.0 + jnp.exp(-x))


def _silu(x):
    return x * _sigmoid(x)


def _mod_kernel(c_ref, w_ref, b_ref, o_ref):
    s = _silu(c_ref[...]).astype(BF16)
    o_ref[0] = _dot(s, w_ref[0].astype(BF16)) + b_ref[0]


def _modulation(cc, w_mod, b_mod):
    depth, d, n = w_mod.shape
    tn = 1024
    return pl.pallas_call(
        _mod_kernel,
        grid=(depth, n // tn),
        in_specs=[
            pl.BlockSpec((8, d), lambda l, j: (0, 0)),
            pl.BlockSpec((1, d, tn), lambda l, j: (l, 0, j)),
            pl.BlockSpec((1, 1, tn), lambda l, j: (l, 0, j)),
        ],
        out_specs=pl.BlockSpec((1, 8, tn), lambda l, j: (l, 0, j)),
        out_shape=jax.ShapeDtypeStruct((depth, 8, n), F32),
        compiler_params=_params("arbitrary", "arbitrary"),
        name="modulation",
    )(cc, w_mod, b_mod.reshape(depth, 1, n))


def _mod_rows(mod_ref, sub, k, is_ctx):
    r = 3 * sub + k
    return jnp.where(is_ctx, mod_ref[1, r:r + 1, :], mod_ref[0, r:r + 1, :])


def _row_max(*blocks):
    acc = None
    for s in blocks:
        for c0 in range(0, s.shape[1], HEAD_DIM):
            t = s[:, c0:c0 + HEAD_DIM]
            acc = t if acc is None else jnp.maximum(acc, t)
    return jnp.max(acc, axis=-1, keepdims=True)


def _with_ones(v):
    return jnp.concatenate([v, jnp.ones_like(v)], axis=1)


def _is_ctx(tile, tm, n_lat):
    row = tile * tm + lax.broadcasted_iota(jnp.int32, (tm, 1), 0)
    return row >= n_lat


def _row_chunks(tile, tm, n_lat, body):
    def step(c, carry):
        r0 = pl.multiple_of(c * ROW_CHUNK, ROW_CHUNK)
        body(pl.ds(r0, ROW_CHUNK), (tile * tm + r0 >= n_lat).astype(jnp.int32))
        return carry

    lax.fori_loop(0, tm // ROW_CHUNK, step, 0, unroll=ROW_UNROLL)


def _adaln_rows(h_ref, mod_ref, g_ref, u_scr, gs_scr, *, sub, tile, tm, n_lat):
    for which in range(2):
        gs_scr[which] = g_ref[...] * (1.0 + mod_ref[which, 3 * sub + 1:3 * sub + 2, :])

    def body(rows, which):
        x = h_ref[rows, :]
        r = lax.rsqrt(jnp.mean(x * x, axis=-1, keepdims=True) + NORM_EPS)
        u_scr[rows, :] = ((x * r) * gs_scr[which] + mod_ref[which, 3 * sub:3 * sub + 1, :]).astype(BF16)

    _row_chunks(tile, tm, n_lat, body)


def _ffn_kernel(h_ref, mod_ref, g_ref, wg_ref, wu_ref, wo_ref, fg_ref, o_ref, u_scr, gs_scr,
                *, sub, tm, n_lat, final_norm):
    i, j = pl.program_id(0), pl.program_id(1)

    @pl.when(j == 0)
    def _():
        _adaln_rows(h_ref, mod_ref, g_ref, u_scr, gs_scr, sub=sub, tile=i, tm=tm, n_lat=n_lat)
        o_ref[...] = jnp.zeros_like(o_ref)

    u = u_scr[...]
    gate = _dot(u, wg_ref[...])
    up = _dot(u, wu_ref[...])
    a = (_silu(gate) * up).astype(BF16)
    d = o_ref.shape[1]
    for c0 in range(0, d, FFN_OUT_COLS):
        o_ref[:, c0:c0 + FFN_OUT_COLS] += _dot(a, wo_ref[:, c0:c0 + FFN_OUT_COLS])

    @pl.when(j == pl.num_programs(1) - 1)
    def _():
        def body(rows, which):
            h = h_ref[rows, :] + (0.5 * mod_ref[which, 3 * sub + 2:3 * sub + 3, :]) * o_ref[rows, :]
            if final_norm:
                h = h * lax.rsqrt(jnp.mean(h * h, axis=-1, keepdims=True) + NORM_EPS) * fg_ref[...]
            o_ref[rows, :] = h

        _row_chunks(i, tm, n_lat, body)


def _ffn(h, mod, g, w_in, w_out, fg, *, layer, half, sub, rows, tm, n_lat, final_norm=False):
    d = h.shape[1]
    f = w_out.shape[2]
    tf = 512
    nf = f // tf
    return pl.pallas_call(
        functools.partial(_ffn_kernel, sub=sub, tm=tm, n_lat=n_lat, final_norm=final_norm),
        grid=(rows // tm, nf),
        in_specs=[
            pl.BlockSpec((tm, d), lambda i, j: (i, 0)),
            pl.BlockSpec((2, 9, d), lambda i, j: (0, 0, 0)),
            pl.BlockSpec((1, d), lambda i, j: (0, 0)),
            pl.BlockSpec((None, None, d, tf), lambda i, j: (layer, half, 0, j)),
            pl.BlockSpec((None, None, d, tf), lambda i, j: (layer, half, 0, nf + j)),
            pl.BlockSpec((None, None, tf, d), lambda i, j: (layer, half, j, 0)),
            pl.BlockSpec((1, d), lambda i, j: (0, 0)),
        ],
        out_specs=pl.BlockSpec((tm, d), lambda i, j: (i, 0)),
        out_shape=jax.ShapeDtypeStruct((rows, d), F32),
        scratch_shapes=[pltpu.VMEM((tm, d), BF16), pltpu.VMEM((2, 1, d), F32)],
        compiler_params=_params("arbitrary", "arbitrary"),
        name="ffn",
    )(h, mod, g, w_in, w_in, w_out, fg)


def _rope(x, cos, sin):
    lane = lax.broadcasted_iota(jnp.int32, x.shape, 1)
    first = (lane % 64) < 32
    swapped = jnp.where(first, pltpu.roll(x, 96, 1), pltpu.roll(x, 32, 1))
    return x * cos + swapped * sin


def _proj_kernel(h_ref, mod_ref, g_ref, w_ref, cos_ref, sin_ref, *rest, tm, tn, n_lat, rope_tiles, n16, has32):
    o16_ref = rest[0]
    o32_ref = rest[1] if has32 else None
    u_scr, gs_scr = rest[-2:]
    i, j = pl.program_id(0), pl.program_id(1)

    @pl.when(j == 0)
    def _():
        _adaln_rows(h_ref, mod_ref, g_ref, u_scr, gs_scr, sub=1, tile=i, tm=tm, n_lat=n_lat)

    y = _dot(u_scr[...], w_ref[...])

    def write16():
        if rope_tiles == 0:
            o16_ref[...] = y.astype(BF16)
            return
        cos, sin = cos_ref[...], sin_ref[...]
        for hd in range(tn // HEAD_DIM):
            sl = slice(hd * HEAD_DIM, (hd + 1) * HEAD_DIM)
            o16_ref[:, sl] = _rope(y[:, sl], cos, sin).astype(BF16)

    if has32:
        pl.when(j < n16)(write16)

        @pl.when(j >= n16)
        def _():
            o32_ref[...] = y
    else:
        write16()


def _proj(h, mod, g, w, cos, sin, *, layer, tm, tn, n_lat, rope_cols, cols16):
    rows, d = h.shape
    n = w.shape[2]
    n16 = cols16 // tn
    rope_tiles = rope_cols // tn
    has32 = n > cols16
    out_specs = [pl.BlockSpec((tm, tn), lambda i, j: (i, jnp.minimum(j, n16 - 1)))]
    out_shape = [jax.ShapeDtypeStruct((rows, cols16), BF16)]
    if has32:
        out_specs.append(pl.BlockSpec((tm, tn), lambda i, j: (i, jnp.maximum(j - n16, 0))))
        out_shape.append(jax.ShapeDtypeStruct((rows, n - cols16), F32))
    return pl.pallas_call(
        functools.partial(_proj_kernel, tm=tm, tn=tn, n_lat=n_lat, rope_tiles=rope_tiles, n16=n16,
                          has32=has32),
        grid=(rows // tm, n // tn),
        in_specs=[
            pl.BlockSpec((tm, d), lambda i, j: (i, 0)),
            pl.BlockSpec((2, 9, d), lambda i, j: (0, 0, 0)),
            pl.BlockSpec((1, d), lambda i, j: (0, 0)),
            pl.BlockSpec((None, d, tn), lambda i, j: (layer, 0, j)),
            pl.BlockSpec((None, tm, HEAD_DIM), lambda i, j: (jnp.where(j < rope_tiles, 0, 1), i, 0)),
            pl.BlockSpec((None, tm, HEAD_DIM), lambda i, j: (jnp.where(j < rope_tiles, 0, 1), i, 0)),
        ],
        out_specs=out_specs,
        out_shape=out_shape,
        scratch_shapes=[pltpu.VMEM((tm, d), BF16), pltpu.VMEM((2, 1, d), F32)],
        compiler_params=_params("arbitrary", "arbitrary"),
        name="proj",
    )(h, mod, g, w, cos, sin)


def _rope_tables(n_lat, n_ctx):
    t = np.arange(n_lat)
    inv = ROPE_THETA ** (-jnp.arange(0, 64, 2, dtype=F32) / 64)
    ang_r = jnp.asarray(t // GRID_W, F32)[:, None] * inv[None, :]
    ang_c = jnp.asarray(t % GRID_W, F32)[:, None] * inv[None, :]
    cr, sr, cc, sc = jnp.cos(ang_r), jnp.sin(ang_r), jnp.cos(ang_c), jnp.sin(ang_c)
    cos = jnp.concatenate([cr, cr, cc, cc], axis=1)
    sin = jnp.concatenate([-sr, sr, -sc, sc], axis=1)
    cos = jnp.concatenate([cos, jnp.ones((n_ctx, HEAD_DIM), F32)], axis=0)
    sin = jnp.concatenate([sin, jnp.zeros((n_ctx, HEAD_DIM), F32)], axis=0)
    return jnp.stack([cos, jnp.ones_like(cos)]), jnp.stack([sin, jnp.zeros_like(sin)])


def _oproj_kernel(a1_ref, a2_ref, w_ref, h_ref, mod_ref, o_ref, *, tm, n_lat, k1):
    is_ctx = _is_ctx(pl.program_id(0), tm, n_lat)
    y = _dot(a1_ref[...], w_ref[:k1, :]) + _dot(a2_ref[...], w_ref[k1:, :])
    o_ref[...] = h_ref[...] + _mod_rows(mod_ref, 1, 2, is_ctx) * y


def _oproj(a1, a2, col2, w, h, mod, *, layer, tm, n_lat):
    rows, d = h.shape
    k1 = w.shape[1] // 2
    return pl.pallas_call(
        functools.partial(_oproj_kernel, tm=tm, n_lat=n_lat, k1=k1),
        grid=(rows // tm,),
        in_specs=[
            pl.BlockSpec((tm, k1), lambda i: (i, 0)),
            pl.BlockSpec((tm, k1), lambda i: (i, col2)),
            pl.BlockSpec((None,) + w.shape[1:], lambda i: (layer, 0, 0)),
            pl.BlockSpec((tm, d), lambda i: (i, 0)),
            pl.BlockSpec((2, 9, d), lambda i: (0, 0, 0)),
        ],
        out_specs=pl.BlockSpec((tm, d), lambda i: (i, 0)),
        out_shape=jax.ShapeDtypeStruct((rows, d), F32),
        compiler_params=_params("arbitrary"),
        name="oproj",
    )(a1, a2, w, h, mod)


def _na_bias(rpb, n_rows):
    nh, n_a, n_b = rpb.shape
    nb = n_rows // NA_Q_ROWS
    w = jnp.full((nh, n_a, 128), MASKED, F32)
    w = w.at[..., :NA_COLS].set(rpb[..., NA_COLS - 1:]).at[..., 128 - (NA_COLS - 1):].set(rpb[..., :NA_COLS - 1])
    toep = jnp.tile(w, (1, 1, GRID_W))[..., :GRID_W * 127].reshape(nh, n_a, GRID_W, 127)[..., :GRID_W]
    toep = jnp.pad(toep.transpose(0, 2, 1, 3), ((0, 0), (0, 0), (NA_K_ROWS, NA_K_ROWS), (0, 0)),
                   constant_values=MASKED)

    q = np.arange(ATT_BLOCK)
    k = np.arange(NA_K_ROWS * GRID_W)
    dr, c = q // GRID_W, q % GRID_W
    kr, kc = k // GRID_W, k % GRID_W
    c0 = np.clip(c - NA_COLS // 2, 0, GRID_W - NA_COLS)
    col_ok = (kc[None, :] >= c0[:, None]) & (kc[None, :] < c0[:, None] + NA_COLS)
    tables, ok = [], []
    for b in (0, 1, nb - 1):
        r = b * NA_Q_ROWS + dr
        r0 = np.clip(r - NA_ROWS // 2, 0, n_rows - NA_ROWS)
        base = int(np.clip(b * NA_Q_ROWS - NA_ROWS // 2, 0, n_rows - NA_K_ROWS))
        kabs = base + kr
        ok.append((kabs[None, :] >= r0[:, None]) & (kabs[None, :] < r0[:, None] + NA_ROWS) & col_ok)
        per_row = []
        for j in range(NA_Q_ROWS):
            lo = base - (b * NA_Q_ROWS + j) + NA_ROWS - 1 + NA_K_ROWS
            per_row.append(toep[:, :, lo:lo + NA_K_ROWS].reshape(nh, GRID_W, NA_K_ROWS * GRID_W))
        tables.append(jnp.stack(per_row, axis=1).reshape(nh, ATT_BLOCK, NA_K_ROWS * GRID_W))
    return jnp.where(np.stack(ok)[:, None], jnp.stack(tables), MASKED)


def _na_kernel(q_ref, k_ref, v_ref, bias_ref, o_ref, *, n_lat, n_ctx, nb):
    b = pl.program_id(1)
    scale = HEAD_DIM ** -0.5
    nk = NA_K_ROWS * GRID_W
    start = pl.multiple_of(jnp.clip(b - 1, 0, nb - 3) * ATT_BLOCK, ATT_BLOCK)

    def one_head(hh, local):
        cols = slice(hh * HEAD_DIM, (hh + 1) * HEAD_DIM)
        q = q_ref[:, cols]
        s_ctx = _dot_nt(q, k_ref[n_lat:n_lat + n_ctx, cols]) * scale
        vc1 = _with_ones(v_ref[n_lat:n_lat + n_ctx, cols])
        if local:
            kl = k_ref[pl.ds(start, nk), cols]
            vl1 = _with_ones(v_ref[pl.ds(start, nk), cols])
            s_loc = _dot_nt(q, kl) * scale + bias_ref[0, hh]
            m = _row_max(s_loc, s_ctx)
            acc = _dot(jnp.exp(s_loc - m).astype(BF16), vl1) + _dot(jnp.exp(s_ctx - m).astype(BF16), vc1)
        else:
            acc = _dot(jnp.exp(s_ctx - _row_max(s_ctx)).astype(BF16), vc1)
        o_ref[:, cols] = (acc[:, :HEAD_DIM] / acc[:, HEAD_DIM:]).astype(o_ref.dtype)

    @pl.when(b < nb)
    def _():
        for hh in range(NA_HEADS_PER_STEP):
            one_head(hh, True)

    @pl.when(b >= nb)
    def _():
        for hh in range(NA_HEADS_PER_STEP):
            one_head(hh, False)


def _na_attention(p, bias, *, n_lat, n_ctx, heads):
    rows = n_lat + n_ctx
    nb = n_lat // ATT_BLOCK
    nblk = rows // ATT_BLOCK
    hps = NA_HEADS_PER_STEP
    width = hps * HEAD_DIM
    groups = heads // hps

    def pattern(b):
        return jnp.where(b == 0, 0, jnp.where(b == nb - 1, 2, 1))

    return pl.pallas_call(
        functools.partial(_na_kernel, n_lat=n_lat, n_ctx=n_ctx, nb=nb),
        grid=(groups, nblk),
        in_specs=[
            pl.BlockSpec((ATT_BLOCK, width), lambda h, b: (b, h)),
            pl.BlockSpec((rows, width), lambda h, b: (0, groups + h)),
            pl.BlockSpec((rows, width), lambda h, b: (0, 2 * groups + h)),
            pl.BlockSpec((1, hps, ATT_BLOCK, NA_K_ROWS * GRID_W), lambda h, b: (pattern(b), h, 0, 0)),
        ],
        out_specs=pl.BlockSpec((ATT_BLOCK, width), lambda h, b: (b, h)),
        out_shape=jax.ShapeDtypeStruct((rows, heads * HEAD_DIM), BF16),
        compiler_params=_params("arbitrary", "arbitrary"),
        name="na_attention",
    )(p, p, p, bias)


def _hgrn_levels():
    levels, m = [], 1
    while m < HG_CHUNK:
        levels.append(m)
        m *= 2
    return levels


def _hgrn_consts():
    c = HG_CHUNK
    levels = _hgrn_levels()
    fine = [m for m in levels if m < HG_ROW_LEVEL]
    idx = np.arange(c)
    t, j = idx[:, None], idx[None, :]
    n = np.zeros((2, 1 + len(fine), c, c), np.float32)
    msk = np.zeros((2, 1 + len(levels), c, c), np.float32)
    n[0, 0], n[1, 0] = j <= t, j >= t
    msk[0, 0] = msk[1, 0] = np.eye(c)
    for li, m in enumerate(levels):
        seg = idx // (2 * m)
        right = ((idx % (2 * m)) >= m)[:, None]
        last_left = (seg * 2 * m + m - 1)[:, None]
        first_right = last_left + 1
        if m < HG_ROW_LEVEL:
            n[0, 1 + li] = np.where(right, (j > last_left) & (j <= t), (j > t) & (j <= last_left))
            n[1, 1 + li] = np.where(right, (j >= first_right) & (j < t), (j >= t) & (j < first_right))
        same = seg[:, None] == seg[None, :]
        msk[0, 1 + li] = same & right & ~right.T
        msk[1, 1 + li] = same & ~right & right.T
    return n.reshape(2, -1, c), msk


def _hgrn_chunk(d, q, fx, v, lb, n_ref, msk_ref, s_scr, b_scr):
    c = HG_CHUNK
    t = jnp.exp(-jnp.abs(fx))
    r = 1.0 / (1.0 + t)
    log_sig = jnp.minimum(fx, 0.0) + jnp.log(r)
    sig_neg = jnp.where(fx >= 0.0, t * r, r)
    if lb is None:
        log_f, kk = log_sig, sig_neg
    else:
        la = jnp.log(lb)
        lc = jnp.log1p(-lb) + log_sig
        log_f = jnp.maximum(la, lc) + jnp.log(1.0 + jnp.exp(-jnp.abs(la - lc)))
        kk = (1.0 - lb) * sig_neg
    qf = q.astype(F32)

    hi = log_f.astype(BF16)
    mid = (log_f - hi.astype(F32)).astype(BF16)
    dd = _dot(n_ref[d], jnp.concatenate([hi, mid], axis=1))
    dd = dd[:, :HEAD_DIM] + dd[:, HEAD_DIM:]
    b = dd[:c]
    b_scr[d] = b

    attn = _dot_nt(q, kk.astype(BF16)) * msk_ref[d, 0]
    for li, m in enumerate(_hgrn_levels()):
        if m < HG_ROW_LEVEL:
            ex = dd[(1 + li) * c:(2 + li) * c]
        else:
            row = m - 1 + d
            ref = jnp.concatenate(
                [jnp.broadcast_to(b_scr[d, s + row:s + row + 1, :], (2 * m, HEAD_DIM)) for s in range(0, c, 2 * m)],
                axis=0)
            ex = -jnp.abs(b - ref)
        e = jnp.exp(ex)
        attn += _dot_nt((qf * e).astype(BF16), (kk * e).astype(BF16)) * msk_ref[d, 1 + li]

    total = b[c - 1:c] if d == 0 else b[0:1]
    state_t = s_scr[d]
    o = _dot_nt((qf * jnp.exp(b)).astype(BF16), state_t.astype(BF16)) + _dot(attn.astype(BF16), v)
    v_t = v.astype(F32).T.astype(BF16)
    s_scr[d] = jnp.exp(total) * state_t + _dot(v_t, (kk * jnp.exp(total - b)).astype(BF16))
    return o


def _hgrn_kernel(qf_ref, qb_ref, ff_ref, fb_ref, if_ref, ib_ref, lbl_ref, n_ref, msk_ref,
                 of_ref, ob_ref, s_scr, b_scr, *, layer_e):
    @pl.when(pl.program_id(1) == 0)
    def _():
        s_scr[...] = jnp.zeros_like(s_scr)

    for d, (q_ref, f_ref, i_ref, o_ref) in enumerate(((qf_ref, ff_ref, if_ref, of_ref),
                                                      (qb_ref, fb_ref, ib_ref, ob_ref))):
        lb = None
        if layer_e > 0:
            lg = lbl_ref[d]
            ex = jnp.exp(lg - jnp.max(lg, axis=0, keepdims=True))
            num = ex[1:2]
            for e in range(2, layer_e + 1):
                num = num + ex[e:e + 1]
            lb = num / jnp.sum(ex, axis=0, keepdims=True)
        o_ref[...] = _hgrn_chunk(d, q_ref[...], f_ref[...], i_ref[...], lb, n_ref, msk_ref, s_scr, b_scr)


def _hgrn_scan(p16, p32, lb_logits, consts, *, layer_e, n_lat, n_ctx, heads, qcol, icol):
    rows = n_lat + n_ctx
    n_mat, msk = consts
    lat_chunks = n_lat // HG_CHUNK
    qc, ic = qcol // HEAD_DIM, icol // HEAD_DIM

    def fwd(s):
        return jnp.where(s == 0, lat_chunks, s - 1)

    def bwd(s):
        return lat_chunks - s

    def spec(blk, col):
        return pl.BlockSpec((HG_CHUNK, HEAD_DIM), lambda h, s: (blk(s), col + h))

    out = jax.ShapeDtypeStruct((rows, heads * HEAD_DIM), F32)
    return pl.pallas_call(
        functools.partial(_hgrn_kernel, layer_e=layer_e),
        grid=(heads, 1 + lat_chunks),
        in_specs=[
            spec(fwd, qc), spec(bwd, qc),
            spec(fwd, 0), spec(bwd, heads),
            spec(fwd, ic), spec(bwd, ic),
            pl.BlockSpec((2, lb_logits.shape[1], HEAD_DIM), lambda h, s: (0, 0, h)),
            pl.BlockSpec(n_mat.shape, lambda h, s: (0, 0, 0)),
            pl.BlockSpec(msk.shape, lambda h, s: (0, 0, 0, 0)),
        ],
        out_specs=[spec(fwd, 0), spec(bwd, 0)],
        out_shape=[out, out],
        scratch_shapes=[pltpu.VMEM((2, HEAD_DIM, HEAD_DIM), F32), pltpu.VMEM((2, HG_CHUNK, HEAD_DIM), F32)],
        compiler_params=_params("arbitrary", "arbitrary"),
        name="hgrn_scan",
    )(p16, p16, p32, p32, p16, p16, lb_logits, n_mat, msk)


def _hgrn_out_kernel(of_ref, ob_ref, g_ref, ng_ref, y_ref):
    o = of_ref[...] + ob_ref[...]
    o = o * lax.rsqrt(jnp.mean(o * o, axis=-1, keepdims=True) + NORM_EPS)
    y_ref[...] = (o * ng_ref[...] * _silu(g_ref[...])).astype(y_ref.dtype)


def _hgrn_out(o_f, o_b, p, norm_g, *, heads, gcol0, tm):
    rows = o_f.shape[0]
    return pl.pallas_call(
        _hgrn_out_kernel,
        grid=(rows // tm, heads),
        in_specs=[
            pl.BlockSpec((tm, HEAD_DIM), lambda i, h: (i, h)),
            pl.BlockSpec((tm, HEAD_DIM), lambda i, h: (i, h)),
            pl.BlockSpec((tm, HEAD_DIM), lambda i, h: (i, gcol0 // HEAD_DIM + h)),
            pl.BlockSpec((1, HEAD_DIM), lambda i, h: (0, h)),
        ],
        out_specs=pl.BlockSpec((tm, HEAD_DIM), lambda i, h: (i, h)),
        out_shape=jax.ShapeDtypeStruct((rows, heads * HEAD_DIM), BF16),
        compiler_params=_params("arbitrary", "arbitrary"),
        name="hgrn_out",
    )(o_f, o_b, p, norm_g)


def _swa_kernel(q_ref, k_ref, v_ref, sink_ref, o_ref, *, n_lat, n_ctx, nb, group, span):
    b = pl.program_id(1)
    scale = HEAD_DIM ** -0.5
    kc = k_ref[n_lat:n_lat + n_ctx, :]
    vc1 = _with_ones(v_ref[n_lat:n_lat + n_ctx, :])

    def finish(g, s_ctx, s_loc, vl1):
        sink = sink_ref[0, g:g + 1, 0:1]
        blocks = (s_ctx,) if s_loc is None else (s_ctx, s_loc)
        m = jnp.maximum(_row_max(*blocks), sink)
        acc = _dot(jnp.exp(s_ctx - m).astype(BF16), vc1)
        if s_loc is not None:
            acc = acc + _dot(jnp.exp(s_loc - m).astype(BF16), vl1)
        den = acc[:, HEAD_DIM:] + jnp.exp(sink - m)
        o_ref[:, g * HEAD_DIM:(g + 1) * HEAD_DIM] = (acc[:, :HEAD_DIM] / den).astype(o_ref.dtype)

    @pl.when(b < nb)
    def _():
        start = pl.multiple_of(jnp.clip(b * ATT_BLOCK - WINDOW, 0, n_lat - span), WINDOW)
        kl = k_ref[pl.ds(start, span), :]
        vl = _with_ones(v_ref[pl.ds(start, span), :])
        qpos = b * ATT_BLOCK + lax.broadcasted_iota(jnp.int32, (ATT_BLOCK, span), 0)
        kpos = start + lax.broadcasted_iota(jnp.int32, (ATT_BLOCK, span), 1)
        valid = jnp.abs(kpos - qpos) <= WINDOW
        for g in range(group):
            q = q_ref[:, g * HEAD_DIM:(g + 1) * HEAD_DIM]
            s_loc = jnp.where(valid, _dot_nt(q, kl) * scale, MASKED)
            finish(g, _dot_nt(q, kc) * scale, s_loc, vl)

    @pl.when(b >= nb)
    def _():
        for g in range(group):
            q = q_ref[:, g * HEAD_DIM:(g + 1) * HEAD_DIM]
            finish(g, _dot_nt(q, kc) * scale, None, None)


def _swa_attention(p, sink, *, n_lat, n_ctx, heads, kv_heads):
    rows = n_lat + n_ctx
    group = heads // kv_heads
    nb = n_lat // ATT_BLOCK
    span = ATT_BLOCK + 2 * WINDOW
    sink_tab = jnp.broadcast_to(sink.astype(F32).reshape(kv_heads, group, 1), (kv_heads, group, HEAD_DIM))
    return pl.pallas_call(
        functools.partial(_swa_kernel, n_lat=n_lat, n_ctx=n_ctx, nb=nb, group=group, span=span),
        grid=(kv_heads, rows // ATT_BLOCK),
        in_specs=[
            pl.BlockSpec((ATT_BLOCK, group * HEAD_DIM), lambda k, b: (b, k)),
            pl.BlockSpec((rows, HEAD_DIM), lambda k, b: (0, heads + k)),
            pl.BlockSpec((rows, HEAD_DIM), lambda k, b: (0, heads + kv_heads + k)),
            pl.BlockSpec((1, group, HEAD_DIM), lambda k, b: (k, 0, 0)),
        ],
        out_specs=pl.BlockSpec((ATT_BLOCK, group * HEAD_DIM), lambda k, b: (b, k)),
        out_shape=jax.ShapeDtypeStruct((rows, heads * HEAD_DIM), BF16),
        compiler_params=_params("arbitrary", "arbitrary"),
        name="swa_attention",
    )(p, p, p, sink_tab)


def kernel(x, c, ctx, c_ctx, w_mod, b_mod, norm_g, w_ff_in, w_ff_out, w_in_even, w_out_even,
           na_rpb, hg_lb_logits, hg_norm_g, w_qkv_odd, w_o_odd, sink_odd, final_norm_g):
    assert x.shape[0] == 1
    depth, d = w_mod.shape[0], x.shape[2]
    n_lat, n_ctx = x.shape[1], ctx.shape[1]
    rows = n_lat + n_ctx
    tm = 768
    assert rows % tm == 0 and n_lat % 1024 == 0 and n_ctx == ATT_BLOCK
    na_heads = na_rpb.shape[1]
    a_w = na_heads * HEAD_DIM
    hg_heads = hg_norm_g.shape[1] // HEAD_DIM
    heads = sink_odd.shape[1]
    kv_heads = (w_qkv_odd.shape[2] // HEAD_DIM - heads) // 2

    h = jnp.concatenate([x[0], ctx[0]], axis=0)
    cc = jnp.zeros((8, d), F32).at[0].set(c[0]).at[1].set(c_ctx)
    mod_all = _modulation(cc, w_mod, b_mod)[:, :2].reshape(depth, 2, 9, d)

    cos, sin = _rope_tables(n_lat, n_ctx)
    hg_consts = _hgrn_consts()
    hg_consts = (jnp.asarray(hg_consts[0], BF16), jnp.asarray(hg_consts[1], F32))
    fg = final_norm_g.reshape(1, d)
    w_ff_in, w_ff_out = w_ff_in.astype(BF16), w_ff_out.astype(BF16)
    b_w = hg_heads * HEAD_DIM
    c_f, c_i, c_g = 3 * a_w + b_w, 3 * a_w + 3 * b_w, 3 * a_w + 4 * b_w
    w_in_even = jnp.concatenate([w_in_even[:, :, :c_f], w_in_even[:, :, c_i:c_g], w_in_even[:, :, c_f:c_i],
                                 w_in_even[:, :, c_g:]], axis=2).astype(BF16)
    w_out_even = w_out_even.astype(BF16)
    w_qkv_odd, w_o_odd = w_qkv_odd.astype(BF16), w_o_odd.astype(BF16)

    for l in range(depth):
        mod = mod_all[l]
        last = l == depth - 1
        h = _ffn(h, mod, norm_g[l, 0].reshape(1, d), w_ff_in, w_ff_out, fg, layer=l, half=0,
                 sub=0, rows=rows, tm=tm, n_lat=n_lat)
        g1 = norm_g[l, 1].reshape(1, d)
        if l % 2 == 0:
            e = l // 2
            p16, p32 = _proj(h, mod, g1, w_in_even, cos, sin, layer=e, tm=tm, tn=1024, n_lat=n_lat,
                             rope_cols=0, cols16=3 * a_w + 2 * b_w)
            ya = _na_attention(p16, _na_bias(na_rpb[e], n_lat // GRID_W), n_lat=n_lat, n_ctx=n_ctx, heads=na_heads)
            o_f, o_b = _hgrn_scan(p16, p32, hg_lb_logits, hg_consts, layer_e=e, n_lat=n_lat, n_ctx=n_ctx,
                                  heads=hg_heads, qcol=3 * a_w, icol=3 * a_w + b_w)
            yb = _hgrn_out(o_f, o_b, p32, hg_norm_g[e].reshape(1, -1), heads=hg_heads, gcol0=2 * b_w, tm=tm)
            h = _oproj(ya, yb, 0, w_out_even, h, mod, layer=e, tm=tm, n_lat=n_lat)
        else:
            o = l // 2
            p16, = _proj(h, mod, g1, w_qkv_odd, cos, sin, layer=o, tm=tm, tn=512, n_lat=n_lat,
                         rope_cols=(heads + kv_heads) * HEAD_DIM, cols16=w_qkv_odd.shape[2])
            y = _swa_attention(p16, sink_odd[o], n_lat=n_lat, n_ctx=n_ctx, heads=heads, kv_heads=kv_heads)
            h = _oproj(y, y, 1, w_o_odd, h, mod, layer=o, tm=tm, n_lat=n_lat)
        g2 = norm_g[l, 2].reshape(1, d)
        if last:
            h = _ffn(h, mod, g2, w_ff_in, w_ff_out, fg, layer=l, half=1,
                     sub=2, rows=n_lat, tm=1024, n_lat=n_lat, final_norm=True)
        else:
            h = _ffn(h, mod, g2, w_ff_in, w_ff_out, fg, layer=l, half=1,
                     sub=2, rows=rows, tm=tm, n_lat=n_lat)
    return h[None]
```

```python
import functools

import numpy as np
import jax
import jax.numpy as jnp
from jax import lax
from jax.experimental import pallas as pl
from jax.experimental.pallas import tpu as pltpu

F32 = jnp.float32
BF16 = jnp.bfloat16

GRID_W = 64
NORM_EPS = 1e-6
ROPE_THETA = 10000.0
HEAD_DIM = 128
NA_ROWS = 8
NA_COLS = 16
NA_Q_ROWS = 4
NA_K_ROWS = NA_Q_ROWS + NA_ROWS
ATT_BLOCK = NA_Q_ROWS * GRID_W
NA_HEADS_PER_STEP = 2
WINDOW = 128
HG_CHUNK = 256
ROW_CHUNK = 16
FFN_COLS = 512
FFN_TAIL_COLS = 256
FFN_OUT_COLS = 512
ROW_UNROLL = 4
HG_ROW_LEVEL = 8
MASKED = -1e30
VMEM_LIMIT = 56 * 1024 * 1024


def _params(*sem):
    return pltpu.CompilerParams(dimension_semantics=sem, vmem_limit_bytes=VMEM_LIMIT)


def _dot(a, b):
    return jnp.dot(a, b, preferred_element_type=F32)


def _dot_nt(a, b):
    return lax.dot_general(a, b, (((1,), (1,)), ((), ())), preferred_element_type=F32)


def _sigmoid(x):
    return 1.0 / (1.0 + jnp.exp(-x))


def _silu(x):
    return x * _sigmoid(x)


def _mod_kernel(c_ref, w_ref, b_ref, o_ref):
    s = _silu(c_ref[...]).astype(BF16)
    o_ref[0] = _dot(s, w_ref[0].astype(BF16)) + b_ref[0]


def _modulation(cc, w_mod, b_mod):
    depth, d, n = w_mod.shape
    tn = 1024
    return pl.pallas_call(
        _mod_kernel,
        grid=(depth, n // tn),
        in_specs=[
            pl.BlockSpec((8, d), lambda l, j: (0, 0)),
            pl.BlockSpec((1, d, tn), lambda l, j: (l, 0, j)),
            pl.BlockSpec((1, 1, tn), lambda l, j: (l, 0, j)),
        ],
        out_specs=pl.BlockSpec((1, 8, tn), lambda l, j: (l, 0, j)),
        out_shape=jax.ShapeDtypeStruct((depth, 8, n), F32),
        compiler_params=_params("arbitrary", "arbitrary"),
        name="modulation",
    )(cc, w_mod, b_mod.reshape(depth, 1, n))


def _mod_rows(mod_ref, sub, k, is_ctx):
    r = 3 * sub + k
    return jnp.where(is_ctx, mod_ref[1, r:r + 1, :], mod_ref[0, r:r + 1, :])


def _row_max(*blocks):
    acc = None
    for s in blocks:
        for c0 in range(0, s.shape[1], HEAD_DIM):
            t = s[:, c0:c0 + HEAD_DIM]
            acc = t if acc is None else jnp.maximum(acc, t)
    return jnp.max(acc, axis=-1, keepdims=True)


def _with_ones(v):
    return jnp.concatenate([v, jnp.ones_like(v)], axis=1)


def _is_ctx(tile, tm, n_lat):
    row = tile * tm + lax.broadcasted_iota(jnp.int32, (tm, 1), 0)
    return row >= n_lat


def _row_chunks(row0, tm, n_lat, body):
    def step(c, carry):
        r0 = pl.multiple_of(c * ROW_CHUNK, ROW_CHUNK)
        body(pl.ds(r0, ROW_CHUNK), (row0 + r0 >= n_lat).astype(jnp.int32))
        return carry

    lax.fori_loop(0, tm // ROW_CHUNK, step, 0, unroll=ROW_UNROLL)


def _adaln_rows(h_ref, mod_ref, g_ref, u_scr, gs_scr, *, sub, row0, tm, n_lat):
    for which in range(2):
        gs_scr[which] = g_ref[...] * (1.0 + mod_ref[which, 3 * sub + 1:3 * sub + 2, :])

    def body(rows, which):
        x = h_ref[rows, :]
        r = lax.rsqrt(jnp.mean(x * x, axis=-1, keepdims=True) + NORM_EPS)
        u_scr[rows, :] = ((x * r) * gs_scr[which] + mod_ref[which, 3 * sub:3 * sub + 1, :]).astype(BF16)

    _row_chunks(row0, tm, n_lat, body)


def _ffn_kernel(h_ref, mod_ref, g_ref, wg_ref, wu_ref, wo_ref, fg_ref, *rest,
                sub, tm, row0, n_lat, final_norm, emit16, aliased):
    rest = rest[1:] if aliased else rest
    o_ref = rest[0]
    u_scr, gs_scr = rest[-2:]
    i, j = pl.program_id(0), pl.program_id(1)
    first_row = row0 + i * tm

    @pl.when(j == 0)
    def _():
        _adaln_rows(h_ref, mod_ref, g_ref, u_scr, gs_scr, sub=sub, row0=first_row, tm=tm, n_lat=n_lat)
        o_ref[...] = jnp.zeros_like(o_ref)

    if emit16:
        wg16_ref, wu16_ref, wo16_ref = rest[1:4]
        wg16_ref[...] = wg_ref[...].astype(BF16)
        wu16_ref[...] = wu_ref[...].astype(BF16)
        wo16_ref[...] = wo_ref[...].astype(BF16)
        wg_ref, wu_ref, wo_ref = wg16_ref, wu16_ref, wo16_ref
    u = u_scr[...]
    a = (_silu(_dot(u, wg_ref[...])) * _dot(u, wu_ref[...])).astype(BF16)
    d = o_ref.shape[1]
    for c0 in range(0, d, FFN_OUT_COLS):
        o_ref[:, c0:c0 + FFN_OUT_COLS] += _dot(a, wo_ref[:, c0:c0 + FFN_OUT_COLS])

    @pl.when(j == pl.num_programs(1) - 1)
    def _():
        def body(rows, which):
            h = h_ref[rows, :] + (0.5 * mod_ref[which, 3 * sub + 2:3 * sub + 3, :]) * o_ref[rows, :]
            if final_norm:
                h = h * lax.rsqrt(jnp.mean(h * h, axis=-1, keepdims=True) + NORM_EPS) * fg_ref[...]
            o_ref[rows, :] = h

        _row_chunks(first_row, tm, n_lat, body)


def _ffn(h, mod, g, w_in, w_out, fg, *, layer, half, sub, rows, tm, n_lat, final_norm=False):
    d = h.shape[1]
    f = w_out.shape[2]
    tail = rows % tm or tm
    n_main = (rows - tail) // tm
    assert (rows - tail) % tail == 0
    common = dict(sub=sub, n_lat=n_lat, final_norm=final_norm)
    small = [
        pl.BlockSpec((2, 9, d), lambda i, j: (0, 0, 0)),
        pl.BlockSpec((1, d), lambda i, j: (0, 0)),
    ]
    fg_spec = pl.BlockSpec((1, d), lambda i, j: (0, 0))

    tf, t0 = FFN_TAIL_COLS, (rows - tail) // tail
    nf = f // tf
    out, wg16, wu16, wo16 = pl.pallas_call(
        functools.partial(_ffn_kernel, tm=tail, row0=rows - tail, emit16=True, aliased=False, **common),
        grid=(1, nf),
        in_specs=[pl.BlockSpec((tail, d), lambda i, j: (t0, 0))] + small + [
            pl.BlockSpec((None, None, d, tf), lambda i, j: (layer, half, 0, j)),
            pl.BlockSpec((None, None, d, tf), lambda i, j: (layer, half, 0, nf + j)),
            pl.BlockSpec((None, None, tf, d), lambda i, j: (layer, half, j, 0)),
            fg_spec,
        ],
        out_specs=[
            pl.BlockSpec((tail, d), lambda i, j: (t0, 0)),
            pl.BlockSpec((d, tf), lambda i, j: (0, j)),
            pl.BlockSpec((d, tf), lambda i, j: (0, j)),
            pl.BlockSpec((tf, d), lambda i, j: (j, 0)),
        ],
        out_shape=[
            jax.ShapeDtypeStruct((rows, d), F32),
            jax.ShapeDtypeStruct((d, f), BF16),
            jax.ShapeDtypeStruct((d, f), BF16),
            jax.ShapeDtypeStruct((f, d), BF16),
        ],
        scratch_shapes=[pltpu.VMEM((tail, d), BF16), pltpu.VMEM((2, 1, d), F32)],
        compiler_params=_params("arbitrary", "arbitrary"),
        name="ffn_tail",
    )(h, mod, g, w_in, w_in, w_out, fg)

    tf = FFN_COLS
    return pl.pallas_call(
        functools.partial(_ffn_kernel, tm=tm, row0=0, emit16=False, aliased=True, **common),
        grid=(n_main, f // tf),
        in_specs=[pl.BlockSpec((tm, d), lambda i, j: (i, 0))] + small + [
            pl.BlockSpec((d, tf), lambda i, j: (0, j)),
            pl.BlockSpec((d, tf), lambda i, j: (0, j)),
            pl.BlockSpec((tf, d), lambda i, j: (j, 0)),
            fg_spec,
            pl.BlockSpec(memory_space=pl.ANY),
        ],
        out_specs=pl.BlockSpec((tm, d), lambda i, j: (i, 0)),
        out_shape=jax.ShapeDtypeStruct((rows, d), F32),
        input_output_aliases={7: 0},
        scratch_shapes=[pltpu.VMEM((tm, d), BF16), pltpu.VMEM((2, 1, d), F32)],
        compiler_params=_params("arbitrary", "arbitrary"),
        name="ffn",
    )(h, mod, g, wg16, wu16, wo16, fg, out)


def _rope(x, cos, sin):
    lane = lax.broadcasted_iota(jnp.int32, x.shape, 1)
    first = (lane % 64) < 32
    swapped = jnp.where(first, pltpu.roll(x, 96, 1), pltpu.roll(x, 32, 1))
    return x * cos + swapped * sin


def _proj_kernel(h_ref, mod_ref, g_ref, w_ref, cos_ref, sin_ref, *rest, tm, tn, n_lat, rope_tiles, n16, has32):
    o16_ref = rest[0]
    o32_ref = rest[1] if has32 else None
    u_scr, gs_scr = rest[-2:]
    i, j = pl.program_id(0), pl.program_id(1)

    @pl.when(j == 0)
    def _():
        _adaln_rows(h_ref, mod_ref, g_ref, u_scr, gs_scr, sub=1, row0=i * tm, tm=tm, n_lat=n_lat)

    y = _dot(u_scr[...], w_ref[...])

    def write16():
        if rope_tiles == 0:
            o16_ref[...] = y.astype(BF16)
            return
        cos, sin = cos_ref[...], sin_ref[...]
        for hd in range(tn // HEAD_DIM):
            sl = slice(hd * HEAD_DIM, (hd + 1) * HEAD_DIM)
            o16_ref[:, sl] = _rope(y[:, sl], cos, sin).astype(BF16)

    if has32:
        pl.when(j < n16)(write16)

        @pl.when(j >= n16)
        def _():
            o32_ref[...] = y
    else:
        write16()


def _proj(h, mod, g, w, cos, sin, *, layer, tm, tn, n_lat, rope_cols, cols16):
    rows, d = h.shape
    n = w.shape[2]
    n16 = cols16 // tn
    rope_tiles = rope_cols // tn
    has32 = n > cols16
    out_specs = [pl.BlockSpec((tm, tn), lambda i, j: (i, jnp.minimum(j, n16 - 1)))]
    out_shape = [jax.ShapeDtypeStruct((rows, cols16), BF16)]
    if has32:
        out_specs.append(pl.BlockSpec((tm, tn), lambda i, j: (i, jnp.maximum(j - n16, 0))))
        out_shape.append(jax.ShapeDtypeStruct((rows, n - cols16), F32))
    return pl.pallas_call(
        functools.partial(_proj_kernel, tm=tm, tn=tn, n_lat=n_lat, rope_tiles=rope_tiles, n16=n16,
                          has32=has32),
        grid=(rows // tm, n // tn),
        in_specs=[
            pl.BlockSpec((tm, d), lambda i, j: (i, 0)),
            pl.BlockSpec((2, 9, d), lambda i, j: (0, 0, 0)),
            pl.BlockSpec((1, d), lambda i, j: (0, 0)),
            pl.BlockSpec((None, d, tn), lambda i, j: (layer, 0, j)),
            pl.BlockSpec((None, tm, HEAD_DIM), lambda i, j: (jnp.where(j < rope_tiles, 0, 1), i, 0)),
            pl.BlockSpec((None, tm, HEAD_DIM), lambda i, j: (jnp.where(j < rope_tiles, 0, 1), i, 0)),
        ],
        out_specs=out_specs,
        out_shape=out_shape,
        scratch_shapes=[pltpu.VMEM((tm, d), BF16), pltpu.VMEM((2, 1, d), F32)],
        compiler_params=_params("arbitrary", "arbitrary"),
        name="proj",
    )(h, mod, g, w, cos, sin)


def _rope_tables(n_lat, n_ctx):
    t = np.arange(n_lat)
    inv = ROPE_THETA ** (-jnp.arange(0, 64, 2, dtype=F32) / 64)
    ang_r = jnp.asarray(t // GRID_W, F32)[:, None] * inv[None, :]
    ang_c = jnp.asarray(t % GRID_W, F32)[:, None] * inv[None, :]
    cr, sr, cc, sc = jnp.cos(ang_r), jnp.sin(ang_r), jnp.cos(ang_c), jnp.sin(ang_c)
    cos = jnp.concatenate([cr, cr, cc, cc], axis=1)
    sin = jnp.concatenate([-sr, sr, -sc, sc], axis=1)
    cos = jnp.concatenate([cos, jnp.ones((n_ctx, HEAD_DIM), F32)], axis=0)
    sin = jnp.concatenate([sin, jnp.zeros((n_ctx, HEAD_DIM), F32)], axis=0)
    return jnp.stack([cos, jnp.ones_like(cos)]), jnp.stack([sin, jnp.zeros_like(sin)])


def _oproj_kernel(a1_ref, a2_ref, w_ref, h_ref, mod_ref, o_ref, *, tm, n_lat, k1):
    is_ctx = _is_ctx(pl.program_id(0), tm, n_lat)
    y = _dot(a1_ref[...], w_ref[:k1, :]) + _dot(a2_ref[...], w_ref[k1:, :])
    o_ref[...] = h_ref[...] + _mod_rows(mod_ref, 1, 2, is_ctx) * y


def _oproj(a1, a2, col2, w, h, mod, *, layer, tm, n_lat):
    rows, d = h.shape
    k1 = w.shape[1] // 2
    return pl.pallas_call(
        functools.partial(_oproj_kernel, tm=tm, n_lat=n_lat, k1=k1),
        grid=(rows // tm,),
        in_specs=[
            pl.BlockSpec((tm, k1), lambda i: (i, 0)),
            pl.BlockSpec((tm, k1), lambda i: (i, col2)),
            pl.BlockSpec((None,) + w.shape[1:], lambda i: (layer, 0, 0)),
            pl.BlockSpec((tm, d), lambda i: (i, 0)),
            pl.BlockSpec((2, 9, d), lambda i: (0, 0, 0)),
        ],
        out_specs=pl.BlockSpec((tm, d), lambda i: (i, 0)),
        out_shape=jax.ShapeDtypeStruct((rows, d), F32),
        compiler_params=_params("arbitrary"),
        name="oproj",
    )(a1, a2, w, h, mod)


def _na_bias(rpb, n_rows):
    nh, n_a, n_b = rpb.shape
    nb = n_rows // NA_Q_ROWS
    w = jnp.full((nh, n_a, 128), MASKED, F32)
    w = w.at[..., :NA_COLS].set(rpb[..., NA_COLS - 1:]).at[..., 128 - (NA_COLS - 1):].set(rpb[..., :NA_COLS - 1])
    toep = jnp.tile(w, (1, 1, GRID_W))[..., :GRID_W * 127].reshape(nh, n_a, GRID_W, 127)[..., :GRID_W]
    toep = jnp.pad(toep.transpose(0, 2, 1, 3), ((0, 0), (0, 0), (NA_K_ROWS, NA_K_ROWS), (0, 0)),
                   constant_values=MASKED)

    q = np.arange(ATT_BLOCK)
    k = np.arange(NA_K_ROWS * GRID_W)
    dr, c = q // GRID_W, q % GRID_W
    kr, kc = k // GRID_W, k % GRID_W
    c0 = np.clip(c - NA_COLS // 2, 0, GRID_W - NA_COLS)
    col_ok = (kc[None, :] >= c0[:, None]) & (kc[None, :] < c0[:, None] + NA_COLS)
    tables, ok = [], []
    for b in (0, 1, nb - 1):
        r = b * NA_Q_ROWS + dr
        r0 = np.clip(r - NA_ROWS // 2, 0, n_rows - NA_ROWS)
        base = int(np.clip(b * NA_Q_ROWS - NA_ROWS // 2, 0, n_rows - NA_K_ROWS))
        kabs = base + kr
        ok.append((kabs[None, :] >= r0[:, None]) & (kabs[None, :] < r0[:, None] + NA_ROWS) & col_ok)
        per_row = []
        for j in range(NA_Q_ROWS):
            lo = base - (b * NA_Q_ROWS + j) + NA_ROWS - 1 + NA_K_ROWS
            per_row.append(toep[:, :, lo:lo + NA_K_ROWS].reshape(nh, GRID_W, NA_K_ROWS * GRID_W))
        tables.append(jnp.stack(per_row, axis=1).reshape(nh, ATT_BLOCK, NA_K_ROWS * GRID_W))
    return jnp.where(np.stack(ok)[:, None], jnp.stack(tables), MASKED)


def _na_kernel(q_ref, k_ref, v_ref, bias_ref, o_ref, *, n_lat, n_ctx, nb):
    b = pl.program_id(1)
    scale = HEAD_DIM ** -0.5
    nk = NA_K_ROWS * GRID_W
    start = pl.multiple_of(jnp.clip(b - 1, 0, nb - 3) * ATT_BLOCK, ATT_BLOCK)

    def one_head(hh, local):
        cols = slice(hh * HEAD_DIM, (hh + 1) * HEAD_DIM)
        q = q_ref[:, cols]
        s_ctx = _dot_nt(q, k_ref[n_lat:n_lat + n_ctx, cols]) * scale
        vc1 = _with_ones(v_ref[n_lat:n_lat + n_ctx, cols])
        if local:
            kl = k_ref[pl.ds(start, nk), cols]
            vl1 = _with_ones(v_ref[pl.ds(start, nk), cols])
            s_loc = _dot_nt(q, kl) * scale + bias_ref[0, hh]
            m = _row_max(s_loc, s_ctx)
            acc = _dot(jnp.exp(s_loc - m).astype(BF16), vl1) + _dot(jnp.exp(s_ctx - m).astype(BF16), vc1)
        else:
            acc = _dot(jnp.exp(s_ctx - _row_max(s_ctx)).astype(BF16), vc1)
        o_ref[:, cols] = (acc[:, :HEAD_DIM] / acc[:, HEAD_DIM:]).astype(o_ref.dtype)

    @pl.when(b < nb)
    def _():
        for hh in range(NA_HEADS_PER_STEP):
            one_head(hh, True)

    @pl.when(b >= nb)
    def _():
        for hh in range(NA_HEADS_PER_STEP):
            one_head(hh, False)


def _na_attention(p, bias, *, n_lat, n_ctx, heads):
    rows = n_lat + n_ctx
    nb = n_lat // ATT_BLOCK
    nblk = rows // ATT_BLOCK
    hps = NA_HEADS_PER_STEP
    width = hps * HEAD_DIM
    groups = heads // hps

    def pattern(b):
        return jnp.where(b == 0, 0, jnp.where(b == nb - 1, 2, 1))

    return pl.pallas_call(
        functools.partial(_na_kernel, n_lat=n_lat, n_ctx=n_ctx, nb=nb),
        grid=(groups, nblk),
        in_specs=[
            pl.BlockSpec((ATT_BLOCK, width), lambda h, b: (b, h)),
            pl.BlockSpec((rows, width), lambda h, b: (0, groups + h)),
            pl.BlockSpec((rows, width), lambda h, b: (0, 2 * groups + h)),
            pl.BlockSpec((1, hps, ATT_BLOCK, NA_K_ROWS * GRID_W), lambda h, b: (pattern(b), h, 0, 0)),
        ],
        out_specs=pl.BlockSpec((ATT_BLOCK, width), lambda h, b: (b, h)),
        out_shape=jax.ShapeDtypeStruct((rows, heads * HEAD_DIM), BF16),
        compiler_params=_params("arbitrary", "arbitrary"),
        name="na_attention",
    )(p, p, p, bias)


def _hgrn_levels():
    levels, m = [], 1
    while m < HG_CHUNK:
        levels.append(m)
        m *= 2
    return levels


def _hgrn_consts():
    c = HG_CHUNK
    levels = _hgrn_levels()
    fine = [m for m in levels if m < HG_ROW_LEVEL]
    idx = np.arange(c)
    t, j = idx[:, None], idx[None, :]
    n = np.zeros((2, 1 + len(fine), c, c), np.float32)
    msk = np.zeros((2, 1 + len(levels), c, c), np.float32)
    n[0, 0], n[1, 0] = j <= t, j >= t
    msk[0, 0] = msk[1, 0] = np.eye(c)
    for li, m in enumerate(levels):
        seg = idx // (2 * m)
        right = ((idx % (2 * m)) >= m)[:, None]
        last_left = (seg * 2 * m + m - 1)[:, None]
        first_right = last_left + 1
        if m < HG_ROW_LEVEL:
            n[0, 1 + li] = np.where(right, (j > last_left) & (j <= t), (j > t) & (j <= last_left))
            n[1, 1 + li] = np.where(right, (j >= first_right) & (j < t), (j >= t) & (j < first_right))
        same = seg[:, None] == seg[None, :]
        msk[0, 1 + li] = same & right & ~right.T
        msk[1, 1 + li] = same & ~right & right.T
    return n.reshape(2, -1, c), msk


def _hgrn_chunk(d, q, fx, v, lb, n_ref, msk_ref, s_scr, b_scr):
    c = HG_CHUNK
    t = jnp.exp(-jnp.abs(fx))
    r = 1.0 / (1.0 + t)
    log_sig = jnp.minimum(fx, 0.0) + jnp.log(r)
    sig_neg = jnp.where(fx >= 0.0, t * r, r)
    if lb is None:
        log_f, kk = log_sig, sig_neg
    else:
        la = jnp.log(lb)
        lc = jnp.log1p(-lb) + log_sig
        log_f = jnp.maximum(la, lc) + jnp.log(1.0 + jnp.exp(-jnp.abs(la - lc)))
        kk = (1.0 - lb) * sig_neg
    qf = q.astype(F32)

    hi = log_f.astype(BF16)
    mid = (log_f - hi.astype(F32)).astype(BF16)
    dd = _dot(n_ref[d], jnp.concatenate([hi, mid], axis=1))
    dd = dd[:, :HEAD_DIM] + dd[:, HEAD_DIM:]
    b = dd[:c]
    b_scr[d] = b

    attn = _dot_nt(q, kk.astype(BF16)) * msk_ref[d, 0]
    for li, m in enumerate(_hgrn_levels()):
        if m < HG_ROW_LEVEL:
            ex = dd[(1 + li) * c:(2 + li) * c]
        else:
            row = m - 1 + d
            ref = jnp.concatenate(
                [jnp.broadcast_to(b_scr[d, s + row:s + row + 1, :], (2 * m, HEAD_DIM)) for s in range(0, c, 2 * m)],
                axis=0)
            ex = -jnp.abs(b - ref)
        e = jnp.exp(ex)
        attn += _dot_nt((qf * e).astype(BF16), (kk * e).astype(BF16)) * msk_ref[d, 1 + li]

    total = b[c - 1:c] if d == 0 else b[0:1]
    state_t = s_scr[d]
    o = _dot_nt((qf * jnp.exp(b)).astype(BF16), state_t.astype(BF16)) + _dot(attn.astype(BF16), v)
    v_t = v.astype(F32).T.astype(BF16)
    s_scr[d] = jnp.exp(total) * state_t + _dot(v_t, (kk * jnp.exp(total - b)).astype(BF16))
    return o


def _hgrn_kernel(qf_ref, qb_ref, ff_ref, fb_ref, if_ref, ib_ref, lbl_ref, n_ref, msk_ref,
                 of_ref, ob_ref, s_scr, b_scr, *, layer_e):
    @pl.when(pl.program_id(1) == 0)
    def _():
        s_scr[...] = jnp.zeros_like(s_scr)

    for d, (q_ref, f_ref, i_ref, o_ref) in enumerate(((qf_ref, ff_ref, if_ref, of_ref),
                                                      (qb_ref, fb_ref, ib_ref, ob_ref))):
        lb = None
        if layer_e > 0:
            lg = lbl_ref[d]
            ex = jnp.exp(lg - jnp.max(lg, axis=0, keepdims=True))
            num = ex[1:2]
            for e in range(2, layer_e + 1):
                num = num + ex[e:e + 1]
            lb = num / jnp.sum(ex, axis=0, keepdims=True)
        o_ref[...] = _hgrn_chunk(d, q_ref[...], f_ref[...], i_ref[...], lb, n_ref, msk_ref, s_scr, b_scr)


def _hgrn_scan(p16, p32, lb_logits, consts, *, layer_e, n_lat, n_ctx, heads, qcol, icol):
    rows = n_lat + n_ctx
    n_mat, msk = consts
    lat_chunks = n_lat // HG_CHUNK
    qc, ic = qcol // HEAD_DIM, icol // HEAD_DIM

    def fwd(s):
        return jnp.where(s == 0, lat_chunks, s - 1)

    def bwd(s):
        return lat_chunks - s

    def spec(blk, col):
        return pl.BlockSpec((HG_CHUNK, HEAD_DIM), lambda h, s: (blk(s), col + h))

    out = jax.ShapeDtypeStruct((rows, heads * HEAD_DIM), F32)
    return pl.pallas_call(
        functools.partial(_hgrn_kernel, layer_e=layer_e),
        grid=(heads, 1 + lat_chunks),
        in_specs=[
            spec(fwd, qc), spec(bwd, qc),
            spec(fwd, 0), spec(bwd, heads),
            spec(fwd, ic), spec(bwd, ic),
            pl.BlockSpec((2, lb_logits.shape[1], HEAD_DIM), lambda h, s: (0, 0, h)),
            pl.BlockSpec(n_mat.shape, lambda h, s: (0, 0, 0)),
            pl.BlockSpec(msk.shape, lambda h, s: (0, 0, 0, 0)),
        ],
        out_specs=[spec(fwd, 0), spec(bwd, 0)],
        out_shape=[out, out],
        scratch_shapes=[pltpu.VMEM((2, HEAD_DIM, HEAD_DIM), F32), pltpu.VMEM((2, HG_CHUNK, HEAD_DIM), F32)],
        compiler_params=_params("arbitrary", "arbitrary"),
        name="hgrn_scan",
    )(p16, p16, p32, p32, p16, p16, lb_logits, n_mat, msk)


def _hgrn_out_kernel(of_ref, ob_ref, g_ref, ng_ref, y_ref):
    o = of_ref[...] + ob_ref[...]
    o = o * lax.rsqrt(jnp.mean(o * o, axis=-1, keepdims=True) + NORM_EPS)
    y_ref[...] = (o * ng_ref[...] * _silu(g_ref[...])).astype(y_ref.dtype)


def _hgrn_out(o_f, o_b, p, norm_g, *, heads, gcol0, tm):
    rows = o_f.shape[0]
    return pl.pallas_call(
        _hgrn_out_kernel,
        grid=(rows // tm, heads),
        in_specs=[
            pl.BlockSpec((tm, HEAD_DIM), lambda i, h: (i, h)),
            pl.BlockSpec((tm, HEAD_DIM), lambda i, h: (i, h)),
            pl.BlockSpec((tm, HEAD_DIM), lambda i, h: (i, gcol0 // HEAD_DIM + h)),
            pl.BlockSpec((1, HEAD_DIM), lambda i, h: (0, h)),
        ],
        out_specs=pl.BlockSpec((tm, HEAD_DIM), lambda i, h: (i, h)),
        out_shape=jax.ShapeDtypeStruct((rows, heads * HEAD_DIM), BF16),
        compiler_params=_params("arbitrary", "arbitrary"),
        name="hgrn_out",
    )(o_f, o_b, p, norm_g)


def _swa_kernel(q_ref, k_ref, v_ref, sink_ref, o_ref, *, n_lat, n_ctx, nb, group, span):
    b = pl.program_id(1)
    scale = HEAD_DIM ** -0.5
    kc = k_ref[n_lat:n_lat + n_ctx, :]
    vc1 = _with_ones(v_ref[n_lat:n_lat + n_ctx, :])

    def finish(g, s_ctx, s_loc, vl1):
        sink = sink_ref[0, g:g + 1, 0:1]
        blocks = (s_ctx,) if s_loc is None else (s_ctx, s_loc)
        m = jnp.maximum(_row_max(*blocks), sink)
        acc = _dot(jnp.exp(s_ctx - m).astype(BF16), vc1)
        if s_loc is not None:
            acc = acc + _dot(jnp.exp(s_loc - m).astype(BF16), vl1)
        den = acc[:, HEAD_DIM:] + jnp.exp(sink - m)
        o_ref[:, g * HEAD_DIM:(g + 1) * HEAD_DIM] = (acc[:, :HEAD_DIM] / den).astype(o_ref.dtype)

    @pl.when(b < nb)
    def _():
        start = pl.multiple_of(jnp.clip(b * ATT_BLOCK - WINDOW, 0, n_lat - span), WINDOW)
        kl = k_ref[pl.ds(start, span), :]
        vl = _with_ones(v_ref[pl.ds(start, span), :])
        qpos = b * ATT_BLOCK + lax.broadcasted_iota(jnp.int32, (ATT_BLOCK, span), 0)
        kpos = start + lax.broadcasted_iota(jnp.int32, (ATT_BLOCK, span), 1)
        valid = jnp.abs(kpos - qpos) <= WINDOW
        for g in range(group):
            q = q_ref[:, g * HEAD_DIM:(g + 1) * HEAD_DIM]
            s_loc = jnp.where(valid, _dot_nt(q, kl) * scale, MASKED)
            finish(g, _dot_nt(q, kc) * scale, s_loc, vl)

    @pl.when(b >= nb)
    def _():
        for g in range(group):
            q = q_ref[:, g * HEAD_DIM:(g + 1) * HEAD_DIM]
            finish(g, _dot_nt(q, kc) * scale, None, None)


def _swa_attention(p, sink, *, n_lat, n_ctx, heads, kv_heads):
    rows = n_lat + n_ctx
    group = heads // kv_heads
    nb = n_lat // ATT_BLOCK
    span = ATT_BLOCK + 2 * WINDOW
    sink_tab = jnp.broadcast_to(sink.astype(F32).reshape(kv_heads, group, 1), (kv_heads, group, HEAD_DIM))
    return pl.pallas_call(
        functools.partial(_swa_kernel, n_lat=n_lat, n_ctx=n_ctx, nb=nb, group=group, span=span),
        grid=(kv_heads, rows // ATT_BLOCK),
        in_specs=[
            pl.BlockSpec((ATT_BLOCK, group * HEAD_DIM), lambda k, b: (b, k)),
            pl.BlockSpec((rows, HEAD_DIM), lambda k, b: (0, heads + k)),
            pl.BlockSpec((rows, HEAD_DIM), lambda k, b: (0, heads + kv_heads + k)),
            pl.BlockSpec((1, group, HEAD_DIM), lambda k, b: (k, 0, 0)),
        ],
        out_specs=pl.BlockSpec((ATT_BLOCK, group * HEAD_DIM), lambda k, b: (b, k)),
        out_shape=jax.ShapeDtypeStruct((rows, heads * HEAD_DIM), BF16),
        compiler_params=_params("arbitrary", "arbitrary"),
        name="swa_attention",
    )(p, p, p, sink_tab)


def kernel(x, c, ctx, c_ctx, w_mod, b_mod, norm_g, w_ff_in, w_ff_out, w_in_even, w_out_even,
           na_rpb, hg_lb_logits, hg_norm_g, w_qkv_odd, w_o_odd, sink_odd, final_norm_g):
    assert x.shape[0] == 1
    depth, d = w_mod.shape[0], x.shape[2]
    n_lat, n_ctx = x.shape[1], ctx.shape[1]
    rows = n_lat + n_ctx
    tm = 768
    assert rows % tm == 0 and n_lat % 1024 == 0 and n_ctx == ATT_BLOCK
    na_heads = na_rpb.shape[1]
    a_w = na_heads * HEAD_DIM
    hg_heads = hg_norm_g.shape[1] // HEAD_DIM
    heads = sink_odd.shape[1]
    kv_heads = (w_qkv_odd.shape[2] // HEAD_DIM - heads) // 2

    h = jnp.concatenate([x[0], ctx[0]], axis=0)
    cc = jnp.zeros((8, d), F32).at[0].set(c[0]).at[1].set(c_ctx)
    mod_all = _modulation(cc, w_mod, b_mod)[:, :2].reshape(depth, 2, 9, d)

    cos, sin = _rope_tables(n_lat, n_ctx)
    hg_consts = _hgrn_consts()
    hg_consts = (jnp.asarray(hg_consts[0], BF16), jnp.asarray(hg_consts[1], F32))
    fg = final_norm_g.reshape(1, d)
    b_w = hg_heads * HEAD_DIM
    c_f, c_i, c_g = 3 * a_w + b_w, 3 * a_w + 3 * b_w, 3 * a_w + 4 * b_w
    w_in_even = jnp.concatenate([w_in_even[:, :, :c_f], w_in_even[:, :, c_i:c_g], w_in_even[:, :, c_f:c_i],
                                 w_in_even[:, :, c_g:]], axis=2).astype(BF16)
    w_out_even = w_out_even.astype(BF16)
    w_qkv_odd, w_o_odd = w_qkv_odd.astype(BF16), w_o_odd.astype(BF16)

    for l in range(depth):
        mod = mod_all[l]
        last = l == depth - 1
        h = _ffn(h, mod, norm_g[l, 0].reshape(1, d), w_ff_in, w_ff_out, fg, layer=l, half=0,
                 sub=0, rows=rows, tm=tm, n_lat=n_lat)
        g1 = norm_g[l, 1].reshape(1, d)
        if l % 2 == 0:
            e = l // 2
            p16, p32 = _proj(h, mod, g1, w_in_even, cos, sin, layer=e, tm=tm, tn=1024, n_lat=n_lat,
                             rope_cols=0, cols16=3 * a_w + 2 * b_w)
            ya = _na_attention(p16, _na_bias(na_rpb[e], n_lat // GRID_W), n_lat=n_lat, n_ctx=n_ctx, heads=na_heads)
            o_f, o_b = _hgrn_scan(p16, p32, hg_lb_logits, hg_consts, layer_e=e, n_lat=n_lat, n_ctx=n_ctx,
                                  heads=hg_heads, qcol=3 * a_w, icol=3 * a_w + b_w)
            yb = _hgrn_out(o_f, o_b, p32, hg_norm_g[e].reshape(1, -1), heads=hg_heads, gcol0=2 * b_w, tm=tm)
            h = _oproj(ya, yb, 0, w_out_even, h, mod, layer=e, tm=tm, n_lat=n_lat)
        else:
            o = l // 2
            p16, = _proj(h, mod, g1, w_qkv_odd, cos, sin, layer=o, tm=tm, tn=512, n_lat=n_lat,
                         rope_cols=(heads + kv_heads) * HEAD_DIM, cols16=w_qkv_odd.shape[2])
            y = _swa_attention(p16, sink_odd[o], n_lat=n_lat, n_ctx=n_ctx, heads=heads, kv_heads=kv_heads)
            h = _oproj(y, y, 1, w_o_odd, h, mod, layer=o, tm=tm, n_lat=n_lat)
        g2 = norm_g[l, 2].reshape(1, d)
        if last:
            h = _ffn(h, mod, g2, w_ff_in, w_ff_out, fg, layer=l, half=1,
                     sub=2, rows=n_lat, tm=tm, n_lat=n_lat, final_norm=True)
        else:
            h = _ffn(h, mod, g2, w_ff_in, w_ff_out, fg, layer=l, half=1,
                     sub=2, rows=rows, tm=tm, n_lat=n_lat)
    return h[None]
```

```python
import functools

import numpy as np
import jax
import jax.numpy as jnp
from jax import lax
from jax.experimental import pallas as pl
from jax.experimental.pallas import tpu as pltpu

F32 = jnp.float32
BF16 = jnp.bfloat16

GRID_W = 64
NORM_EPS = 1e-6
ROPE_THETA = 10000.0
HEAD_DIM = 128
NA_ROWS = 8
NA_COLS = 16
NA_Q_ROWS = 4
NA_K_ROWS = NA_Q_ROWS + NA_ROWS
ATT_BLOCK = NA_Q_ROWS * GRID_W
NA_HEADS_PER_STEP = 2
WINDOW = 128
HG_CHUNK = 256
ROW_CHUNK = 16
FFN_COLS = 512
FFN_TAIL_COLS = 256
QKV_SLAB = 512
FFN_OUT_COLS = 512
ROW_UNROLL = 4
HG_ROW_LEVEL = 8
MASKED = -1e30
VMEM_LIMIT = 56 * 1024 * 1024


def _params(*sem):
    return pltpu.CompilerParams(dimension_semantics=sem, vmem_limit_bytes=VMEM_LIMIT)


def _dot(a, b):
    return jnp.dot(a, b, preferred_element_type=F32)


def _dot_nt(a, b):
    return lax.dot_general(a, b, (((1,), (1,)), ((), ())), preferred_element_type=F32)


def _sigmoid(x):
    return 1.0 / (1.0 + jnp.exp(-x))


def _silu(x):
    return x * _sigmoid(x)


def _mod_kernel(c_ref, w_ref, b_ref, o_ref):
    s = _silu(c_ref[...]).astype(BF16)
    o_ref[0] = _dot(s, w_ref[0].astype(BF16)) + b_ref[0]


def _modulation(cc, w_mod, b_mod):
    depth, d, n = w_mod.shape
    tn = 1024
    return pl.pallas_call(
        _mod_kernel,
        grid=(depth, n // tn),
        in_specs=[
            pl.BlockSpec((8, d), lambda l, j: (0, 0)),
            pl.BlockSpec((1, d, tn), lambda l, j: (l, 0, j)),
            pl.BlockSpec((1, 1, tn), lambda l, j: (l, 0, j)),
        ],
        out_specs=pl.BlockSpec((1, 8, tn), lambda l, j: (l, 0, j)),
        out_shape=jax.ShapeDtypeStruct((depth, 8, n), F32),
        compiler_params=_params("arbitrary", "arbitrary"),
        name="modulation",
    )(cc, w_mod, b_mod.reshape(depth, 1, n))


def _mod_rows(mod_ref, sub, k, is_ctx):
    r = 3 * sub + k
    return jnp.where(is_ctx, mod_ref[1, r:r + 1, :], mod_ref[0, r:r + 1, :])


def _row_max(*blocks):
    acc = None
    for s in blocks:
        for c0 in range(0, s.shape[1], HEAD_DIM):
            t = s[:, c0:c0 + HEAD_DIM]
            acc = t if acc is None else jnp.maximum(acc, t)
    return jnp.max(acc, axis=-1, keepdims=True)


def _with_ones(v):
    return jnp.concatenate([v, jnp.ones_like(v)], axis=1)


def _is_ctx(tile, tm, n_lat):
    row = tile * tm + lax.broadcasted_iota(jnp.int32, (tm, 1), 0)
    return row >= n_lat


def _row_chunks(row0, tm, n_lat, body):
    def step(c, carry):
        r0 = pl.multiple_of(c * ROW_CHUNK, ROW_CHUNK)
        body(pl.ds(r0, ROW_CHUNK), (row0 + r0 >= n_lat).astype(jnp.int32))
        return carry

    lax.fori_loop(0, tm // ROW_CHUNK, step, 0, unroll=ROW_UNROLL)


def _adaln_rows(h_ref, mod_ref, g_ref, u_scr, gs_scr, *, sub, row0, tm, n_lat):
    for which in range(2):
        gs_scr[which] = g_ref[...] * (1.0 + mod_ref[which, 3 * sub + 1:3 * sub + 2, :])

    def body(rows, which):
        x = h_ref[rows, :]
        r = lax.rsqrt(jnp.mean(x * x, axis=-1, keepdims=True) + NORM_EPS)
        u_scr[rows, :] = ((x * r) * gs_scr[which] + mod_ref[which, 3 * sub:3 * sub + 1, :]).astype(BF16)

    _row_chunks(row0, tm, n_lat, body)


def _ffn_kernel(h_ref, mod_ref, g_ref, wg_ref, wu_ref, wo_ref, fg_ref, *rest,
                sub, tm, row0, n_lat, final_norm, emit16, aliased):
    rest = rest[1:] if aliased else rest
    o_ref = rest[0]
    u_scr, gs_scr = rest[-2:]
    i, j = pl.program_id(0), pl.program_id(1)
    first_row = row0 + i * tm

    @pl.when(j == 0)
    def _():
        _adaln_rows(h_ref, mod_ref, g_ref, u_scr, gs_scr, sub=sub, row0=first_row, tm=tm, n_lat=n_lat)
        o_ref[...] = jnp.zeros_like(o_ref)

    if emit16:
        wg16_ref, wu16_ref, wo16_ref = rest[1:4]
        wg16_ref[...] = wg_ref[...].astype(BF16)
        wu16_ref[...] = wu_ref[...].astype(BF16)
        wo16_ref[...] = wo_ref[...].astype(BF16)
        wg_ref, wu_ref, wo_ref = wg16_ref, wu16_ref, wo16_ref
    u = u_scr[...]
    a = (_silu(_dot(u, wg_ref[...])) * _dot(u, wu_ref[...])).astype(BF16)
    d = o_ref.shape[1]
    for c0 in range(0, d, FFN_OUT_COLS):
        o_ref[:, c0:c0 + FFN_OUT_COLS] += _dot(a, wo_ref[:, c0:c0 + FFN_OUT_COLS])

    @pl.when(j == pl.num_programs(1) - 1)
    def _():
        def body(rows, which):
            h = h_ref[rows, :] + (0.5 * mod_ref[which, 3 * sub + 2:3 * sub + 3, :]) * o_ref[rows, :]
            if final_norm:
                h = h * lax.rsqrt(jnp.mean(h * h, axis=-1, keepdims=True) + NORM_EPS) * fg_ref[...]
            o_ref[rows, :] = h

        _row_chunks(first_row, tm, n_lat, body)


def _ffn(h, mod, g, w_in, w_out, fg, *, layer, half, sub, rows, tm, n_lat, final_norm=False, h_tail=None):
    d = h.shape[1]
    f = w_out.shape[2]
    tail = rows % tm or tm
    n_main = (rows - tail) // tm
    assert (rows - tail) % tail == 0
    tail_src, tail_blk = (h, (rows - tail) // tail) if h_tail is None else (h_tail, 0)
    common = dict(sub=sub, n_lat=n_lat, final_norm=final_norm)
    small = [
        pl.BlockSpec((2, 9, d), lambda i, j: (0, 0, 0)),
        pl.BlockSpec((1, d), lambda i, j: (0, 0)),
    ]
    fg_spec = pl.BlockSpec((1, d), lambda i, j: (0, 0))

    tf, t0 = FFN_TAIL_COLS, (rows - tail) // tail
    nf = f // tf
    out, wg16, wu16, wo16 = pl.pallas_call(
        functools.partial(_ffn_kernel, tm=tail, row0=rows - tail, emit16=True, aliased=False, **common),
        grid=(1, nf),
        in_specs=[pl.BlockSpec((tail, d), lambda i, j: (tail_blk, 0))] + small + [
            pl.BlockSpec((None, None, d, tf), lambda i, j: (layer, half, 0, j)),
            pl.BlockSpec((None, None, d, tf), lambda i, j: (layer, half, 0, nf + j)),
            pl.BlockSpec((None, None, tf, d), lambda i, j: (layer, half, j, 0)),
            fg_spec,
        ],
        out_specs=[
            pl.BlockSpec((tail, d), lambda i, j: (t0, 0)),
            pl.BlockSpec((d, tf), lambda i, j: (0, j)),
            pl.BlockSpec((d, tf), lambda i, j: (0, j)),
            pl.BlockSpec((tf, d), lambda i, j: (j, 0)),
        ],
        out_shape=[
            jax.ShapeDtypeStruct((rows, d), F32),
            jax.ShapeDtypeStruct((d, f), BF16),
            jax.ShapeDtypeStruct((d, f), BF16),
            jax.ShapeDtypeStruct((f, d), BF16),
        ],
        scratch_shapes=[pltpu.VMEM((tail, d), BF16), pltpu.VMEM((2, 1, d), F32)],
        compiler_params=_params("arbitrary", "arbitrary"),
        name="ffn_tail",
    )(tail_src, mod, g, w_in, w_in, w_out, fg)

    tf = FFN_COLS
    return pl.pallas_call(
        functools.partial(_ffn_kernel, tm=tm, row0=0, emit16=False, aliased=True, **common),
        grid=(n_main, f // tf),
        in_specs=[pl.BlockSpec((tm, d), lambda i, j: (i, 0))] + small + [
            pl.BlockSpec((d, tf), lambda i, j: (0, j)),
            pl.BlockSpec((d, tf), lambda i, j: (0, j)),
            pl.BlockSpec((tf, d), lambda i, j: (j, 0)),
            fg_spec,
            pl.BlockSpec(memory_space=pl.ANY),
        ],
        out_specs=pl.BlockSpec((tm, d), lambda i, j: (i, 0)),
        out_shape=jax.ShapeDtypeStruct((rows, d), F32),
        input_output_aliases={7: 0},
        scratch_shapes=[pltpu.VMEM((tm, d), BF16), pltpu.VMEM((2, 1, d), F32)],
        compiler_params=_params("arbitrary", "arbitrary"),
        name="ffn",
    )(h, mod, g, wg16, wu16, wo16, fg, out)


def _rope(x, cos, sin):
    lane = lax.broadcasted_iota(jnp.int32, x.shape, 1)
    first = (lane % 64) < 32
    swapped = jnp.where(first, pltpu.roll(x, 96, 1), pltpu.roll(x, 32, 1))
    return x * cos + swapped * sin


def _proj_kernel(h_ref, mod_ref, g_ref, w_ref, o16_ref, o32_ref, u_scr, gs_scr, *, tm, n_lat, n16):
    i, j = pl.program_id(0), pl.program_id(1)

    @pl.when(j == 0)
    def _():
        _adaln_rows(h_ref, mod_ref, g_ref, u_scr, gs_scr, sub=1, row0=i * tm, tm=tm, n_lat=n_lat)

    y = _dot(u_scr[...], w_ref[...])

    @pl.when(j < n16)
    def _():
        o16_ref[...] = y.astype(BF16)

    @pl.when(j >= n16)
    def _():
        o32_ref[...] = y


def _proj(h, mod, g, w, *, layer, tm, tn, n_lat, cols16):
    rows, d = h.shape
    n = w.shape[2]
    n16 = cols16 // tn
    return pl.pallas_call(
        functools.partial(_proj_kernel, tm=tm, n_lat=n_lat, n16=n16),
        grid=(rows // tm, n // tn),
        in_specs=[
            pl.BlockSpec((tm, d), lambda i, j: (i, 0)),
            pl.BlockSpec((2, 9, d), lambda i, j: (0, 0, 0)),
            pl.BlockSpec((1, d), lambda i, j: (0, 0)),
            pl.BlockSpec((None, d, tn), lambda i, j: (layer, 0, j)),
        ],
        out_specs=[
            pl.BlockSpec((tm, tn), lambda i, j: (i, jnp.minimum(j, n16 - 1))),
            pl.BlockSpec((tm, tn), lambda i, j: (i, jnp.maximum(j - n16, 0))),
        ],
        out_shape=[jax.ShapeDtypeStruct((rows, cols16), BF16), jax.ShapeDtypeStruct((rows, n - cols16), F32)],
        scratch_shapes=[pltpu.VMEM((tm, d), BF16), pltpu.VMEM((2, 1, d), F32)],
        compiler_params=_params("arbitrary", "arbitrary"),
        name="proj",
    )(h, mod, g, w)


def _qkv_kernel(h_ref, mod_ref, g_ref, w_ref, cos_ref, sin_ref, o_ref, u_scr, gs_scr, *, tm, n_lat, rope_cols):
    _adaln_rows(h_ref, mod_ref, g_ref, u_scr, gs_scr, sub=1, row0=pl.program_id(0) * tm, tm=tm, n_lat=n_lat)
    u = u_scr[...]
    cos, sin = cos_ref[...], sin_ref[...]
    for c0 in range(0, o_ref.shape[1], QKV_SLAB):
        y = _dot(u, w_ref[:, c0:c0 + QKV_SLAB])
        for hd in range(0, QKV_SLAB, HEAD_DIM):
            x = y[:, hd:hd + HEAD_DIM]
            if c0 + hd < rope_cols:
                x = _rope(x, cos, sin)
            o_ref[:, c0 + hd:c0 + hd + HEAD_DIM] = x.astype(BF16)


def _qkv_proj(h, mod, g, w, cos, sin, *, layer, tm, n_lat, rope_cols):
    rows, d = h.shape
    n = w.shape[2]
    return pl.pallas_call(
        functools.partial(_qkv_kernel, tm=tm, n_lat=n_lat, rope_cols=rope_cols),
        grid=(rows // tm,),
        in_specs=[
            pl.BlockSpec((tm, d), lambda i: (i, 0)),
            pl.BlockSpec((2, 9, d), lambda i: (0, 0, 0)),
            pl.BlockSpec((1, d), lambda i: (0, 0)),
            pl.BlockSpec((None, d, n), lambda i: (layer, 0, 0), pipeline_mode=pl.Buffered(1)),
            pl.BlockSpec((tm, HEAD_DIM), lambda i: (i, 0)),
            pl.BlockSpec((tm, HEAD_DIM), lambda i: (i, 0)),
        ],
        out_specs=pl.BlockSpec((tm, n), lambda i: (i, 0)),
        out_shape=jax.ShapeDtypeStruct((rows, n), BF16),
        scratch_shapes=[pltpu.VMEM((tm, d), BF16), pltpu.VMEM((2, 1, d), F32)],
        compiler_params=_params("arbitrary"),
        name="qkv_proj",
    )(h, mod, g, w, cos, sin)


def _rope_tables(n_lat, n_ctx):
    t = np.arange(n_lat)
    inv = (ROPE_THETA ** (-np.arange(0, 64, 2, dtype=np.float32) / 64)).astype(np.float32)
    ang_r = (t // GRID_W).astype(np.float32)[:, None] * inv[None, :]
    ang_c = (t % GRID_W).astype(np.float32)[:, None] * inv[None, :]
    cr, sr, cc, sc = np.cos(ang_r), np.sin(ang_r), np.cos(ang_c), np.sin(ang_c)
    cos = np.concatenate([cr, cr, cc, cc], axis=1)
    sin = np.concatenate([-sr, sr, -sc, sc], axis=1)
    cos = np.concatenate([cos, np.ones((n_ctx, HEAD_DIM), np.float32)], axis=0)
    sin = np.concatenate([sin, np.zeros((n_ctx, HEAD_DIM), np.float32)], axis=0)
    return jnp.asarray(cos, F32), jnp.asarray(sin, F32)


def _oproj_kernel(a1_ref, a2_ref, w_ref, h_ref, mod_ref, o_ref, *, tm, n_lat, k1):
    is_ctx = _is_ctx(pl.program_id(0), tm, n_lat)
    y = _dot(a1_ref[...], w_ref[:k1, :]) + _dot(a2_ref[...], w_ref[k1:, :])
    o_ref[...] = h_ref[...] + _mod_rows(mod_ref, 1, 2, is_ctx) * y


def _oproj(a1, a2, col2, w, h, mod, *, layer, tm, n_lat):
    rows, d = h.shape
    k1 = w.shape[1] // 2
    return pl.pallas_call(
        functools.partial(_oproj_kernel, tm=tm, n_lat=n_lat, k1=k1),
        grid=(rows // tm,),
        in_specs=[
            pl.BlockSpec((tm, k1), lambda i: (i, 0)),
            pl.BlockSpec((tm, k1), lambda i: (i, col2)),
            pl.BlockSpec((None,) + w.shape[1:], lambda i: (layer, 0, 0)),
            pl.BlockSpec((tm, d), lambda i: (i, 0)),
            pl.BlockSpec((2, 9, d), lambda i: (0, 0, 0)),
        ],
        out_specs=pl.BlockSpec((tm, d), lambda i: (i, 0)),
        out_shape=jax.ShapeDtypeStruct((rows, d), F32),
        compiler_params=_params("arbitrary"),
        name="oproj",
    )(a1, a2, w, h, mod)


def _na_bias(rpb, n_rows):
    nh, n_a, n_b = rpb.shape
    nb = n_rows // NA_Q_ROWS
    w = jnp.full((nh, n_a, 128), MASKED, F32)
    w = w.at[..., :NA_COLS].set(rpb[..., NA_COLS - 1:]).at[..., 128 - (NA_COLS - 1):].set(rpb[..., :NA_COLS - 1])
    toep = jnp.tile(w, (1, 1, GRID_W))[..., :GRID_W * 127].reshape(nh, n_a, GRID_W, 127)[..., :GRID_W]
    toep = jnp.pad(toep.transpose(0, 2, 1, 3), ((0, 0), (0, 0), (NA_K_ROWS, NA_K_ROWS), (0, 0)),
                   constant_values=MASKED)

    q = np.arange(ATT_BLOCK)
    k = np.arange(NA_K_ROWS * GRID_W)
    dr, c = q // GRID_W, q % GRID_W
    kr, kc = k // GRID_W, k % GRID_W
    c0 = np.clip(c - NA_COLS // 2, 0, GRID_W - NA_COLS)
    col_ok = (kc[None, :] >= c0[:, None]) & (kc[None, :] < c0[:, None] + NA_COLS)
    tables, ok = [], []
    for b in (0, 1, nb - 1):
        r = b * NA_Q_ROWS + dr
        r0 = np.clip(r - NA_ROWS // 2, 0, n_rows - NA_ROWS)
        base = int(np.clip(b * NA_Q_ROWS - NA_ROWS // 2, 0, n_rows - NA_K_ROWS))
        kabs = base + kr
        ok.append((kabs[None, :] >= r0[:, None]) & (kabs[None, :] < r0[:, None] + NA_ROWS) & col_ok)
        per_row = []
        for j in range(NA_Q_ROWS):
            lo = base - (b * NA_Q_ROWS + j) + NA_ROWS - 1 + NA_K_ROWS
            per_row.append(toep[:, :, lo:lo + NA_K_ROWS].reshape(nh, GRID_W, NA_K_ROWS * GRID_W))
        tables.append(jnp.stack(per_row, axis=1).reshape(nh, ATT_BLOCK, NA_K_ROWS * GRID_W))
    return jnp.where(np.stack(ok)[:, None], jnp.stack(tables), MASKED)


def _na_kernel(q_ref, k_ref, v_ref, bias_ref, o_ref, *, n_lat, n_ctx, nb):
    b = pl.program_id(1)
    scale = HEAD_DIM ** -0.5
    nk = NA_K_ROWS * GRID_W
    start = pl.multiple_of(jnp.clip(b - 1, 0, nb - 3) * ATT_BLOCK, ATT_BLOCK)

    def one_head(hh, local):
        cols = slice(hh * HEAD_DIM, (hh + 1) * HEAD_DIM)
        q = q_ref[:, cols]
        s_ctx = _dot_nt(q, k_ref[n_lat:n_lat + n_ctx, cols]) * scale
        vc1 = _with_ones(v_ref[n_lat:n_lat + n_ctx, cols])
        if local:
            kl = k_ref[pl.ds(start, nk), cols]
            vl1 = _with_ones(v_ref[pl.ds(start, nk), cols])
            s_loc = _dot_nt(q, kl) * scale + bias_ref[0, hh]
            m = _row_max(s_loc, s_ctx)
            acc = _dot(jnp.exp(s_loc - m).astype(BF16), vl1) + _dot(jnp.exp(s_ctx - m).astype(BF16), vc1)
        else:
            acc = _dot(jnp.exp(s_ctx - _row_max(s_ctx)).astype(BF16), vc1)
        o_ref[:, cols] = (acc[:, :HEAD_DIM] / acc[:, HEAD_DIM:]).astype(o_ref.dtype)

    @pl.when(b < nb)
    def _():
        for hh in range(NA_HEADS_PER_STEP):
            one_head(hh, True)

    @pl.when(b >= nb)
    def _():
        for hh in range(NA_HEADS_PER_STEP):
            one_head(hh, False)


def _na_attention(p, bias, *, layer, n_lat, n_ctx, heads):
    rows = n_lat + n_ctx
    nb = n_lat // ATT_BLOCK
    nblk = rows // ATT_BLOCK
    hps = NA_HEADS_PER_STEP
    width = hps * HEAD_DIM
    groups = heads // hps

    def pattern(b):
        return jnp.where(b == 0, 0, jnp.where(b == nb - 1, 2, 1))

    return pl.pallas_call(
        functools.partial(_na_kernel, n_lat=n_lat, n_ctx=n_ctx, nb=nb),
        grid=(groups, nblk),
        in_specs=[
            pl.BlockSpec((ATT_BLOCK, width), lambda h, b: (b, h)),
            pl.BlockSpec((rows, width), lambda h, b: (0, groups + h)),
            pl.BlockSpec((rows, width), lambda h, b: (0, 2 * groups + h)),
            pl.BlockSpec((1, hps, ATT_BLOCK, NA_K_ROWS * GRID_W),
                         lambda h, b: (pattern(b), layer * groups + h, 0, 0)),
        ],
        out_specs=pl.BlockSpec((ATT_BLOCK, width), lambda h, b: (b, h)),
        out_shape=jax.ShapeDtypeStruct((rows, heads * HEAD_DIM), BF16),
        compiler_params=_params("arbitrary", "arbitrary"),
        name="na_attention",
    )(p, p, p, bias)


def _hgrn_levels():
    levels, m = [], 1
    while m < HG_CHUNK:
        levels.append(m)
        m *= 2
    return levels


def _hgrn_consts():
    c = HG_CHUNK
    levels = _hgrn_levels()
    fine = [m for m in levels if m < HG_ROW_LEVEL]
    idx = np.arange(c)
    t, j = idx[:, None], idx[None, :]
    n = np.zeros((2, 1 + len(fine), c, c), np.float32)
    msk = np.zeros((2, 1 + len(levels), c, c), np.float32)
    n[0, 0], n[1, 0] = j <= t, j >= t
    msk[0, 0] = msk[1, 0] = np.eye(c)
    for li, m in enumerate(levels):
        seg = idx // (2 * m)
        right = ((idx % (2 * m)) >= m)[:, None]
        last_left = (seg * 2 * m + m - 1)[:, None]
        first_right = last_left + 1
        if m < HG_ROW_LEVEL:
            n[0, 1 + li] = np.where(right, (j > last_left) & (j <= t), (j > t) & (j <= last_left))
            n[1, 1 + li] = np.where(right, (j >= first_right) & (j < t), (j >= t) & (j < first_right))
        same = seg[:, None] == seg[None, :]
        msk[0, 1 + li] = same & right & ~right.T
        msk[1, 1 + li] = same & ~right & right.T
    return n.reshape(2, -1, c), msk


def _hgrn_chunk(d, q, fx, v, lb, n_ref, msk_ref, s_scr, b_scr):
    c = HG_CHUNK
    t = jnp.exp(-jnp.abs(fx))
    r = 1.0 / (1.0 + t)
    log_sig = jnp.minimum(fx, 0.0) + jnp.log(r)
    sig_neg = jnp.where(fx >= 0.0, t * r, r)
    if lb is None:
        log_f, kk = log_sig, sig_neg
    else:
        la = jnp.log(lb)
        lc = jnp.log1p(-lb) + log_sig
        log_f = jnp.maximum(la, lc) + jnp.log(1.0 + jnp.exp(-jnp.abs(la - lc)))
        kk = (1.0 - lb) * sig_neg
    qf = q.astype(F32)

    hi = log_f.astype(BF16)
    mid = (log_f - hi.astype(F32)).astype(BF16)
    dd = _dot(n_ref[d], jnp.concatenate([hi, mid], axis=1))
    dd = dd[:, :HEAD_DIM] + dd[:, HEAD_DIM:]
    b = dd[:c]
    b_scr[d] = b

    attn = _dot_nt(q, kk.astype(BF16)) * msk_ref[d, 0]
    for li, m in enumerate(_hgrn_levels()):
        if m < HG_ROW_LEVEL:
            ex = dd[(1 + li) * c:(2 + li) * c]
        else:
            row = m - 1 + d
            ref = jnp.concatenate(
                [jnp.broadcast_to(b_scr[d, s + row:s + row + 1, :], (2 * m, HEAD_DIM)) for s in range(0, c, 2 * m)],
                axis=0)
            ex = -jnp.abs(b - ref)
        e = jnp.exp(ex)
        attn += _dot_nt((qf * e).astype(BF16), (kk * e).astype(BF16)) * msk_ref[d, 1 + li]

    total = b[c - 1:c] if d == 0 else b[0:1]
    state_t = s_scr[d]
    o = _dot_nt((qf * jnp.exp(b)).astype(BF16), state_t.astype(BF16)) + _dot(attn.astype(BF16), v)
    v_t = v.astype(F32).T.astype(BF16)
    s_scr[d] = jnp.exp(total) * state_t + _dot(v_t, (kk * jnp.exp(total - b)).astype(BF16))
    return o


def _hgrn_kernel(qf_ref, qb_ref, ff_ref, fb_ref, if_ref, ib_ref, lbl_ref, n_ref, msk_ref,
                 of_ref, ob_ref, s_scr, b_scr, *, layer_e):
    @pl.when(pl.program_id(1) == 0)
    def _():
        s_scr[...] = jnp.zeros_like(s_scr)

    for d, (q_ref, f_ref, i_ref, o_ref) in enumerate(((qf_ref, ff_ref, if_ref, of_ref),
                                                      (qb_ref, fb_ref, ib_ref, ob_ref))):
        lb = None
        if layer_e > 0:
            lg = lbl_ref[d]
            ex = jnp.exp(lg - jnp.max(lg, axis=0, keepdims=True))
            num = ex[1:2]
            for e in range(2, layer_e + 1):
                num = num + ex[e:e + 1]
            lb = num / jnp.sum(ex, axis=0, keepdims=True)
        o_ref[...] = _hgrn_chunk(d, q_ref[...], f_ref[...], i_ref[...], lb, n_ref, msk_ref, s_scr, b_scr)


def _hgrn_scan(p16, p32, lb_logits, consts, *, layer_e, n_lat, n_ctx, heads, qcol, icol):
    rows = n_lat + n_ctx
    n_mat, msk = consts
    lat_chunks = n_lat // HG_CHUNK
    qc, ic = qcol // HEAD_DIM, icol // HEAD_DIM

    def fwd(s):
        return jnp.where(s == 0, lat_chunks, s - 1)

    def bwd(s):
        return lat_chunks - s

    def spec(blk, col):
        return pl.BlockSpec((HG_CHUNK, HEAD_DIM), lambda h, s: (blk(s), col + h))

    out = jax.ShapeDtypeStruct((rows, heads * HEAD_DIM), F32)
    return pl.pallas_call(
        functools.partial(_hgrn_kernel, layer_e=layer_e),
        grid=(heads, 1 + lat_chunks),
        in_specs=[
            spec(fwd, qc), spec(bwd, qc),
            spec(fwd, 0), spec(bwd, heads),
            spec(fwd, ic), spec(bwd, ic),
            pl.BlockSpec((2, lb_logits.shape[1], HEAD_DIM), lambda h, s: (0, 0, h)),
            pl.BlockSpec(n_mat.shape, lambda h, s: (0, 0, 0)),
            pl.BlockSpec(msk.shape, lambda h, s: (0, 0, 0, 0)),
        ],
        out_specs=[spec(fwd, 0), spec(bwd, 0)],
        out_shape=[out, out],
        scratch_shapes=[pltpu.VMEM((2, HEAD_DIM, HEAD_DIM), F32), pltpu.VMEM((2, HG_CHUNK, HEAD_DIM), F32)],
        compiler_params=_params("arbitrary", "arbitrary"),
        name="hgrn_scan",
    )(p16, p16, p32, p32, p16, p16, lb_logits, n_mat, msk)


def _hgrn_out_kernel(of_ref, ob_ref, g_ref, ng_ref, y_ref):
    o = of_ref[...] + ob_ref[...]
    o = o * lax.rsqrt(jnp.mean(o * o, axis=-1, keepdims=True) + NORM_EPS)
    y_ref[...] = (o * ng_ref[...] * _silu(g_ref[...])).astype(y_ref.dtype)


def _hgrn_out(o_f, o_b, p, norm_g, *, heads, gcol0, tm):
    rows = o_f.shape[0]
    return pl.pallas_call(
        _hgrn_out_kernel,
        grid=(rows // tm, heads),
        in_specs=[
            pl.BlockSpec((tm, HEAD_DIM), lambda i, h: (i, h)),
            pl.BlockSpec((tm, HEAD_DIM), lambda i, h: (i, h)),
            pl.BlockSpec((tm, HEAD_DIM), lambda i, h: (i, gcol0 // HEAD_DIM + h)),
            pl.BlockSpec((1, HEAD_DIM), lambda i, h: (0, h)),
        ],
        out_specs=pl.BlockSpec((tm, HEAD_DIM), lambda i, h: (i, h)),
        out_shape=jax.ShapeDtypeStruct((rows, heads * HEAD_DIM), BF16),
        compiler_params=_params("arbitrary", "arbitrary"),
        name="hgrn_out",
    )(o_f, o_b, p, norm_g)


def _swa_kernel(q_ref, k_ref, v_ref, sink_ref, o_ref, *, n_lat, n_ctx, nb, group, span):
    b = pl.program_id(1)
    scale = HEAD_DIM ** -0.5
    kc = k_ref[n_lat:n_lat + n_ctx, :]
    vc1 = _with_ones(v_ref[n_lat:n_lat + n_ctx, :])

    def finish(g, s_ctx, s_loc, vl1):
        sink = sink_ref[0, g:g + 1, 0:1]
        blocks = (s_ctx,) if s_loc is None else (s_ctx, s_loc)
        m = jnp.maximum(_row_max(*blocks), sink)
        acc = _dot(jnp.exp(s_ctx - m).astype(BF16), vc1)
        if s_loc is not None:
            acc = acc + _dot(jnp.exp(s_loc - m).astype(BF16), vl1)
        den = acc[:, HEAD_DIM:] + jnp.exp(sink - m)
        o_ref[:, g * HEAD_DIM:(g + 1) * HEAD_DIM] = (acc[:, :HEAD_DIM] / den).astype(o_ref.dtype)

    @pl.when(b < nb)
    def _():
        start = pl.multiple_of(jnp.clip(b * ATT_BLOCK - WINDOW, 0, n_lat - span), WINDOW)
        kl = k_ref[pl.ds(start, span), :]
        vl = _with_ones(v_ref[pl.ds(start, span), :])
        qpos = b * ATT_BLOCK + lax.broadcasted_iota(jnp.int32, (ATT_BLOCK, span), 0)
        kpos = start + lax.broadcasted_iota(jnp.int32, (ATT_BLOCK, span), 1)
        valid = jnp.abs(kpos - qpos) <= WINDOW
        for g in range(group):
            q = q_ref[:, g * HEAD_DIM:(g + 1) * HEAD_DIM]
            s_loc = jnp.where(valid, _dot_nt(q, kl) * scale, MASKED)
            finish(g, _dot_nt(q, kc) * scale, s_loc, vl)

    @pl.when(b >= nb)
    def _():
        for g in range(group):
            q = q_ref[:, g * HEAD_DIM:(g + 1) * HEAD_DIM]
            finish(g, _dot_nt(q, kc) * scale, None, None)


def _swa_attention(p, sink, *, n_lat, n_ctx, heads, kv_heads):
    rows = n_lat + n_ctx
    group = heads // kv_heads
    nb = n_lat // ATT_BLOCK
    span = ATT_BLOCK + 2 * WINDOW
    sink_tab = jnp.broadcast_to(sink.astype(F32).reshape(kv_heads, group, 1), (kv_heads, group, HEAD_DIM))
    return pl.pallas_call(
        functools.partial(_swa_kernel, n_lat=n_lat, n_ctx=n_ctx, nb=nb, group=group, span=span),
        grid=(kv_heads, rows // ATT_BLOCK),
        in_specs=[
            pl.BlockSpec((ATT_BLOCK, group * HEAD_DIM), lambda k, b: (b, k)),
            pl.BlockSpec((rows, HEAD_DIM), lambda k, b: (0, heads + k)),
            pl.BlockSpec((rows, HEAD_DIM), lambda k, b: (0, heads + kv_heads + k)),
            pl.BlockSpec((1, group, HEAD_DIM), lambda k, b: (k, 0, 0)),
        ],
        out_specs=pl.BlockSpec((ATT_BLOCK, group * HEAD_DIM), lambda k, b: (b, k)),
        out_shape=jax.ShapeDtypeStruct((rows, heads * HEAD_DIM), BF16),
        compiler_params=_params("arbitrary", "arbitrary"),
        name="swa_attention",
    )(p, p, p, sink_tab)


def kernel(x, c, ctx, c_ctx, w_mod, b_mod, norm_g, w_ff_in, w_ff_out, w_in_even, w_out_even,
           na_rpb, hg_lb_logits, hg_norm_g, w_qkv_odd, w_o_odd, sink_odd, final_norm_g):
    assert x.shape[0] == 1
    depth, d = w_mod.shape[0], x.shape[2]
    n_lat, n_ctx = x.shape[1], ctx.shape[1]
    rows = n_lat + n_ctx
    tm = 768
    assert rows % tm == 0 and n_lat % 1024 == 0 and n_ctx == ATT_BLOCK
    na_heads = na_rpb.shape[1]
    a_w = na_heads * HEAD_DIM
    hg_heads = hg_norm_g.shape[1] // HEAD_DIM
    heads = sink_odd.shape[1]
    kv_heads = (w_qkv_odd.shape[2] // HEAD_DIM - heads) // 2

    cc =jnp.zeros((8, d), F32).at[0].set(c[0]).at[1].set(c_ctx)
    mod_all = _modulation(cc, w_mod, b_mod)[:, :2].reshape(depth, 2, 9, d)

    cos, sin = _rope_tables(n_lat, n_ctx)
    na_bias = _na_bias(na_rpb.reshape((-1,) + na_rpb.shape[2:]), n_lat // GRID_W)
    hg_consts = _hgrn_consts()
    hg_consts = (jnp.asarray(hg_consts[0], BF16), jnp.asarray(hg_consts[1], F32))
    fg = final_norm_g.reshape(1, d)
    b_w = hg_heads * HEAD_DIM
    c_f, c_i, c_g = 3 * a_w + b_w, 3 * a_w + 3 * b_w, 3 * a_w + 4 * b_w
    w_in_even = jnp.concatenate([w_in_even[:, :, :c_f], w_in_even[:, :, c_i:c_g], w_in_even[:, :, c_f:c_i],
                                 w_in_even[:, :, c_g:]], axis=2).astype(BF16)
    w_out_even = w_out_even.astype(BF16)
    w_qkv_odd, w_o_odd = w_qkv_odd.astype(BF16), w_o_odd.astype(BF16)

    h = x[0]
    h_tail = jnp.concatenate([x[0, rows - tm:], ctx[0]], axis=0)
    for l in range(depth):
        mod = mod_all[l]
        last = l == depth - 1
        h = _ffn(h, mod, norm_g[l, 0].reshape(1, d), w_ff_in, w_ff_out, fg, layer=l, half=0,
                 sub=0, rows=rows, tm=tm, n_lat=n_lat, h_tail=h_tail if l == 0 else None)
        g1 = norm_g[l, 1].reshape(1, d)
        if l % 2 == 0:
            e = l // 2
            p16, p32 = _proj(h, mod, g1, w_in_even, layer=e, tm=tm, tn=1024, n_lat=n_lat, cols16=3 * a_w + 2 * b_w)
            ya = _na_attention(p16, na_bias, layer=e, n_lat=n_lat, n_ctx=n_ctx, heads=na_heads)
            o_f, o_b = _hgrn_scan(p16, p32, hg_lb_logits, hg_consts, layer_e=e, n_lat=n_lat, n_ctx=n_ctx,
                                  heads=hg_heads, qcol=3 * a_w, icol=3 * a_w + b_w)
            yb = _hgrn_out(o_f, o_b, p32, hg_norm_g[e].reshape(1, -1), heads=hg_heads, gcol0=2 * b_w, tm=tm)
            h = _oproj(ya, yb, 0, w_out_even, h, mod, layer=e, tm=tm, n_lat=n_lat)
        else:
            o = l // 2
            p16 = _qkv_proj(h, mod, g1, w_qkv_odd, cos, sin, layer=o, tm=tm, n_lat=n_lat,
                            rope_cols=(heads + kv_heads) * HEAD_DIM)
            y = _swa_attention(p16, sink_odd[o], n_lat=n_lat, n_ctx=n_ctx, heads=heads, kv_heads=kv_heads)
            h = _oproj(y, y, 1, w_o_odd, h, mod, layer=o, tm=tm, n_lat=n_lat)
        g2 = norm_g[l, 2].reshape(1, d)
        if last:
            h = _ffn(h, mod, g2, w_ff_in, w_ff_out, fg, layer=l, half=1,
                     sub=2, rows=n_lat, tm=tm, n_lat=n_lat, final_norm=True)
        else:
            h = _ffn(h, mod, g2, w_ff_in, w_ff_out, fg, layer=l, half=1,
                     sub=2, rows=rows, tm=tm, n_lat=n_lat)
    return h[None]
```

```python
import functools

import numpy as np
import jax
import jax.numpy as jnp
from jax import lax
from jax.experimental import pallas as pl
from jax.experimental.pallas import tpu as pltpu

F32 = jnp.float32
BF16 = jnp.bfloat16

GRID_W = 64
NORM_EPS = 1e-6
ROPE_THETA = 10000.0
HEAD_DIM = 128
NA_ROWS = 8
NA_COLS = 16
NA_Q_ROWS = 4
NA_K_ROWS = NA_Q_ROWS + NA_ROWS
ATT_BLOCK = NA_Q_ROWS * GRID_W
NA_HEADS_PER_STEP = 2
WINDOW = 128
HG_STEP = 256
HG_CHUNK = 128
HG_HEADS_PER_STEP = 2
ROW_CHUNK = 16
FFN_COLS = 512
FFN_TAIL_COLS = 256
QKV_SLAB = 512
FFN_OUT_COLS = 512
ROW_UNROLL = 4
HG_ROW_LEVEL = 8
MASKED = -1e30
VMEM_LIMIT = 56 * 1024 * 1024


def _params(*sem):
    return pltpu.CompilerParams(dimension_semantics=sem, vmem_limit_bytes=VMEM_LIMIT)


def _dot(a, b):
    return jnp.dot(a, b, preferred_element_type=F32)


def _dot_nt(a, b):
    return lax.dot_general(a, b, (((1,), (1,)), ((), ())), preferred_element_type=F32)


def _sigmoid(x):
    return 1.0 / (1.0 + jnp.exp(-x))


def _silu(x):
    return x * _sigmoid(x)


def _mod_kernel(c_ref, w_ref, b_ref, o_ref):
    s = _silu(c_ref[...]).astype(BF16)
    o_ref[0] = _dot(s, w_ref[0].astype(BF16)) + b_ref[0]


def _modulation(cc, w_mod, b_mod):
    depth, d, n = w_mod.shape
    tn = 1024
    return pl.pallas_call(
        _mod_kernel,
        grid=(depth, n // tn),
        in_specs=[
            pl.BlockSpec((8, d), lambda l, j: (0, 0)),
            pl.BlockSpec((1, d, tn), lambda l, j: (l, 0, j)),
            pl.BlockSpec((1, 1, tn), lambda l, j: (l, 0, j)),
        ],
        out_specs=pl.BlockSpec((1, 8, tn), lambda l, j: (l, 0, j)),
        out_shape=jax.ShapeDtypeStruct((depth, 8, n), F32),
        compiler_params=_params("arbitrary", "arbitrary"),
        name="modulation",
    )(cc, w_mod, b_mod.reshape(depth, 1, n))


def _mod_rows(mod_ref, sub, k, is_ctx):
    r = 3 * sub + k
    return jnp.where(is_ctx, mod_ref[1, r:r + 1, :], mod_ref[0, r:r + 1, :])


def _row_max(*blocks):
    acc = None
    for s in blocks:
        for c0 in range(0, s.shape[1], HEAD_DIM):
            t = s[:, c0:c0 + HEAD_DIM]
            acc = t if acc is None else jnp.maximum(acc, t)
    return jnp.max(acc, axis=-1, keepdims=True)


def _with_ones(v):
    return jnp.concatenate([v, jnp.ones_like(v)], axis=1)


def _is_ctx(tile, tm, n_lat):
    row = tile * tm + lax.broadcasted_iota(jnp.int32, (tm, 1), 0)
    return row >= n_lat


def _row_chunks(row0, tm, n_lat, body):
    def step(c, carry):
        r0 = pl.multiple_of(c * ROW_CHUNK, ROW_CHUNK)
        body(pl.ds(r0, ROW_CHUNK), (row0 + r0 >= n_lat).astype(jnp.int32))
        return carry

    lax.fori_loop(0, tm // ROW_CHUNK, step, 0, unroll=ROW_UNROLL)


def _adaln_rows(h_ref, mod_ref, g_ref, u_scr, gs_scr, *, sub, row0, tm, n_lat):
    for which in range(2):
        gs_scr[which] = g_ref[...] * (1.0 + mod_ref[which, 3 * sub + 1:3 * sub + 2, :])

    def body(rows, which):
        x = h_ref[rows, :]
        r = lax.rsqrt(jnp.mean(x * x, axis=-1, keepdims=True) + NORM_EPS)
        u_scr[rows, :] = ((x * r) * gs_scr[which] + mod_ref[which, 3 * sub:3 * sub + 1, :]).astype(BF16)

    _row_chunks(row0, tm, n_lat, body)


def _ffn_kernel(h_ref, mod_ref, g_ref, wg_ref, wu_ref, wo_ref, fg_ref, *rest,
                sub, tm, row0, n_lat, final_norm, emit16, aliased):
    rest = rest[1:] if aliased else rest
    o_ref = rest[0]
    u_scr, gs_scr = rest[-2:]
    i, j = pl.program_id(0), pl.program_id(1)
    first_row = row0 + i * tm

    @pl.when(j == 0)
    def _():
        _adaln_rows(h_ref, mod_ref, g_ref, u_scr, gs_scr, sub=sub, row0=first_row, tm=tm, n_lat=n_lat)
        o_ref[...] = jnp.zeros_like(o_ref)

    if emit16:
        wg16_ref, wu16_ref, wo16_ref = rest[1:4]
        wg16_ref[...] = wg_ref[...].astype(BF16)
        wu16_ref[...] = wu_ref[...].astype(BF16)
        wo16_ref[...] = wo_ref[...].astype(BF16)
        wg_ref, wu_ref, wo_ref = wg16_ref, wu16_ref, wo16_ref
    u = u_scr[...]
    a = (_silu(_dot(u, wg_ref[...])) * _dot(u, wu_ref[...])).astype(BF16)
    d = o_ref.shape[1]
    for c0 in range(0, d, FFN_OUT_COLS):
        o_ref[:, c0:c0 + FFN_OUT_COLS] += _dot(a, wo_ref[:, c0:c0 + FFN_OUT_COLS])

    @pl.when(j == pl.num_programs(1) - 1)
    def _():
        def body(rows, which):
            h = h_ref[rows, :] + (0.5 * mod_ref[which, 3 * sub + 2:3 * sub + 3, :]) * o_ref[rows, :]
            if final_norm:
                h = h * lax.rsqrt(jnp.mean(h * h, axis=-1, keepdims=True) + NORM_EPS) * fg_ref[...]
            o_ref[rows, :] = h

        _row_chunks(first_row, tm, n_lat, body)


def _ffn(h, mod, g, w_in, w_out, fg, *, layer, half, sub, rows, tm, n_lat, final_norm=False, h_tail=None):
    d = h.shape[1]
    f = w_out.shape[2]
    tail = rows % tm or tm
    n_main = (rows - tail) // tm
    assert (rows - tail) % tail == 0
    tail_src, tail_blk = (h, (rows - tail) // tail) if h_tail is None else (h_tail, 0)
    common = dict(sub=sub, n_lat=n_lat, final_norm=final_norm)
    small = [
        pl.BlockSpec((2, 9, d), lambda i, j: (0, 0, 0)),
        pl.BlockSpec((1, d), lambda i, j: (0, 0)),
    ]
    fg_spec = pl.BlockSpec((1, d), lambda i, j: (0, 0))

    tf, t0 = FFN_TAIL_COLS, (rows - tail) // tail
    nf = f // tf
    out, wg16, wu16, wo16 = pl.pallas_call(
        functools.partial(_ffn_kernel, tm=tail, row0=rows - tail, emit16=True, aliased=False, **common),
        grid=(1, nf),
        in_specs=[pl.BlockSpec((tail, d), lambda i, j: (tail_blk, 0))] + small + [
            pl.BlockSpec((None, None, d, tf), lambda i, j: (layer, half, 0, j)),
            pl.BlockSpec((None, None, d, tf), lambda i, j: (layer, half, 0, nf + j)),
            pl.BlockSpec((None, None, tf, d), lambda i, j: (layer, half, j, 0)),
            fg_spec,
        ],
        out_specs=[
            pl.BlockSpec((tail, d), lambda i, j: (t0, 0)),
            pl.BlockSpec((d, tf), lambda i, j: (0, j)),
            pl.BlockSpec((d, tf), lambda i, j: (0, j)),
            pl.BlockSpec((tf, d), lambda i, j: (j, 0)),
        ],
        out_shape=[
            jax.ShapeDtypeStruct((rows, d), F32),
            jax.ShapeDtypeStruct((d, f), BF16),
            jax.ShapeDtypeStruct((d, f), BF16),
            jax.ShapeDtypeStruct((f, d), BF16),
        ],
        scratch_shapes=[pltpu.VMEM((tail, d), BF16), pltpu.VMEM((2, 1, d), F32)],
        compiler_params=_params("arbitrary", "arbitrary"),
        name="ffn_tail",
    )(tail_src, mod, g, w_in, w_in, w_out, fg)

    tf = FFN_COLS
    return pl.pallas_call(
        functools.partial(_ffn_kernel, tm=tm, row0=0, emit16=False, aliased=True, **common),
        grid=(n_main, f // tf),
        in_specs=[pl.BlockSpec((tm, d), lambda i, j: (i, 0))] + small + [
            pl.BlockSpec((d, tf), lambda i, j: (0, j)),
            pl.BlockSpec((d, tf), lambda i, j: (0, j)),
            pl.BlockSpec((tf, d), lambda i, j: (j, 0)),
            fg_spec,
            pl.BlockSpec(memory_space=pl.ANY),
        ],
        out_specs=pl.BlockSpec((tm, d), lambda i, j: (i, 0)),
        out_shape=jax.ShapeDtypeStruct((rows, d), F32),
        input_output_aliases={7: 0},
        scratch_shapes=[pltpu.VMEM((tm, d), BF16), pltpu.VMEM((2, 1, d), F32)],
        compiler_params=_params("arbitrary", "arbitrary"),
        name="ffn",
    )(h, mod, g, wg16, wu16, wo16, fg, out)


def _rope(x, cos, sin):
    lane = lax.broadcasted_iota(jnp.int32, x.shape, 1)
    first = (lane % 64) < 32
    swapped = jnp.where(first, pltpu.roll(x, 96, 1), pltpu.roll(x, 32, 1))
    return x * cos + swapped * sin


def _proj_kernel(h_ref, mod_ref, g_ref, w_ref, o16_ref, o32_ref, u_scr, gs_scr, *, tm, n_lat, n16):
    i, j = pl.program_id(0), pl.program_id(1)

    @pl.when(j == 0)
    def _():
        _adaln_rows(h_ref, mod_ref, g_ref, u_scr, gs_scr, sub=1, row0=i * tm, tm=tm, n_lat=n_lat)

    y = _dot(u_scr[...], w_ref[...])

    @pl.when(j < n16)
    def _():
        o16_ref[...] = y.astype(BF16)

    @pl.when(j >= n16)
    def _():
        o32_ref[...] = y


def _proj(h, mod, g, w, *, layer, tm, tn, n_lat, cols16):
    rows, d = h.shape
    n = w.shape[2]
    n16 = cols16 // tn
    return pl.pallas_call(
        functools.partial(_proj_kernel, tm=tm, n_lat=n_lat, n16=n16),
        grid=(rows // tm, n // tn),
        in_specs=[
            pl.BlockSpec((tm, d), lambda i, j: (i, 0)),
            pl.BlockSpec((2, 9, d), lambda i, j: (0, 0, 0)),
            pl.BlockSpec((1, d), lambda i, j: (0, 0)),
            pl.BlockSpec((None, d, tn), lambda i, j: (layer, 0, j)),
        ],
        out_specs=[
            pl.BlockSpec((tm, tn), lambda i, j: (i, jnp.minimum(j, n16 - 1))),
            pl.BlockSpec((tm, tn), lambda i, j: (i, jnp.maximum(j - n16, 0))),
        ],
        out_shape=[jax.ShapeDtypeStruct((rows, cols16), BF16), jax.ShapeDtypeStruct((rows, n - cols16), F32)],
        scratch_shapes=[pltpu.VMEM((tm, d), BF16), pltpu.VMEM((2, 1, d), F32)],
        compiler_params=_params("arbitrary", "arbitrary"),
        name="proj",
    )(h, mod, g, w)


def _qkv_kernel(h_ref, mod_ref, g_ref, w_ref, cos_ref, sin_ref, o_ref, u_scr, gs_scr, *, tm, n_lat, rope_cols):
    _adaln_rows(h_ref, mod_ref, g_ref, u_scr, gs_scr, sub=1, row0=pl.program_id(0) * tm, tm=tm, n_lat=n_lat)
    u = u_scr[...]
    cos, sin = cos_ref[...], sin_ref[...]
    for c0 in range(0, o_ref.shape[1], QKV_SLAB):
        y = _dot(u, w_ref[:, c0:c0 + QKV_SLAB])
        for hd in range(0, QKV_SLAB, HEAD_DIM):
            x = y[:, hd:hd + HEAD_DIM]
            if c0 + hd < rope_cols:
                x = _rope(x, cos, sin)
            o_ref[:, c0 + hd:c0 + hd + HEAD_DIM] = x.astype(BF16)


def _qkv_proj(h, mod, g, w, cos, sin, *, layer, tm, n_lat, rope_cols):
    rows, d = h.shape
    n = w.shape[2]
    return pl.pallas_call(
        functools.partial(_qkv_kernel, tm=tm, n_lat=n_lat, rope_cols=rope_cols),
        grid=(rows // tm,),
        in_specs=[
            pl.BlockSpec((tm, d), lambda i: (i, 0)),
            pl.BlockSpec((2, 9, d), lambda i: (0, 0, 0)),
            pl.BlockSpec((1, d), lambda i: (0, 0)),
            pl.BlockSpec((None, d, n), lambda i: (layer, 0, 0), pipeline_mode=pl.Buffered(1)),
            pl.BlockSpec((tm, HEAD_DIM), lambda i: (i, 0)),
            pl.BlockSpec((tm, HEAD_DIM), lambda i: (i, 0)),
        ],
        out_specs=pl.BlockSpec((tm, n), lambda i: (i, 0)),
        out_shape=jax.ShapeDtypeStruct((rows, n), BF16),
        scratch_shapes=[pltpu.VMEM((tm, d), BF16), pltpu.VMEM((2, 1, d), F32)],
        compiler_params=_params("arbitrary"),
        name="qkv_proj",
    )(h, mod, g, w, cos, sin)


def _rope_tables(n_lat, n_ctx):
    t = np.arange(n_lat)
    inv = (ROPE_THETA ** (-np.arange(0, 64, 2, dtype=np.float32) / 64)).astype(np.float32)
    ang_r = (t // GRID_W).astype(np.float32)[:, None] * inv[None, :]
    ang_c = (t % GRID_W).astype(np.float32)[:, None] * inv[None, :]
    cr, sr, cc, sc = np.cos(ang_r), np.sin(ang_r), np.cos(ang_c), np.sin(ang_c)
    cos = np.concatenate([cr, cr, cc, cc], axis=1)
    sin = np.concatenate([-sr, sr, -sc, sc], axis=1)
    cos = np.concatenate([cos, np.ones((n_ctx, HEAD_DIM), np.float32)], axis=0)
    sin = np.concatenate([sin, np.zeros((n_ctx, HEAD_DIM), np.float32)], axis=0)
    return jnp.asarray(cos, F32), jnp.asarray(sin, F32)


def _oproj_kernel(a1_ref, a2_ref, w_ref, h_ref, mod_ref, o_ref, *, tm, n_lat, k1):
    is_ctx = _is_ctx(pl.program_id(0), tm, n_lat)
    y = _dot(a1_ref[...], w_ref[:k1, :]) + _dot(a2_ref[...], w_ref[k1:, :])
    o_ref[...] = h_ref[...] + _mod_rows(mod_ref, 1, 2, is_ctx) * y


def _oproj(a1, a2, col2, w, h, mod, *, layer, tm, n_lat):
    rows, d = h.shape
    k1 = w.shape[1] // 2
    return pl.pallas_call(
        functools.partial(_oproj_kernel, tm=tm, n_lat=n_lat, k1=k1),
        grid=(rows // tm,),
        in_specs=[
            pl.BlockSpec((tm, k1), lambda i: (i, 0)),
            pl.BlockSpec((tm, k1), lambda i: (i, col2)),
            pl.BlockSpec((None,) + w.shape[1:], lambda i: (layer, 0, 0)),
            pl.BlockSpec((tm, d), lambda i: (i, 0)),
            pl.BlockSpec((2, 9, d), lambda i: (0, 0, 0)),
        ],
        out_specs=pl.BlockSpec((tm, d), lambda i: (i, 0)),
        out_shape=jax.ShapeDtypeStruct((rows, d), F32),
        compiler_params=_params("arbitrary"),
        name="oproj",
    )(a1, a2, w, h, mod)


def _oproj_hgrn_kernel(a1_ref, of_ref, ob_ref, g_ref, ng_ref, w_ref, h_ref, mod_ref, o_ref, yb_scr,
                       *, tm, n_lat, k1):
    for c0 in range(0, k1, HEAD_DIM):
        sl = slice(c0, c0 + HEAD_DIM)
        o = of_ref[:, sl] + ob_ref[:, sl]
        o = o * lax.rsqrt(jnp.mean(o * o, axis=-1, keepdims=True) + NORM_EPS)
        yb_scr[:, sl] = (o * ng_ref[:, sl] * _silu(g_ref[:, sl])).astype(BF16)
    is_ctx = _is_ctx(pl.program_id(0), tm, n_lat)
    y = _dot(a1_ref[...], w_ref[:k1, :]) + _dot(yb_scr[...], w_ref[k1:, :])
    o_ref[...] = h_ref[...] + _mod_rows(mod_ref, 1, 2, is_ctx) * y


def _oproj_hgrn(a1, o_f, o_b, p32, gcol0, norm_g, w, h, mod, *, layer, tm, n_lat):
    rows, d = h.shape
    k1 = w.shape[1] // 2
    half = lambda i: (i, 0)
    return pl.pallas_call(
        functools.partial(_oproj_hgrn_kernel, tm=tm, n_lat=n_lat, k1=k1),
        grid=(rows // tm,),
        in_specs=[
            pl.BlockSpec((tm, k1), half),
            pl.BlockSpec((tm, k1), half),
            pl.BlockSpec((tm, k1), half),
            pl.BlockSpec((tm, k1), lambda i: (i, gcol0 // k1)),
            pl.BlockSpec((1, k1), lambda i: (0, 0)),
            pl.BlockSpec((None,) + w.shape[1:], lambda i: (layer, 0, 0)),
            pl.BlockSpec((tm, d), half),
            pl.BlockSpec((2, 9, d), lambda i: (0, 0, 0)),
        ],
        out_specs=pl.BlockSpec((tm, d), half),
        out_shape=jax.ShapeDtypeStruct((rows, d), F32),
        scratch_shapes=[pltpu.VMEM((tm, k1), BF16)],
        compiler_params=_params("arbitrary"),
        name="oproj_hgrn",
    )(a1, o_f, o_b, p32, norm_g, w, h, mod)


def _na_bias(rpb, n_rows):
    nh, n_a, n_b = rpb.shape
    nb = n_rows // NA_Q_ROWS
    w = jnp.full((nh, n_a, 128), MASKED, F32)
    w = w.at[..., :NA_COLS].set(rpb[..., NA_COLS - 1:]).at[..., 128 - (NA_COLS - 1):].set(rpb[..., :NA_COLS - 1])
    toep = jnp.tile(w, (1, 1, GRID_W))[..., :GRID_W * 127].reshape(nh, n_a, GRID_W, 127)[..., :GRID_W]
    toep = jnp.pad(toep.transpose(0, 2, 1, 3), ((0, 0), (0, 0), (NA_K_ROWS, NA_K_ROWS), (0, 0)),
                   constant_values=MASKED)

    q = np.arange(ATT_BLOCK)
    k = np.arange(NA_K_ROWS * GRID_W)
    dr, c = q // GRID_W, q % GRID_W
    kr, kc = k // GRID_W, k % GRID_W
    c0 = np.clip(c - NA_COLS // 2, 0, GRID_W - NA_COLS)
    col_ok = (kc[None, :] >= c0[:, None]) & (kc[None, :] < c0[:, None] + NA_COLS)
    tables, ok = [], []
    for b in (0, 1, nb - 1):
        r = b * NA_Q_ROWS + dr
        r0 = np.clip(r - NA_ROWS // 2, 0, n_rows - NA_ROWS)
        base = int(np.clip(b * NA_Q_ROWS - NA_ROWS // 2, 0, n_rows - NA_K_ROWS))
        kabs = base + kr
        ok.append((kabs[None, :] >= r0[:, None]) & (kabs[None, :] < r0[:, None] + NA_ROWS) & col_ok)
        per_row = []
        for j in range(NA_Q_ROWS):
            lo = base - (b * NA_Q_ROWS + j) + NA_ROWS - 1 + NA_K_ROWS
            per_row.append(toep[:, :, lo:lo + NA_K_ROWS].reshape(nh, GRID_W, NA_K_ROWS * GRID_W))
        tables.append(jnp.stack(per_row, axis=1).reshape(nh, ATT_BLOCK, NA_K_ROWS * GRID_W))
    return jnp.where(np.stack(ok)[:, None], jnp.stack(tables), MASKED)


def _na_kernel(q_ref, k_ref, v_ref, bias_ref, o_ref, *, n_lat, n_ctx, nb):
    b = pl.program_id(1)
    scale = HEAD_DIM ** -0.5
    nk = NA_K_ROWS * GRID_W
    start = pl.multiple_of(jnp.clip(b - 1, 0, nb - 3) * ATT_BLOCK, ATT_BLOCK)

    def one_head(hh, local):
        cols = slice(hh * HEAD_DIM, (hh + 1) * HEAD_DIM)
        q = q_ref[:, cols]
        s_ctx = _dot_nt(q, k_ref[n_lat:n_lat + n_ctx, cols]) * scale
        vc1 = _with_ones(v_ref[n_lat:n_lat + n_ctx, cols])
        if local:
            kl = k_ref[pl.ds(start, nk), cols]
            vl1 = _with_ones(v_ref[pl.ds(start, nk), cols])
            s_loc = _dot_nt(q, kl) * scale + bias_ref[0, hh]
            m = _row_max(s_loc, s_ctx)
            acc = _dot(jnp.exp(s_loc - m).astype(BF16), vl1) + _dot(jnp.exp(s_ctx - m).astype(BF16), vc1)
        else:
            acc = _dot(jnp.exp(s_ctx - _row_max(s_ctx)).astype(BF16), vc1)
        o_ref[:, cols] = (acc[:, :HEAD_DIM] / acc[:, HEAD_DIM:]).astype(o_ref.dtype)

    @pl.when(b < nb)
    def _():
        for hh in range(NA_HEADS_PER_STEP):
            one_head(hh, True)

    @pl.when(b >= nb)
    def _():
        for hh in range(NA_HEADS_PER_STEP):
            one_head(hh, False)


def _na_attention(p, bias, *, layer, n_lat, n_ctx, heads):
    rows = n_lat + n_ctx
    nb = n_lat // ATT_BLOCK
    nblk = rows // ATT_BLOCK
    hps = NA_HEADS_PER_STEP
    width = hps * HEAD_DIM
    groups = heads // hps

    def pattern(b):
        return jnp.where(b == 0, 0, jnp.where(b == nb - 1, 2, 1))

    return pl.pallas_call(
        functools.partial(_na_kernel, n_lat=n_lat, n_ctx=n_ctx, nb=nb),
        grid=(groups, nblk),
        in_specs=[
            pl.BlockSpec((ATT_BLOCK, width), lambda h, b: (b, h)),
            pl.BlockSpec((rows, width), lambda h, b: (0, groups + h)),
            pl.BlockSpec((rows, width), lambda h, b: (0, 2 * groups + h)),
            pl.BlockSpec((1, hps, ATT_BLOCK, NA_K_ROWS * GRID_W),
                         lambda h, b: (pattern(b), layer * groups + h, 0, 0)),
        ],
        out_specs=pl.BlockSpec((ATT_BLOCK, width), lambda h, b: (b, h)),
        out_shape=jax.ShapeDtypeStruct((rows, heads * HEAD_DIM), BF16),
        compiler_params=_params("arbitrary", "arbitrary"),
        name="na_attention",
    )(p, p, p, bias)


def _hgrn_levels():
    levels, m = [], 1
    while m < HG_CHUNK:
        levels.append(m)
        m *= 2
    return levels


def _hgrn_consts():
    c = HG_CHUNK
    levels = _hgrn_levels()
    fine = [m for m in levels if m < HG_ROW_LEVEL]
    idx = np.arange(c)
    t, j = idx[:, None], idx[None, :]
    n = np.zeros((2, 1 + len(fine), c, c), np.float32)
    msk = np.zeros((2, 1 + len(levels), c, c), np.float32)
    n[0, 0], n[1, 0] = j <= t, j >= t
    msk[0, 0] = msk[1, 0] = np.eye(c)
    for li, m in enumerate(levels):
        seg = idx // (2 * m)
        right = ((idx % (2 * m)) >= m)[:, None]
        last_left = (seg * 2 * m + m - 1)[:, None]
        first_right = last_left + 1
        if m < HG_ROW_LEVEL:
            n[0, 1 + li] = np.where(right, (j > last_left) & (j <= t), (j > t) & (j <= last_left))
            n[1, 1 + li] = np.where(right, (j >= first_right) & (j < t), (j >= t) & (j < first_right))
        same = seg[:, None] == seg[None, :]
        msk[0, 1 + li] = same & right & ~right.T
        msk[1, 1 + li] = same & ~right & right.T
    return n.reshape(2, -1, c), msk


def _hgrn_chunk(d, q, fx, v, lb, n_ref, msk_ref, s_scr, b_scr):
    c = HG_CHUNK
    t = jnp.exp(-jnp.abs(fx))
    r = 1.0 / (1.0 + t)
    log_sig = jnp.minimum(fx, 0.0) + jnp.log(r)
    sig_neg = jnp.where(fx >= 0.0, t * r, r)
    if lb is None:
        log_f, kk = log_sig, sig_neg
    else:
        la = jnp.log(lb)
        lc = jnp.log1p(-lb) + log_sig
        log_f = jnp.maximum(la, lc) + jnp.log(1.0 + jnp.exp(-jnp.abs(la - lc)))
        kk = (1.0 - lb) * sig_neg
    qf = q.astype(F32)

    hi = log_f.astype(BF16)
    mid = (log_f - hi.astype(F32)).astype(BF16)
    dd = _dot(n_ref[d], jnp.concatenate([hi, mid], axis=1))
    dd = dd[:, :HEAD_DIM] + dd[:, HEAD_DIM:]
    b = dd[:c]
    b_scr[d] = b

    attn = _dot_nt(q, kk.astype(BF16)) * msk_ref[d, 0]
    for li, m in enumerate(_hgrn_levels()):
        if m < HG_ROW_LEVEL:
            ex = dd[(1 + li) * c:(2 + li) * c]
        else:
            row = m - 1 + d
            ref = jnp.concatenate(
                [jnp.broadcast_to(b_scr[d, s + row:s + row + 1, :], (2 * m, HEAD_DIM)) for s in range(0, c, 2 * m)],
                axis=0)
            ex = -jnp.abs(b - ref)
        e = jnp.exp(ex)
        attn += _dot_nt((qf * e).astype(BF16), (kk * e).astype(BF16)) * msk_ref[d, 1 + li]

    total = b[c - 1:c] if d == 0 else b[0:1]
    state_t = s_scr[d]
    o = _dot_nt((qf * jnp.exp(b)).astype(BF16), state_t.astype(BF16)) + _dot(attn.astype(BF16), v)
    v_t = v.astype(F32).T.astype(BF16)
    s_scr[d] = jnp.exp(total) * state_t + _dot(v_t, (kk * jnp.exp(total - b)).astype(BF16))
    return o


def _hgrn_kernel(qf_ref, qb_ref, ff_ref, fb_ref, if_ref, ib_ref, lbl_ref, n_ref, msk_ref,
                 of_ref, ob_ref, s_scr, b_scr, *, layer_e):
    @pl.when(pl.program_id(1) == 0)
    def _():
        s_scr[...] = jnp.zeros_like(s_scr)

    refs = ((qf_ref, ff_ref, if_ref, of_ref), (qb_ref, fb_ref, ib_ref, ob_ref))
    lbs = [None, None]
    if layer_e > 0:
        for d in range(2):
            lg = lbl_ref[d]
            ex = jnp.exp(lg - jnp.max(lg, axis=0, keepdims=True))
            num = ex[1:2]
            for e in range(2, layer_e + 1):
                num = num + ex[e:e + 1]
            lbs[d] = num / jnp.sum(ex, axis=0, keepdims=True)

    n_sub = HG_STEP // HG_CHUNK
    for k in range(n_sub):
        for hh in range(HG_HEADS_PER_STEP):
            cols = slice(hh * HEAD_DIM, (hh + 1) * HEAD_DIM)
            for d, (q_ref, f_ref, i_ref, o_ref) in enumerate(refs):
                sub = k if d == 0 else n_sub - 1 - k
                rows = slice(sub * HG_CHUNK, (sub + 1) * HG_CHUNK)
                lb = None if lbs[d] is None else lbs[d][:, cols]
                o_ref[rows, cols] = _hgrn_chunk(d, q_ref[rows, cols], f_ref[rows, cols], i_ref[rows, cols], lb,
                                                n_ref, msk_ref, s_scr.at[hh], b_scr.at[hh, k])


def _hgrn_scan(p16, p32, lb_logits, consts, *, layer_e, n_lat, n_ctx, heads, qcol, icol):
    rows = n_lat + n_ctx
    n_mat, msk = consts
    lat_chunks = n_lat // HG_STEP
    qc, ic = qcol // HEAD_DIM, icol // HEAD_DIM

    def fwd(s):
        return jnp.where(s == 0, lat_chunks, s - 1)

    def bwd(s):
        return lat_chunks - s

    hps = HG_HEADS_PER_STEP
    width = hps * HEAD_DIM

    def spec(blk, col):
        return pl.BlockSpec((HG_STEP, width), lambda h, s: (blk(s), col // hps + h))

    out = jax.ShapeDtypeStruct((rows, heads * HEAD_DIM), F32)
    return pl.pallas_call(
        functools.partial(_hgrn_kernel, layer_e=layer_e),
        grid=(heads // hps, 1 + lat_chunks),
        in_specs=[
            spec(fwd, qc), spec(bwd, qc),
            spec(fwd, 0), spec(bwd, heads),
            spec(fwd, ic), spec(bwd, ic),
            pl.BlockSpec((2, lb_logits.shape[1], width), lambda h, s: (0, 0, h)),
            pl.BlockSpec(n_mat.shape, lambda h, s: (0, 0, 0)),
            pl.BlockSpec(msk.shape, lambda h, s: (0, 0, 0, 0)),
        ],
        out_specs=[spec(fwd, 0), spec(bwd, 0)],
        out_shape=[out, out],
        scratch_shapes=[pltpu.VMEM((hps, 2, HEAD_DIM, HEAD_DIM), F32),
                        pltpu.VMEM((hps, HG_STEP // HG_CHUNK, 2, HG_CHUNK, HEAD_DIM), F32)],
        compiler_params=_params("arbitrary", "arbitrary"),
        name="hgrn_scan",
    )(p16, p16, p32, p32, p16, p16, lb_logits, n_mat, msk)


def _hgrn_out_kernel(of_ref, ob_ref, g_ref, ng_ref, y_ref):
    o = of_ref[...] + ob_ref[...]
    o = o * lax.rsqrt(jnp.mean(o * o, axis=-1, keepdims=True) + NORM_EPS)
    y_ref[...] = (o * ng_ref[...] * _silu(g_ref[...])).astype(y_ref.dtype)


def _hgrn_out(o_f, o_b, p, norm_g, *, heads, gcol0, tm):
    rows = o_f.shape[0]
    return pl.pallas_call(
        _hgrn_out_kernel,
        grid=(rows // tm, heads),
        in_specs=[
            pl.BlockSpec((tm, HEAD_DIM), lambda i, h: (i, h)),
            pl.BlockSpec((tm, HEAD_DIM), lambda i, h: (i, h)),
            pl.BlockSpec((tm, HEAD_DIM), lambda i, h: (i, gcol0 // HEAD_DIM + h)),
            pl.BlockSpec((1, HEAD_DIM), lambda i, h: (0, h)),
        ],
        out_specs=pl.BlockSpec((tm, HEAD_DIM), lambda i, h: (i, h)),
        out_shape=jax.ShapeDtypeStruct((rows, heads * HEAD_DIM), BF16),
        compiler_params=_params("arbitrary", "arbitrary"),
        name="hgrn_out",
    )(o_f, o_b, p, norm_g)


def _swa_kernel(q_ref, k_ref, v_ref, sink_ref, o_ref, *, n_lat, n_ctx, nb, group, span):
    b = pl.program_id(1)
    scale = HEAD_DIM ** -0.5
    kc = k_ref[n_lat:n_lat + n_ctx, :]
    vc1 = _with_ones(v_ref[n_lat:n_lat + n_ctx, :])

    def finish(g, s_ctx, s_loc, vl1):
        sink = sink_ref[0, g:g + 1, 0:1]
        blocks = (s_ctx,) if s_loc is None else (s_ctx, s_loc)
        m = jnp.maximum(_row_max(*blocks), sink)
        acc = _dot(jnp.exp(s_ctx - m).astype(BF16), vc1)
        if s_loc is not None:
            acc = acc + _dot(jnp.exp(s_loc - m).astype(BF16), vl1)
        den = acc[:, HEAD_DIM:] + jnp.exp(sink - m)
        o_ref[:, g * HEAD_DIM:(g + 1) * HEAD_DIM] = (acc[:, :HEAD_DIM] / den).astype(o_ref.dtype)

    @pl.when(b < nb)
    def _():
        start = pl.multiple_of(jnp.clip(b * ATT_BLOCK - WINDOW, 0, n_lat - span), WINDOW)
        kl = k_ref[pl.ds(start, span), :]
        vl = _with_ones(v_ref[pl.ds(start, span), :])
        qpos = b * ATT_BLOCK + lax.broadcasted_iota(jnp.int32, (ATT_BLOCK, span), 0)
        kpos = start + lax.broadcasted_iota(jnp.int32, (ATT_BLOCK, span), 1)
        valid = jnp.abs(kpos - qpos) <= WINDOW
        for g in range(group):
            q = q_ref[:, g * HEAD_DIM:(g + 1) * HEAD_DIM]
            s_loc = jnp.where(valid, _dot_nt(q, kl) * scale, MASKED)
            finish(g, _dot_nt(q, kc) * scale, s_loc, vl)

    @pl.when(b >= nb)
    def _():
        for g in range(group):
            q = q_ref[:, g * HEAD_DIM:(g + 1) * HEAD_DIM]
            finish(g, _dot_nt(q, kc) * scale, None, None)


def _swa_attention(p, sink, *, n_lat, n_ctx, heads, kv_heads):
    rows = n_lat + n_ctx
    group = heads // kv_heads
    nb = n_lat // ATT_BLOCK
    span = ATT_BLOCK + 2 * WINDOW
    sink_tab = jnp.broadcast_to(sink.astype(F32).reshape(kv_heads, group, 1), (kv_heads, group, HEAD_DIM))
    return pl.pallas_call(
        functools.partial(_swa_kernel, n_lat=n_lat, n_ctx=n_ctx, nb=nb, group=group, span=span),
        grid=(kv_heads, rows // ATT_BLOCK),
        in_specs=[
            pl.BlockSpec((ATT_BLOCK, group * HEAD_DIM), lambda k, b: (b, k)),
            pl.BlockSpec((rows, HEAD_DIM), lambda k, b: (0, heads + k)),
            pl.BlockSpec((rows, HEAD_DIM), lambda k, b: (0, heads + kv_heads + k)),
            pl.BlockSpec((1, group, HEAD_DIM), lambda k, b: (k, 0, 0)),
        ],
        out_specs=pl.BlockSpec((ATT_BLOCK, group * HEAD_DIM), lambda k, b: (b, k)),
        out_shape=jax.ShapeDtypeStruct((rows, heads * HEAD_DIM), BF16),
        compiler_params=_params("arbitrary", "arbitrary"),
        name="swa_attention",
    )(p, p, p, sink_tab)


def kernel(x, c, ctx, c_ctx, w_mod, b_mod, norm_g, w_ff_in, w_ff_out, w_in_even, w_out_even,
           na_rpb, hg_lb_logits, hg_norm_g, w_qkv_odd, w_o_odd, sink_odd, final_norm_g):
    assert x.shape[0] == 1
    depth, d = w_mod.shape[0], x.shape[2]
    n_lat, n_ctx = x.shape[1], ctx.shape[1]
    rows = n_lat + n_ctx
    tm = 768
    assert rows % tm == 0 and n_lat % 1024 == 0 and n_ctx == ATT_BLOCK
    na_heads = na_rpb.shape[1]
    a_w = na_heads * HEAD_DIM
    hg_heads = hg_norm_g.shape[1] // HEAD_DIM
    heads = sink_odd.shape[1]
    kv_heads = (w_qkv_odd.shape[2] // HEAD_DIM - heads) // 2

    cc =jnp.zeros((8, d), F32).at[0].set(c[0]).at[1].set(c_ctx)
    mod_all = _modulation(cc, w_mod, b_mod)[:, :2].reshape(depth, 2, 9, d)

    cos, sin = _rope_tables(n_lat, n_ctx)
    na_bias = _na_bias(na_rpb.reshape((-1,) + na_rpb.shape[2:]), n_lat // GRID_W)
    hg_consts = _hgrn_consts()
    hg_consts = (jnp.asarray(hg_consts[0], BF16), jnp.asarray(hg_consts[1], F32))
    fg = final_norm_g.reshape(1, d)
    b_w = hg_heads * HEAD_DIM
    c_f, c_i, c_g = 3 * a_w + b_w, 3 * a_w + 3 * b_w, 3 * a_w + 4 * b_w
    w_in_even = jnp.concatenate([w_in_even[:, :, :c_f], w_in_even[:, :, c_i:c_g], w_in_even[:, :, c_f:c_i],
                                 w_in_even[:, :, c_g:]], axis=2).astype(BF16)
    w_out_even = w_out_even.astype(BF16)
    w_qkv_odd, w_o_odd = w_qkv_odd.astype(BF16), w_o_odd.astype(BF16)

    h = x[0]
    h_tail = jnp.concatenate([x[0, rows - tm:], ctx[0]], axis=0)
    for l in range(depth):
        mod = mod_all[l]
        last = l == depth - 1
        h = _ffn(h, mod, norm_g[l, 0].reshape(1, d), w_ff_in, w_ff_out, fg, layer=l, half=0,
                 sub=0, rows=rows, tm=tm, n_lat=n_lat, h_tail=h_tail if l == 0 else None)
        g1 = norm_g[l, 1].reshape(1, d)
        if l % 2 == 0:
            e = l // 2
            p16, p32 = _proj(h, mod, g1, w_in_even, layer=e, tm=tm, tn=1024, n_lat=n_lat, cols16=3 * a_w + 2 * b_w)
            ya = _na_attention(p16, na_bias, layer=e, n_lat=n_lat, n_ctx=n_ctx, heads=na_heads)
            o_f, o_b = _hgrn_scan(p16, p32, hg_lb_logits, hg_consts, layer_e=e, n_lat=n_lat, n_ctx=n_ctx,
                                  heads=hg_heads, qcol=3 * a_w, icol=3 * a_w + b_w)
            h = _oproj_hgrn(ya, o_f, o_b, p32, 2 * b_w, hg_norm_g[e].reshape(1, -1), w_out_even, h, mod,
                            layer=e, tm=tm // 2, n_lat=n_lat)
        else:
            o = l // 2
            p16 = _qkv_proj(h, mod, g1, w_qkv_odd, cos, sin, layer=o, tm=tm, n_lat=n_lat,
                            rope_cols=(heads + kv_heads) * HEAD_DIM)
            y = _swa_attention(p16, sink_odd[o], n_lat=n_lat, n_ctx=n_ctx, heads=heads, kv_heads=kv_heads)
            h = _oproj(y, y, 1, w_o_odd, h, mod, layer=o, tm=tm, n_lat=n_lat)
        g2 = norm_g[l, 2].reshape(1, d)
        if last:
            h = _ffn(h, mod, g2, w_ff_in, w_ff_out, fg, layer=l, half=1,
                     sub=2, rows=n_lat, tm=tm, n_lat=n_lat, final_norm=True)
        else:
            h = _ffn(h, mod, g2, w_ff_in, w_ff_out, fg, layer=l, half=1,
                     sub=2, rows=rows, tm=tm, n_lat=n_lat)
    return h[None]
```

```python
import functools

import numpy as np
import jax
import jax.numpy as jnp
from jax import lax
from jax.experimental import pallas as pl
from jax.experimental.pallas import tpu as pltpu

F32 = jnp.float32
BF16 = jnp.bfloat16

GRID_W = 64
NORM_EPS = 1e-6
ROPE_THETA = 10000.0
HEAD_DIM = 128
NA_ROWS = 8
NA_COLS = 16
NA_Q_ROWS = 4
NA_K_ROWS = NA_Q_ROWS + NA_ROWS
ATT_BLOCK = NA_Q_ROWS * GRID_W
NA_HEADS_PER_STEP = 4
SWA_KV_PER_STEP = 2
WINDOW = 128
HG_STEP = 256
HG_CHUNK = 128
HG_HEADS_PER_STEP = 2
ROW_CHUNK = 16
FFN_COLS = 512
FFN_TAIL_COLS = 256
QKV_SLAB = 512
FFN_OUT_COLS = 512
ROW_UNROLL = 4
HG_ROW_LEVEL = 8
MASKED = -1e30
VMEM_LIMIT = 56 * 1024 * 1024


def _params(*sem):
    return pltpu.CompilerParams(dimension_semantics=sem, vmem_limit_bytes=VMEM_LIMIT)


def _dot(a, b):
    return jnp.dot(a, b, preferred_element_type=F32)


def _dot_nt(a, b):
    return lax.dot_general(a, b, (((1,), (1,)), ((), ())), preferred_element_type=F32)


def _sigmoid(x):
    return 1.0 / (1.0 + jnp.exp(-x))


def _silu(x):
    return x * _sigmoid(x)


def _mod_kernel(c_ref, w_ref, b_ref, o_ref):
    s = _silu(c_ref[...]).astype(BF16)
    o_ref[0] = _dot(s, w_ref[0].astype(BF16)) + b_ref[0]


def _modulation(cc, w_mod, b_mod):
    depth, d, n = w_mod.shape
    tn = 1024
    return pl.pallas_call(
        _mod_kernel,
        grid=(depth, n // tn),
        in_specs=[
            pl.BlockSpec((8, d), lambda l, j: (0, 0)),
            pl.BlockSpec((1, d, tn), lambda l, j: (l, 0, j)),
            pl.BlockSpec((1, 1, tn), lambda l, j: (l, 0, j)),
        ],
        out_specs=pl.BlockSpec((1, 8, tn), lambda l, j: (l, 0, j)),
        out_shape=jax.ShapeDtypeStruct((depth, 8, n), F32),
        compiler_params=_params("arbitrary", "arbitrary"),
        name="modulation",
    )(cc, w_mod, b_mod.reshape(depth, 1, n))


def _mod_rows(mod_ref, sub, k, is_ctx):
    r = 3 * sub + k
    return jnp.where(is_ctx, mod_ref[1, r:r + 1, :], mod_ref[0, r:r + 1, :])


def _row_max(*blocks):
    acc = None
    for s in blocks:
        for c0 in range(0, s.shape[1], HEAD_DIM):
            t = s[:, c0:c0 + HEAD_DIM]
            acc = t if acc is None else jnp.maximum(acc, t)
    return jnp.max(acc, axis=-1, keepdims=True)


def _with_ones(v):
    return jnp.concatenate([v, jnp.ones_like(v)], axis=1)


def _is_ctx(tile, tm, n_lat):
    row = tile * tm + lax.broadcasted_iota(jnp.int32, (tm, 1), 0)
    return row >= n_lat


def _row_chunks(row0, tm, n_lat, body):
    def step(c, carry):
        r0 = pl.multiple_of(c * ROW_CHUNK, ROW_CHUNK)
        body(pl.ds(r0, ROW_CHUNK), (row0 + r0 >= n_lat).astype(jnp.int32))
        return carry

    lax.fori_loop(0, tm // ROW_CHUNK, step, 0, unroll=ROW_UNROLL)


def _adaln_rows(h_ref, mod_ref, g_ref, u_scr, gs_scr, *, sub, row0, tm, n_lat):
    for which in range(2):
        gs_scr[which] = g_ref[...] * (1.0 + mod_ref[which, 3 * sub + 1:3 * sub + 2, :])

    def body(rows, which):
        x = h_ref[rows, :]
        r = lax.rsqrt(jnp.mean(x * x, axis=-1, keepdims=True) + NORM_EPS)
        u_scr[rows, :] = ((x * r) * gs_scr[which] + mod_ref[which, 3 * sub:3 * sub + 1, :]).astype(BF16)

    _row_chunks(row0, tm, n_lat, body)


def _ffn_kernel(h_ref, mod_ref, g_ref, wg_ref, wu_ref, wo_ref, fg_ref, *rest,
                sub, tm, row0, n_lat, final_norm, emit16, aliased):
    rest = rest[1:] if aliased else rest
    o_ref = rest[0]
    u_scr, gs_scr = rest[-2:]
    i, j = pl.program_id(0), pl.program_id(1)
    first_row = row0 + i * tm

    @pl.when(j == 0)
    def _():
        _adaln_rows(h_ref, mod_ref, g_ref, u_scr, gs_scr, sub=sub, row0=first_row, tm=tm, n_lat=n_lat)
        o_ref[...] = jnp.zeros_like(o_ref)

    if emit16:
        wg16_ref, wu16_ref, wo16_ref = rest[1:4]
        wg16_ref[...] = wg_ref[...].astype(BF16)
        wu16_ref[...] = wu_ref[...].astype(BF16)
        wo16_ref[...] = wo_ref[...].astype(BF16)
        wg_ref, wu_ref, wo_ref = wg16_ref, wu16_ref, wo16_ref
    u = u_scr[...]
    a = (_silu(_dot(u, wg_ref[...])) * _dot(u, wu_ref[...])).astype(BF16)
    d = o_ref.shape[1]
    for c0 in range(0, d, FFN_OUT_COLS):
        o_ref[:, c0:c0 + FFN_OUT_COLS] += _dot(a, wo_ref[:, c0:c0 + FFN_OUT_COLS])

    @pl.when(j == pl.num_programs(1) - 1)
    def _():
        def body(rows, which):
            h = h_ref[rows, :] + (0.5 * mod_ref[which, 3 * sub + 2:3 * sub + 3, :]) * o_ref[rows, :]
            if final_norm:
                h = h * lax.rsqrt(jnp.mean(h * h, axis=-1, keepdims=True) + NORM_EPS) * fg_ref[...]
            o_ref[rows, :] = h

        _row_chunks(first_row, tm, n_lat, body)


def _ffn(h, mod, g, w_in, w_out, fg, *, layer, half, sub, rows, tm, n_lat, final_norm=False, h_tail=None):
    d = h.shape[1]
    f = w_out.shape[2]
    tail = rows % tm or tm
    n_main = (rows - tail) // tm
    assert (rows - tail) % tail == 0
    tail_src, tail_blk = (h, (rows - tail) // tail) if h_tail is None else (h_tail, 0)
    common = dict(sub=sub, n_lat=n_lat, final_norm=final_norm)
    small = [
        pl.BlockSpec((2, 9, d), lambda i, j: (0, 0, 0)),
        pl.BlockSpec((1, d), lambda i, j: (0, 0)),
    ]
    fg_spec = pl.BlockSpec((1, d), lambda i, j: (0, 0))

    tf, t0 = FFN_TAIL_COLS, (rows - tail) // tail
    nf = f // tf
    out, wg16, wu16, wo16 = pl.pallas_call(
        functools.partial(_ffn_kernel, tm=tail, row0=rows - tail, emit16=True, aliased=False, **common),
        grid=(1, nf),
        in_specs=[pl.BlockSpec((tail, d), lambda i, j: (tail_blk, 0))] + small + [
            pl.BlockSpec((None, None, d, tf), lambda i, j: (layer, half, 0, j)),
            pl.BlockSpec((None, None, d, tf), lambda i, j: (layer, half, 0, nf + j)),
            pl.BlockSpec((None, None, tf, d), lambda i, j: (layer, half, j, 0)),
            fg_spec,
        ],
        out_specs=[
            pl.BlockSpec((tail, d), lambda i, j: (t0, 0)),
            pl.BlockSpec((d, tf), lambda i, j: (0, j)),
            pl.BlockSpec((d, tf), lambda i, j: (0, j)),
            pl.BlockSpec((tf, d), lambda i, j: (j, 0)),
        ],
        out_shape=[
            jax.ShapeDtypeStruct((rows, d), F32),
            jax.ShapeDtypeStruct((d, f), BF16),
            jax.ShapeDtypeStruct((d, f), BF16),
            jax.ShapeDtypeStruct((f, d), BF16),
        ],
        scratch_shapes=[pltpu.VMEM((tail, d), BF16), pltpu.VMEM((2, 1, d), F32)],
        compiler_params=_params("arbitrary", "arbitrary"),
        name="ffn_tail",
    )(tail_src, mod, g, w_in, w_in, w_out, fg)

    tf = FFN_COLS
    return pl.pallas_call(
        functools.partial(_ffn_kernel, tm=tm, row0=0, emit16=False, aliased=True, **common),
        grid=(n_main, f // tf),
        in_specs=[pl.BlockSpec((tm, d), lambda i, j: (i, 0))] + small + [
            pl.BlockSpec((d, tf), lambda i, j: (0, j)),
            pl.BlockSpec((d, tf), lambda i, j: (0, j)),
            pl.BlockSpec((tf, d), lambda i, j: (j, 0)),
            fg_spec,
            pl.BlockSpec(memory_space=pl.ANY),
        ],
        out_specs=pl.BlockSpec((tm, d), lambda i, j: (i, 0)),
        out_shape=jax.ShapeDtypeStruct((rows, d), F32),
        input_output_aliases={7: 0},
        scratch_shapes=[pltpu.VMEM((tm, d), BF16), pltpu.VMEM((2, 1, d), F32)],
        compiler_params=_params("arbitrary", "arbitrary"),
        name="ffn",
    )(h, mod, g, wg16, wu16, wo16, fg, out)


def _rope(x, cos, sin):
    lane = lax.broadcasted_iota(jnp.int32, x.shape, 1)
    first = (lane % 64) < 32
    swapped = jnp.where(first, pltpu.roll(x, 96, 1), pltpu.roll(x, 32, 1))
    return x * cos + swapped * sin


def _proj_kernel(h_ref, mod_ref, g_ref, w_ref, o16_ref, o32_ref, u_scr, gs_scr, *, tm, n_lat, n16):
    i, j = pl.program_id(0), pl.program_id(1)

    @pl.when(j == 0)
    def _():
        _adaln_rows(h_ref, mod_ref, g_ref, u_scr, gs_scr, sub=1, row0=i * tm, tm=tm, n_lat=n_lat)

    y = _dot(u_scr[...], w_ref[...])

    @pl.when(j < n16)
    def _():
        o16_ref[...] = y.astype(BF16)

    @pl.when(j >= n16)
    def _():
        o32_ref[...] = y


def _proj(h, mod, g, w, *, layer, tm, tn, n_lat, cols16):
    rows, d = h.shape
    n = w.shape[2]
    n16 = cols16 // tn
    return pl.pallas_call(
        functools.partial(_proj_kernel, tm=tm, n_lat=n_lat, n16=n16),
        grid=(rows // tm, n // tn),
        in_specs=[
            pl.BlockSpec((tm, d), lambda i, j: (i, 0)),
            pl.BlockSpec((2, 9, d), lambda i, j: (0, 0, 0)),
            pl.BlockSpec((1, d), lambda i, j: (0, 0)),
            pl.BlockSpec((None, d, tn), lambda i, j: (layer, 0, j)),
        ],
        out_specs=[
            pl.BlockSpec((tm, tn), lambda i, j: (i, jnp.minimum(j, n16 - 1))),
            pl.BlockSpec((tm, tn), lambda i, j: (i, jnp.maximum(j - n16, 0))),
        ],
        out_shape=[jax.ShapeDtypeStruct((rows, cols16), BF16), jax.ShapeDtypeStruct((rows, n - cols16), F32)],
        scratch_shapes=[pltpu.VMEM((tm, d), BF16), pltpu.VMEM((2, 1, d), F32)],
        compiler_params=_params("arbitrary", "arbitrary"),
        name="proj",
    )(h, mod, g, w)


def _qkv_kernel(h_ref, mod_ref, g_ref, w_ref, cos_ref, sin_ref, o_ref, u_scr, gs_scr, *, tm, n_lat, rope_cols):
    _adaln_rows(h_ref, mod_ref, g_ref, u_scr, gs_scr, sub=1, row0=pl.program_id(0) * tm, tm=tm, n_lat=n_lat)
    u = u_scr[...]
    cos, sin = cos_ref[...], sin_ref[...]
    for c0 in range(0, o_ref.shape[1], QKV_SLAB):
        y = _dot(u, w_ref[:, c0:c0 + QKV_SLAB])
        for hd in range(0, QKV_SLAB, HEAD_DIM):
            x = y[:, hd:hd + HEAD_DIM]
            if c0 + hd < rope_cols:
                x = _rope(x, cos, sin)
            o_ref[:, c0 + hd:c0 + hd + HEAD_DIM] = x.astype(BF16)


def _qkv_proj(h, mod, g, w, cos, sin, *, layer, tm, n_lat, rope_cols):
    rows, d = h.shape
    n = w.shape[2]
    return pl.pallas_call(
        functools.partial(_qkv_kernel, tm=tm, n_lat=n_lat, rope_cols=rope_cols),
        grid=(rows // tm,),
        in_specs=[
            pl.BlockSpec((tm, d), lambda i: (i, 0)),
            pl.BlockSpec((2, 9, d), lambda i: (0, 0, 0)),
            pl.BlockSpec((1, d), lambda i: (0, 0)),
            pl.BlockSpec((None, d, n), lambda i: (layer, 0, 0), pipeline_mode=pl.Buffered(1)),
            pl.BlockSpec((tm, HEAD_DIM), lambda i: (i, 0)),
            pl.BlockSpec((tm, HEAD_DIM), lambda i: (i, 0)),
        ],
        out_specs=pl.BlockSpec((tm, n), lambda i: (i, 0)),
        out_shape=jax.ShapeDtypeStruct((rows, n), BF16),
        scratch_shapes=[pltpu.VMEM((tm, d), BF16), pltpu.VMEM((2, 1, d), F32)],
        compiler_params=_params("arbitrary"),
        name="qkv_proj",
    )(h, mod, g, w, cos, sin)


def _rope_tables(n_lat, n_ctx):
    t = np.arange(n_lat)
    inv = (ROPE_THETA ** (-np.arange(0, 64, 2, dtype=np.float32) / 64)).astype(np.float32)
    ang_r = (t // GRID_W).astype(np.float32)[:, None] * inv[None, :]
    ang_c = (t % GRID_W).astype(np.float32)[:, None] * inv[None, :]
    cr, sr, cc, sc = np.cos(ang_r), np.sin(ang_r), np.cos(ang_c), np.sin(ang_c)
    cos = np.concatenate([cr, cr, cc, cc], axis=1)
    sin = np.concatenate([-sr, sr, -sc, sc], axis=1)
    cos = np.concatenate([cos, np.ones((n_ctx, HEAD_DIM), np.float32)], axis=0)
    sin = np.concatenate([sin, np.zeros((n_ctx, HEAD_DIM), np.float32)], axis=0)
    return jnp.asarray(cos, F32), jnp.asarray(sin, F32)


def _oproj_kernel(a1_ref, a2_ref, w_ref, h_ref, mod_ref, o_ref, *, tm, n_lat, k1):
    is_ctx = _is_ctx(pl.program_id(0), tm, n_lat)
    y = _dot(a1_ref[...], w_ref[:k1, :]) + _dot(a2_ref[...], w_ref[k1:, :])
    o_ref[...] = h_ref[...] + _mod_rows(mod_ref, 1, 2, is_ctx) * y


def _oproj(a1, a2, col2, w, h, mod, *, layer, tm, n_lat):
    rows, d = h.shape
    k1 = w.shape[1] // 2
    return pl.pallas_call(
        functools.partial(_oproj_kernel, tm=tm, n_lat=n_lat, k1=k1),
        grid=(rows // tm,),
        in_specs=[
            pl.BlockSpec((tm, k1), lambda i: (i, 0)),
            pl.BlockSpec((tm, k1), lambda i: (i, col2)),
            pl.BlockSpec((None,) + w.shape[1:], lambda i: (layer, 0, 0)),
            pl.BlockSpec((tm, d), lambda i: (i, 0)),
            pl.BlockSpec((2, 9, d), lambda i: (0, 0, 0)),
        ],
        out_specs=pl.BlockSpec((tm, d), lambda i: (i, 0)),
        out_shape=jax.ShapeDtypeStruct((rows, d), F32),
        compiler_params=_params("arbitrary"),
        name="oproj",
    )(a1, a2, w, h, mod)


def _oproj_hgrn_kernel(a1_ref, of_ref, ob_ref, g_ref, ng_ref, w_ref, h_ref, mod_ref, o_ref, yb_scr,
                       *, tm, n_lat, k1):
    for c0 in range(0, k1, HEAD_DIM):
        sl = slice(c0, c0 + HEAD_DIM)
        o = of_ref[:, sl] + ob_ref[:, sl]
        o = o * lax.rsqrt(jnp.mean(o * o, axis=-1, keepdims=True) + NORM_EPS)
        yb_scr[:, sl] = (o * ng_ref[:, sl] * _silu(g_ref[:, sl])).astype(BF16)
    is_ctx = _is_ctx(pl.program_id(0), tm, n_lat)
    y = _dot(a1_ref[...], w_ref[:k1, :]) + _dot(yb_scr[...], w_ref[k1:, :])
    o_ref[...] = h_ref[...] + _mod_rows(mod_ref, 1, 2, is_ctx) * y


def _oproj_hgrn(a1, o_f, o_b, p32, gcol0, norm_g, w, h, mod, *, layer, tm, n_lat):
    rows, d = h.shape
    k1 = w.shape[1] // 2
    half = lambda i: (i, 0)
    return pl.pallas_call(
        functools.partial(_oproj_hgrn_kernel, tm=tm, n_lat=n_lat, k1=k1),
        grid=(rows // tm,),
        in_specs=[
            pl.BlockSpec((tm, k1), half),
            pl.BlockSpec((tm, k1), half),
            pl.BlockSpec((tm, k1), half),
            pl.BlockSpec((tm, k1), lambda i: (i, gcol0 // k1)),
            pl.BlockSpec((1, k1), lambda i: (0, 0)),
            pl.BlockSpec((None,) + w.shape[1:], lambda i: (layer, 0, 0)),
            pl.BlockSpec((tm, d), half),
            pl.BlockSpec((2, 9, d), lambda i: (0, 0, 0)),
        ],
        out_specs=pl.BlockSpec((tm, d), half),
        out_shape=jax.ShapeDtypeStruct((rows, d), F32),
        scratch_shapes=[pltpu.VMEM((tm, k1), BF16)],
        compiler_params=_params("arbitrary"),
        name="oproj_hgrn",
    )(a1, o_f, o_b, p32, norm_g, w, h, mod)


def _na_bias(rpb, n_rows):
    nh, n_a, n_b = rpb.shape
    nb = n_rows // NA_Q_ROWS
    w = jnp.full((nh, n_a, 128), MASKED, F32)
    w = w.at[..., :NA_COLS].set(rpb[..., NA_COLS - 1:]).at[..., 128 - (NA_COLS - 1):].set(rpb[..., :NA_COLS - 1])
    toep = jnp.tile(w, (1, 1, GRID_W))[..., :GRID_W * 127].reshape(nh, n_a, GRID_W, 127)[..., :GRID_W]
    toep = jnp.pad(toep.transpose(0, 2, 1, 3), ((0, 0), (0, 0), (NA_K_ROWS, NA_K_ROWS), (0, 0)),
                   constant_values=MASKED)

    q = np.arange(ATT_BLOCK)
    k = np.arange(NA_K_ROWS * GRID_W)
    dr, c = q // GRID_W, q % GRID_W
    kr, kc = k // GRID_W, k % GRID_W
    c0 = np.clip(c - NA_COLS // 2, 0, GRID_W - NA_COLS)
    col_ok = (kc[None, :] >= c0[:, None]) & (kc[None, :] < c0[:, None] + NA_COLS)
    tables, ok = [], []
    for b in (0, 1, nb - 1):
        r = b * NA_Q_ROWS + dr
        r0 = np.clip(r - NA_ROWS // 2, 0, n_rows - NA_ROWS)
        base = int(np.clip(b * NA_Q_ROWS - NA_ROWS // 2, 0, n_rows - NA_K_ROWS))
        kabs = base + kr
        ok.append((kabs[None, :] >= r0[:, None]) & (kabs[None, :] < r0[:, None] + NA_ROWS) & col_ok)
        per_row = []
        for j in range(NA_Q_ROWS):
            lo = base - (b * NA_Q_ROWS + j) + NA_ROWS - 1 + NA_K_ROWS
            per_row.append(toep[:, :, lo:lo + NA_K_ROWS].reshape(nh, GRID_W, NA_K_ROWS * GRID_W))
        tables.append(jnp.stack(per_row, axis=1).reshape(nh, ATT_BLOCK, NA_K_ROWS * GRID_W))
    return jnp.where(np.stack(ok)[:, None], jnp.stack(tables), MASKED)


def _na_kernel(q_ref, k_ref, v_ref, bias_ref, o_ref, *, n_lat, n_ctx, nb):
    b = pl.program_id(1)
    scale = HEAD_DIM ** -0.5
    nk = NA_K_ROWS * GRID_W
    start = pl.multiple_of(jnp.clip(b - 1, 0, nb - 3) * ATT_BLOCK, ATT_BLOCK)

    def one_head(hh, local):
        cols = slice(hh * HEAD_DIM, (hh + 1) * HEAD_DIM)
        q = q_ref[:, cols]
        s_ctx = _dot_nt(q, k_ref[n_lat:n_lat + n_ctx, cols]) * scale
        vc1 = _with_ones(v_ref[n_lat:n_lat + n_ctx, cols])
        if local:
            kl = k_ref[pl.ds(start, nk), cols]
            vl1 = _with_ones(v_ref[pl.ds(start, nk), cols])
            s_loc = _dot_nt(q, kl) * scale + bias_ref[0, hh]
            m = _row_max(s_loc, s_ctx)
            acc = _dot(jnp.exp(s_loc - m).astype(BF16), vl1) + _dot(jnp.exp(s_ctx - m).astype(BF16), vc1)
        else:
            acc = _dot(jnp.exp(s_ctx - _row_max(s_ctx)).astype(BF16), vc1)
        o_ref[:, cols] = (acc[:, :HEAD_DIM] / acc[:, HEAD_DIM:]).astype(o_ref.dtype)

    @pl.when(b < nb)
    def _():
        for hh in range(NA_HEADS_PER_STEP):
            one_head(hh, True)

    @pl.when(b >= nb)
    def _():
        for hh in range(NA_HEADS_PER_STEP):
            one_head(hh, False)


def _na_attention(p, bias, *, layer, n_lat, n_ctx, heads):
    rows = n_lat + n_ctx
    nb = n_lat // ATT_BLOCK
    nblk = rows // ATT_BLOCK
    hps = NA_HEADS_PER_STEP
    width = hps * HEAD_DIM
    groups = heads // hps

    def pattern(b):
        return jnp.where(b == 0, 0, jnp.where(b == nb - 1, 2, 1))

    return pl.pallas_call(
        functools.partial(_na_kernel, n_lat=n_lat, n_ctx=n_ctx, nb=nb),
        grid=(groups, nblk),
        in_specs=[
            pl.BlockSpec((ATT_BLOCK, width), lambda h, b: (b, h)),
            pl.BlockSpec((rows, width), lambda h, b: (0, groups + h)),
            pl.BlockSpec((rows, width), lambda h, b: (0, 2 * groups + h)),
            pl.BlockSpec((1, hps, ATT_BLOCK, NA_K_ROWS * GRID_W),
                         lambda h, b: (pattern(b), layer * groups + h, 0, 0)),
        ],
        out_specs=pl.BlockSpec((ATT_BLOCK, width), lambda h, b: (b, h)),
        out_shape=jax.ShapeDtypeStruct((rows, heads * HEAD_DIM), BF16),
        compiler_params=_params("arbitrary", "arbitrary"),
        name="na_attention",
    )(p, p, p, bias)


def _hgrn_levels():
    levels, m = [], 1
    while m < HG_CHUNK:
        levels.append(m)
        m *= 2
    return levels


def _hgrn_consts():
    c = HG_CHUNK
    levels = _hgrn_levels()
    fine = [m for m in levels if m < HG_ROW_LEVEL]
    idx = np.arange(c)
    t, j = idx[:, None], idx[None, :]
    n = np.zeros((2, 1 + len(fine), c, c), np.float32)
    msk = np.zeros((2, 1 + len(levels), c, c), np.float32)
    n[0, 0], n[1, 0] = j <= t, j >= t
    msk[0, 0] = msk[1, 0] = np.eye(c)
    for li, m in enumerate(levels):
        seg = idx // (2 * m)
        right = ((idx % (2 * m)) >= m)[:, None]
        last_left = (seg * 2 * m + m - 1)[:, None]
        first_right = last_left + 1
        if m < HG_ROW_LEVEL:
            n[0, 1 + li] = np.where(right, (j > last_left) & (j <= t), (j > t) & (j <= last_left))
            n[1, 1 + li] = np.where(right, (j >= first_right) & (j < t), (j >= t) & (j < first_right))
        same = seg[:, None] == seg[None, :]
        msk[0, 1 + li] = same & right & ~right.T
        msk[1, 1 + li] = same & ~right & right.T
    return n.reshape(2, -1, c), msk


def _hgrn_chunk(d, q, fx, v, lb, n_ref, msk_ref, s_scr, b_scr):
    c = HG_CHUNK
    t = jnp.exp(-jnp.abs(fx))
    r = 1.0 / (1.0 + t)
    log_sig = jnp.minimum(fx, 0.0) + jnp.log(r)
    sig_neg = jnp.where(fx >= 0.0, t * r, r)
    if lb is None:
        log_f, kk = log_sig, sig_neg
    else:
        la = jnp.log(lb)
        lc = jnp.log1p(-lb) + log_sig
        log_f = jnp.maximum(la, lc) + jnp.log(1.0 + jnp.exp(-jnp.abs(la - lc)))
        kk = (1.0 - lb) * sig_neg
    qf = q.astype(F32)

    hi = log_f.astype(BF16)
    mid = (log_f - hi.astype(F32)).astype(BF16)
    dd = _dot(n_ref[d], jnp.concatenate([hi, mid], axis=1))
    dd = dd[:, :HEAD_DIM] + dd[:, HEAD_DIM:]
    b = dd[:c]
    b_scr[d] = b

    attn = _dot_nt(q, kk.astype(BF16)) * msk_ref[d, 0]
    for li, m in enumerate(_hgrn_levels()):
        if m < HG_ROW_LEVEL:
            ex = dd[(1 + li) * c:(2 + li) * c]
        else:
            row = m - 1 + d
            ref = jnp.concatenate(
                [jnp.broadcast_to(b_scr[d, s + row:s + row + 1, :], (2 * m, HEAD_DIM)) for s in range(0, c, 2 * m)],
                axis=0)
            ex = -jnp.abs(b - ref)
        e = jnp.exp(ex)
        attn += _dot_nt((qf * e).astype(BF16), (kk * e).astype(BF16)) * msk_ref[d, 1 + li]

    total = b[c - 1:c] if d == 0 else b[0:1]
    state_t = s_scr[d]
    o = _dot_nt((qf * jnp.exp(b)).astype(BF16), state_t.astype(BF16)) + _dot(attn.astype(BF16), v)
    v_t = v.astype(F32).T.astype(BF16)
    s_scr[d] = jnp.exp(total) * state_t + _dot(v_t, (kk * jnp.exp(total - b)).astype(BF16))
    return o


def _hgrn_kernel(qf_ref, qb_ref, ff_ref, fb_ref, if_ref, ib_ref, lbl_ref, n_ref, msk_ref,
                 of_ref, ob_ref, s_scr, b_scr, *, layer_e):
    @pl.when(pl.program_id(1) == 0)
    def _():
        s_scr[...] = jnp.zeros_like(s_scr)

    refs = ((qf_ref, ff_ref, if_ref, of_ref), (qb_ref, fb_ref, ib_ref, ob_ref))
    lbs = [None, None]
    if layer_e > 0:
        for d in range(2):
            lg = lbl_ref[d]
            ex = jnp.exp(lg - jnp.max(lg, axis=0, keepdims=True))
            num = ex[1:2]
            for e in range(2, layer_e + 1):
                num = num + ex[e:e + 1]
            lbs[d] = num / jnp.sum(ex, axis=0, keepdims=True)

    n_sub = HG_STEP // HG_CHUNK
    for k in range(n_sub):
        for hh in range(HG_HEADS_PER_STEP):
            cols = slice(hh * HEAD_DIM, (hh + 1) * HEAD_DIM)
            for d, (q_ref, f_ref, i_ref, o_ref) in enumerate(refs):
                sub = k if d == 0 else n_sub - 1 - k
                rows = slice(sub * HG_CHUNK, (sub + 1) * HG_CHUNK)
                lb = None if lbs[d] is None else lbs[d][:, cols]
                o_ref[rows, cols] = _hgrn_chunk(d, q_ref[rows, cols], f_ref[rows, cols], i_ref[rows, cols], lb,
                                                n_ref, msk_ref, s_scr.at[hh], b_scr.at[hh, k])


def _hgrn_scan(p16, p32, lb_logits, consts, *, layer_e, n_lat, n_ctx, heads, qcol, icol):
    rows = n_lat + n_ctx
    n_mat, msk = consts
    lat_chunks = n_lat // HG_STEP
    qc, ic = qcol // HEAD_DIM, icol // HEAD_DIM

    def fwd(s):
        return jnp.where(s == 0, lat_chunks, s - 1)

    def bwd(s):
        return lat_chunks - s

    hps = HG_HEADS_PER_STEP
    width = hps * HEAD_DIM

    def spec(blk, col):
        return pl.BlockSpec((HG_STEP, width), lambda h, s: (blk(s), col // hps + h))

    out = jax.ShapeDtypeStruct((rows, heads * HEAD_DIM), F32)
    return pl.pallas_call(
        functools.partial(_hgrn_kernel, layer_e=layer_e),
        grid=(heads // hps, 1 + lat_chunks),
        in_specs=[
            spec(fwd, qc), spec(bwd, qc),
            spec(fwd, 0), spec(bwd, heads),
            spec(fwd, ic), spec(bwd, ic),
            pl.BlockSpec((2, lb_logits.shape[1], width), lambda h, s: (0, 0, h)),
            pl.BlockSpec(n_mat.shape, lambda h, s: (0, 0, 0)),
            pl.BlockSpec(msk.shape, lambda h, s: (0, 0, 0, 0)),
        ],
        out_specs=[spec(fwd, 0), spec(bwd, 0)],
        out_shape=[out, out],
        scratch_shapes=[pltpu.VMEM((hps, 2, HEAD_DIM, HEAD_DIM), F32),
                        pltpu.VMEM((hps, HG_STEP // HG_CHUNK, 2, HG_CHUNK, HEAD_DIM), F32)],
        compiler_params=_params("arbitrary", "arbitrary"),
        name="hgrn_scan",
    )(p16, p16, p32, p32, p16, p16, lb_logits, n_mat, msk)


def _hgrn_out_kernel(of_ref, ob_ref, g_ref, ng_ref, y_ref):
    o = of_ref[...] + ob_ref[...]
    o = o * lax.rsqrt(jnp.mean(o * o, axis=-1, keepdims=True) + NORM_EPS)
    y_ref[...] = (o * ng_ref[...] * _silu(g_ref[...])).astype(y_ref.dtype)


def _hgrn_out(o_f, o_b, p, norm_g, *, heads, gcol0, tm):
    rows = o_f.shape[0]
    return pl.pallas_call(
        _hgrn_out_kernel,
        grid=(rows // tm, heads),
        in_specs=[
            pl.BlockSpec((tm, HEAD_DIM), lambda i, h: (i, h)),
            pl.BlockSpec((tm, HEAD_DIM), lambda i, h: (i, h)),
            pl.BlockSpec((tm, HEAD_DIM), lambda i, h: (i, gcol0 // HEAD_DIM + h)),
            pl.BlockSpec((1, HEAD_DIM), lambda i, h: (0, h)),
        ],
        out_specs=pl.BlockSpec((tm, HEAD_DIM), lambda i, h: (i, h)),
        out_shape=jax.ShapeDtypeStruct((rows, heads * HEAD_DIM), BF16),
        compiler_params=_params("arbitrary", "arbitrary"),
        name="hgrn_out",
    )(o_f, o_b, p, norm_g)


def _swa_kernel(q_ref, k_ref, v_ref, sink_ref, bias_ref, o_ref, *, n_lat, n_ctx, nb, group, span):
    b = pl.program_id(1)
    scale = HEAD_DIM ** -0.5
    start = pl.multiple_of(jnp.clip(b * ATT_BLOCK - WINDOW, 0, n_lat - span), WINDOW)

    def one_kv_head(kv, local):
        kcols = slice(kv * HEAD_DIM, (kv + 1) * HEAD_DIM)
        kc = k_ref[n_lat:n_lat + n_ctx, kcols]
        vc1 = _with_ones(v_ref[n_lat:n_lat + n_ctx, kcols])
        if local:
            kl = k_ref[pl.ds(start, span), kcols]
            vl1 = _with_ones(v_ref[pl.ds(start, span), kcols])
        for g in range(group):
            qcols = slice((kv * group + g) * HEAD_DIM, (kv * group + g + 1) * HEAD_DIM)
            q = q_ref[:, qcols]
            sink = sink_ref[kv, g:g + 1, 0:1]
            s_ctx = _dot_nt(q, kc) * scale
            if local:
                s_loc = _dot_nt(q, kl) * scale + bias_ref[0]
                m = jnp.maximum(_row_max(s_ctx, s_loc), sink)
                acc = _dot(jnp.exp(s_ctx - m).astype(BF16), vc1) + _dot(jnp.exp(s_loc - m).astype(BF16), vl1)
            else:
                m = jnp.maximum(_row_max(s_ctx), sink)
                acc = _dot(jnp.exp(s_ctx - m).astype(BF16), vc1)
            den = acc[:, HEAD_DIM:] + jnp.exp(sink - m)
            o_ref[:, qcols] = (acc[:, :HEAD_DIM] / den).astype(o_ref.dtype)

    @pl.when(b < nb)
    def _():
        for kv in range(SWA_KV_PER_STEP):
            one_kv_head(kv, True)

    @pl.when(b >= nb)
    def _():
        for kv in range(SWA_KV_PER_STEP):
            one_kv_head(kv, False)


def _swa_bias(n_lat):
    span = ATT_BLOCK + 2 * WINDOW
    nb = n_lat // ATT_BLOCK
    i = np.arange(ATT_BLOCK)[:, None]
    j = np.arange(span)[None, :]
    tables = []
    for b in (0, 1, nb - 1):
        start = int(np.clip(b * ATT_BLOCK - WINDOW, 0, n_lat - span))
        rel = (start + j) - (b * ATT_BLOCK + i)
        tables.append(np.where(np.abs(rel) <= WINDOW, 0.0, MASKED))
    return jnp.asarray(np.stack(tables), F32)


def _swa_attention(p, sink, *, n_lat, n_ctx, heads, kv_heads):
    rows = n_lat + n_ctx
    group = heads // kv_heads
    nb = n_lat // ATT_BLOCK
    span = ATT_BLOCK + 2 * WINDOW
    kps = SWA_KV_PER_STEP
    sink_tab = jnp.broadcast_to(sink.astype(F32).reshape(kv_heads, group, 1), (kv_heads, group, HEAD_DIM))

    def pattern(b):
        return jnp.where(b == 0, 0, jnp.where(b == nb - 1, 2, 1))

    return pl.pallas_call(
        functools.partial(_swa_kernel, n_lat=n_lat, n_ctx=n_ctx, nb=nb, group=group, span=span),
        grid=(kv_heads // kps, rows // ATT_BLOCK),
        in_specs=[
            pl.BlockSpec((ATT_BLOCK, kps * group * HEAD_DIM), lambda k, b: (b, k)),
            pl.BlockSpec((rows, kps * HEAD_DIM), lambda k, b: (0, heads // kps + k)),
            pl.BlockSpec((rows, kps * HEAD_DIM), lambda k, b: (0, (heads + kv_heads) // kps + k)),
            pl.BlockSpec((kps, group, HEAD_DIM), lambda k, b: (k, 0, 0)),
            pl.BlockSpec((1, ATT_BLOCK, span), lambda k, b: (pattern(b), 0, 0)),
        ],
        out_specs=pl.BlockSpec((ATT_BLOCK, kps * group * HEAD_DIM), lambda k, b: (b, k)),
        out_shape=jax.ShapeDtypeStruct((rows, heads * HEAD_DIM), BF16),
        compiler_params=_params("arbitrary", "arbitrary"),
        name="swa_attention",
    )(p, p, p, sink_tab, _swa_bias(n_lat))


def kernel(x, c, ctx, c_ctx, w_mod, b_mod, norm_g, w_ff_in, w_ff_out, w_in_even, w_out_even,
           na_rpb, hg_lb_logits, hg_norm_g, w_qkv_odd, w_o_odd, sink_odd, final_norm_g):
    assert x.shape[0] == 1
    depth, d = w_mod.shape[0], x.shape[2]
    n_lat, n_ctx = x.shape[1], ctx.shape[1]
    rows = n_lat + n_ctx
    tm = 768
    assert rows % tm == 0 and n_lat % 1024 == 0 and n_ctx == ATT_BLOCK
    na_heads = na_rpb.shape[1]
    a_w = na_heads * HEAD_DIM
    hg_heads = hg_norm_g.shape[1] // HEAD_DIM
    heads = sink_odd.shape[1]
    kv_heads = (w_qkv_odd.shape[2] // HEAD_DIM - heads) // 2

    cc =jnp.zeros((8, d), F32).at[0].set(c[0]).at[1].set(c_ctx)
    mod_all = _modulation(cc, w_mod, b_mod)[:, :2].reshape(depth, 2, 9, d)

    cos, sin = _rope_tables(n_lat, n_ctx)
    na_bias = _na_bias(na_rpb.reshape((-1,) + na_rpb.shape[2:]), n_lat // GRID_W)
    hg_consts = _hgrn_consts()
    hg_consts = (jnp.asarray(hg_consts[0], BF16), jnp.asarray(hg_consts[1], F32))
    fg = final_norm_g.reshape(1, d)
    b_w = hg_heads * HEAD_DIM
    c_f, c_i, c_g = 3 * a_w + b_w, 3 * a_w + 3 * b_w, 3 * a_w + 4 * b_w
    w_in_even = jnp.concatenate([w_in_even[:, :, :c_f], w_in_even[:, :, c_i:c_g], w_in_even[:, :, c_f:c_i],
                                 w_in_even[:, :, c_g:]], axis=2).astype(BF16)
    w_out_even = w_out_even.astype(BF16)
    w_qkv_odd, w_o_odd = w_qkv_odd.astype(BF16), w_o_odd.astype(BF16)

    h = x[0]
    h_tail = jnp.concatenate([x[0, rows - tm:], ctx[0]], axis=0)
    for l in range(depth):
        mod = mod_all[l]
        last = l == depth - 1
        h = _ffn(h, mod, norm_g[l, 0].reshape(1, d), w_ff_in, w_ff_out, fg, layer=l, half=0,
                 sub=0, rows=rows, tm=tm, n_lat=n_lat, h_tail=h_tail if l == 0 else None)
        g1 = norm_g[l, 1].reshape(1, d)
        if l % 2 == 0:
            e = l // 2
            p16, p32 = _proj(h, mod, g1, w_in_even, layer=e, tm=tm, tn=1024, n_lat=n_lat, cols16=3 * a_w + 2 * b_w)
            ya = _na_attention(p16, na_bias, layer=e, n_lat=n_lat, n_ctx=n_ctx, heads=na_heads)
            o_f, o_b = _hgrn_scan(p16, p32, hg_lb_logits, hg_consts, layer_e=e, n_lat=n_lat, n_ctx=n_ctx,
                                  heads=hg_heads, qcol=3 * a_w, icol=3 * a_w + b_w)
            h = _oproj_hgrn(ya, o_f, o_b, p32, 2 * b_w, hg_norm_g[e].reshape(1, -1), w_out_even, h, mod,
                            layer=e, tm=tm // 2, n_lat=n_lat)
        else:
            o = l // 2
            p16 = _qkv_proj(h, mod, g1, w_qkv_odd, cos, sin, layer=o, tm=tm, n_lat=n_lat,
                            rope_cols=(heads + kv_heads) * HEAD_DIM)
            y = _swa_attention(p16, sink_odd[o], n_lat=n_lat, n_ctx=n_ctx, heads=heads, kv_heads=kv_heads)
            h = _oproj(y, y, 1, w_o_odd, h, mod, layer=o, tm=tm, n_lat=n_lat)
        g2 = norm_g[l, 2].reshape(1, d)
        if last:
            h = _ffn(h, mod, g2, w_ff_in, w_ff_out, fg, layer=l, half=1,
                     sub=2, rows=n_lat, tm=tm, n_lat=n_lat, final_norm=True)
        else:
            h = _ffn(h, mod, g2, w_ff_in, w_ff_out, fg, layer=l, half=1,
                     sub=2, rows=rows, tm=tm, n_lat=n_lat)
    return h[None]
```

```python
import functools

import numpy as np
import jax
import jax.numpy as jnp
from jax import lax
from jax.experimental import pallas as pl
from jax.experimental.pallas import tpu as pltpu

F32 = jnp.float32
BF16 = jnp.bfloat16

GRID_W = 64
NORM_EPS = 1e-6
ROPE_THETA = 10000.0
HEAD_DIM = 128
NA_ROWS = 8
NA_COLS = 16
NA_Q_ROWS = 4
NA_K_ROWS = NA_Q_ROWS + NA_ROWS
ATT_BLOCK = NA_Q_ROWS * GRID_W
NA_HEADS_PER_STEP = 4
SWA_KV_PER_STEP = 4
WINDOW = 128
HG_STEP = 256
HG_CHUNK = 128
HG_HEADS_PER_STEP = 4
ROW_CHUNK = 16
FFN_COLS = 512
FFN_TAIL_COLS = 256
QKV_SLAB = 512
FFN_OUT_COLS = 512
ROW_UNROLL = 4
HG_ROW_LEVEL = 8
MASKED = -1e30
VMEM_LIMIT = 56 * 1024 * 1024


def _params(*sem):
    return pltpu.CompilerParams(dimension_semantics=sem, vmem_limit_bytes=VMEM_LIMIT)


def _dot(a, b):
    return jnp.dot(a, b, preferred_element_type=F32)


def _dot_nt(a, b):
    return lax.dot_general(a, b, (((1,), (1,)), ((), ())), preferred_element_type=F32)


def _sigmoid(x):
    return 1.0 / (1.0 + jnp.exp(-x))


def _silu(x):
    return x * _sigmoid(x)


def _mod_kernel(c_ref, w_ref, b_ref, o_ref):
    s = _silu(c_ref[...]).astype(BF16)
    o_ref[0, 0] = _dot(s, w_ref[0].astype(BF16)) + b_ref[0]


def _modulation(cc, w_mod, b_mod):
    depth, d, n = w_mod.shape
    return pl.pallas_call(
        _mod_kernel,
        grid=(depth, n // d),
        in_specs=[
            pl.BlockSpec((8, d), lambda l, j: (0, 0)),
            pl.BlockSpec((1, d, d), lambda l, j: (l, 0, j)),
            pl.BlockSpec((1, 1, d), lambda l, j: (l, 0, j)),
        ],
        out_specs=pl.BlockSpec((1, 1, 8, d), lambda l, j: (l, j, 0, 0)),
        out_shape=jax.ShapeDtypeStruct((depth, n // d, 8, d), F32),
        compiler_params=_params("arbitrary", "arbitrary"),
        name="modulation",
    )(cc, w_mod, b_mod.reshape(depth, 1, n))


def _mod_rows(mod_ref, sub, k, is_ctx):
    r = 3 * sub + k
    return jnp.where(is_ctx, mod_ref[r, 1:2, :], mod_ref[r, 0:1, :])


def _row_max(*blocks):
    acc = None
    for s in blocks:
        for c0 in range(0, s.shape[1], HEAD_DIM):
            t = s[:, c0:c0 + HEAD_DIM]
            acc = t if acc is None else jnp.maximum(acc, t)
    return jnp.max(acc, axis=-1, keepdims=True)


def _with_ones(v):
    return jnp.concatenate([v, jnp.ones_like(v)], axis=1)


def _is_ctx(tile, tm, n_lat):
    row = tile * tm + lax.broadcasted_iota(jnp.int32, (tm, 1), 0)
    return row >= n_lat


def _row_chunks(row0, tm, n_lat, body):
    def step(c, carry):
        r0 = pl.multiple_of(c * ROW_CHUNK, ROW_CHUNK)
        body(pl.ds(r0, ROW_CHUNK), (row0 + r0 >= n_lat).astype(jnp.int32))
        return carry

    lax.fori_loop(0, tm // ROW_CHUNK, step, 0, unroll=ROW_UNROLL)


def _adaln_rows(h_ref, mod_ref, g_ref, u_scr, gs_scr, *, sub, row0, tm, n_lat):
    for which in range(2):
        gs_scr[which] = g_ref[...] * (1.0 + mod_ref[3 * sub + 1, which:which + 1, :])

    def body(rows, which):
        x = h_ref[rows, :]
        r = lax.rsqrt(jnp.mean(x * x, axis=-1, keepdims=True) + NORM_EPS)
        u_scr[rows, :] = ((x * r) * gs_scr[which] + mod_ref[3 * sub, pl.ds(which, 1), :]).astype(BF16)

    _row_chunks(row0, tm, n_lat, body)


def _ffn_kernel(h_ref, mod_ref, g_ref, wg_ref, wu_ref, wo_ref, fg_ref, *rest,
                sub, tm, row0, n_lat, final_norm, emit16, aliased):
    rest = rest[1:] if aliased else rest
    o_ref = rest[0]
    u_scr, gs_scr = rest[-2:]
    i, j = pl.program_id(0), pl.program_id(1)
    first_row = row0 + i * tm

    @pl.when(j == 0)
    def _():
        _adaln_rows(h_ref, mod_ref, g_ref, u_scr, gs_scr, sub=sub, row0=first_row, tm=tm, n_lat=n_lat)
        o_ref[...] = jnp.zeros_like(o_ref)

    if emit16:
        wg16_ref, wu16_ref, wo16_ref = rest[1:4]
        wg16_ref[...] = wg_ref[...].astype(BF16)
        wu16_ref[...] = wu_ref[...].astype(BF16)
        wo16_ref[...] = wo_ref[...].astype(BF16)
        wg_ref, wu_ref, wo_ref = wg16_ref, wu16_ref, wo16_ref
    u = u_scr[...]
    a = (_silu(_dot(u, wg_ref[...])) * _dot(u, wu_ref[...])).astype(BF16)
    d = o_ref.shape[1]
    for c0 in range(0, d, FFN_OUT_COLS):
        o_ref[:, c0:c0 + FFN_OUT_COLS] += _dot(a, wo_ref[:, c0:c0 + FFN_OUT_COLS])

    @pl.when(j == pl.num_programs(1) - 1)
    def _():
        def body(rows, which):
            h = h_ref[rows, :] + (0.5 * mod_ref[3 * sub + 2, pl.ds(which, 1), :]) * o_ref[rows, :]
            if final_norm:
                h = h * lax.rsqrt(jnp.mean(h * h, axis=-1, keepdims=True) + NORM_EPS) * fg_ref[...]
            o_ref[rows, :] = h

        _row_chunks(first_row, tm, n_lat, body)


def _ffn(h, mod, g, w_in, w_out, fg, *, layer, half, sub, rows, tm, n_lat, final_norm=False, h_tail=None):
    d = h.shape[1]
    f = w_out.shape[2]
    tail = rows % tm or tm
    n_main = (rows - tail) // tm
    assert (rows - tail) % tail == 0
    tail_src, tail_blk = (h, (rows - tail) // tail) if h_tail is None else (h_tail, 0)
    common = dict(sub=sub, n_lat=n_lat, final_norm=final_norm)
    small = [
        pl.BlockSpec((9, 8, d), lambda i, j: (0, 0, 0)),
        pl.BlockSpec((1, d), lambda i, j: (0, 0)),
    ]
    fg_spec = pl.BlockSpec((1, d), lambda i, j: (0, 0))

    tf, t0 = FFN_TAIL_COLS, (rows - tail) // tail
    nf = f // tf
    out, wg16, wu16, wo16 = pl.pallas_call(
        functools.partial(_ffn_kernel, tm=tail, row0=rows - tail, emit16=True, aliased=False, **common),
        grid=(1, nf),
        in_specs=[pl.BlockSpec((tail, d), lambda i, j: (tail_blk, 0))] + small + [
            pl.BlockSpec((None, None, d, tf), lambda i, j: (layer, half, 0, j)),
            pl.BlockSpec((None, None, d, tf), lambda i, j: (layer, half, 0, nf + j)),
            pl.BlockSpec((None, None, tf, d), lambda i, j: (layer, half, j, 0)),
            fg_spec,
        ],
        out_specs=[
            pl.BlockSpec((tail, d), lambda i, j: (t0, 0)),
            pl.BlockSpec((d, tf), lambda i, j: (0, j)),
            pl.BlockSpec((d, tf), lambda i, j: (0, j)),
            pl.BlockSpec((tf, d), lambda i, j: (j, 0)),
        ],
        out_shape=[
            jax.ShapeDtypeStruct((rows, d), F32),
            jax.ShapeDtypeStruct((d, f), BF16),
            jax.ShapeDtypeStruct((d, f), BF16),
            jax.ShapeDtypeStruct((f, d), BF16),
        ],
        scratch_shapes=[pltpu.VMEM((tail, d), BF16), pltpu.VMEM((2, 1, d), F32)],
        compiler_params=_params("arbitrary", "arbitrary"),
        name="ffn_tail",
    )(tail_src, mod, g, w_in, w_in, w_out, fg)

    tf = FFN_COLS
    return pl.pallas_call(
        functools.partial(_ffn_kernel, tm=tm, row0=0, emit16=False, aliased=True, **common),
        grid=(n_main, f // tf),
        in_specs=[pl.BlockSpec((tm, d), lambda i, j: (i, 0))] + small + [
            pl.BlockSpec((d, tf), lambda i, j: (0, j)),
            pl.BlockSpec((d, tf), lambda i, j: (0, j)),
            pl.BlockSpec((tf, d), lambda i, j: (j, 0)),
            fg_spec,
            pl.BlockSpec(memory_space=pl.ANY),
        ],
        out_specs=pl.BlockSpec((tm, d), lambda i, j: (i, 0)),
        out_shape=jax.ShapeDtypeStruct((rows, d), F32),
        input_output_aliases={7: 0},
        scratch_shapes=[pltpu.VMEM((tm, d), BF16), pltpu.VMEM((2, 1, d), F32)],
        compiler_params=_params("arbitrary", "arbitrary"),
        name="ffn",
    )(h, mod, g, wg16, wu16, wo16, fg, out)


def _rope(x, cos, sin):
    lane = lax.broadcasted_iota(jnp.int32, x.shape, 1)
    first = (lane % 64) < 32
    swapped = jnp.where(first, pltpu.roll(x, 96, 1), pltpu.roll(x, 32, 1))
    return x * cos + swapped * sin


def _proj_kernel(h_ref, mod_ref, g_ref, w_ref, o16_ref, o32_ref, u_scr, gs_scr, *, tm, n_lat, n16):
    i, j = pl.program_id(0), pl.program_id(1)

    @pl.when(j == 0)
    def _():
        _adaln_rows(h_ref, mod_ref, g_ref, u_scr, gs_scr, sub=1, row0=i * tm, tm=tm, n_lat=n_lat)

    y = _dot(u_scr[...], w_ref[...])

    @pl.when(j < n16)
    def _():
        o16_ref[...] = y.astype(BF16)

    @pl.when(j >= n16)
    def _():
        o32_ref[...] = y


def _proj(h, mod, g, w, *, layer, tm, tn, n_lat, cols16):
    rows, d = h.shape
    n = w.shape[2]
    n16 = cols16 // tn
    return pl.pallas_call(
        functools.partial(_proj_kernel, tm=tm, n_lat=n_lat, n16=n16),
        grid=(rows // tm, n // tn),
        in_specs=[
            pl.BlockSpec((tm, d), lambda i, j: (i, 0)),
            pl.BlockSpec((9, 8, d), lambda i, j: (0, 0, 0)),
            pl.BlockSpec((1, d), lambda i, j: (0, 0)),
            pl.BlockSpec((None, d, tn), lambda i, j: (layer, 0, j)),
        ],
        out_specs=[
            pl.BlockSpec((tm, tn), lambda i, j: (i, jnp.minimum(j, n16 - 1))),
            pl.BlockSpec((tm, tn), lambda i, j: (i, jnp.maximum(j - n16, 0))),
        ],
        out_shape=[jax.ShapeDtypeStruct((rows, cols16), BF16), jax.ShapeDtypeStruct((rows, n - cols16), F32)],
        scratch_shapes=[pltpu.VMEM((tm, d), BF16), pltpu.VMEM((2, 1, d), F32)],
        compiler_params=_params("arbitrary", "arbitrary"),
        name="proj",
    )(h, mod, g, w)


def _qkv_kernel(h_ref, mod_ref, g_ref, w_ref, cos_ref, sin_ref, o_ref, u_scr, gs_scr, *, tm, n_lat, rope_cols):
    _adaln_rows(h_ref, mod_ref, g_ref, u_scr, gs_scr, sub=1, row0=pl.program_id(0) * tm, tm=tm, n_lat=n_lat)
    u = u_scr[...]
    cos, sin = cos_ref[...], sin_ref[...]
    for c0 in range(0, o_ref.shape[1], QKV_SLAB):
        y = _dot(u, w_ref[:, c0:c0 + QKV_SLAB])
        for hd in range(0, QKV_SLAB, HEAD_DIM):
            x = y[:, hd:hd + HEAD_DIM]
            if c0 + hd < rope_cols:
                x = _rope(x, cos, sin)
            o_ref[:, c0 + hd:c0 + hd + HEAD_DIM] = x.astype(BF16)


def _qkv_proj(h, mod, g, w, cos, sin, *, layer, tm, n_lat, rope_cols):
    rows, d = h.shape
    n = w.shape[2]
    return pl.pallas_call(
        functools.partial(_qkv_kernel, tm=tm, n_lat=n_lat, rope_cols=rope_cols),
        grid=(rows // tm,),
        in_specs=[
            pl.BlockSpec((tm, d), lambda i: (i, 0)),
            pl.BlockSpec((9, 8, d), lambda i: (0, 0, 0)),
            pl.BlockSpec((1, d), lambda i: (0, 0)),
            pl.BlockSpec((None, d, n), lambda i: (layer, 0, 0), pipeline_mode=pl.Buffered(1)),
            pl.BlockSpec((tm, HEAD_DIM), lambda i: (i, 0)),
            pl.BlockSpec((tm, HEAD_DIM), lambda i: (i, 0)),
        ],
        out_specs=pl.BlockSpec((tm, n), lambda i: (i, 0)),
        out_shape=jax.ShapeDtypeStruct((rows, n), BF16),
        scratch_shapes=[pltpu.VMEM((tm, d), BF16), pltpu.VMEM((2, 1, d), F32)],
        compiler_params=_params("arbitrary"),
        name="qkv_proj",
    )(h, mod, g, w, cos, sin)


def _rope_tables(n_lat, n_ctx):
    t = np.arange(n_lat)
    inv = (ROPE_THETA ** (-np.arange(0, 64, 2, dtype=np.float32) / 64)).astype(np.float32)
    ang_r = (t // GRID_W).astype(np.float32)[:, None] * inv[None, :]
    ang_c = (t % GRID_W).astype(np.float32)[:, None] * inv[None, :]
    cr, sr, cc, sc = np.cos(ang_r), np.sin(ang_r), np.cos(ang_c), np.sin(ang_c)
    cos = np.concatenate([cr, cr, cc, cc], axis=1)
    sin = np.concatenate([-sr, sr, -sc, sc], axis=1)
    cos = np.concatenate([cos, np.ones((n_ctx, HEAD_DIM), np.float32)], axis=0)
    sin = np.concatenate([sin, np.zeros((n_ctx, HEAD_DIM), np.float32)], axis=0)
    return jnp.asarray(cos, F32), jnp.asarray(sin, F32)


def _oproj_kernel(a1_ref, a2_ref, w_ref, h_ref, mod_ref, o_ref, *, tm, n_lat, k1):
    is_ctx = _is_ctx(pl.program_id(0), tm, n_lat)
    y = _dot(a1_ref[...], w_ref[:k1, :]) + _dot(a2_ref[...], w_ref[k1:, :])
    o_ref[...] = h_ref[...] + _mod_rows(mod_ref, 1, 2, is_ctx) * y


def _oproj(a1, a2, col2, w, h, mod, *, layer, tm, n_lat):
    rows, d = h.shape
    k1 = w.shape[1] // 2
    return pl.pallas_call(
        functools.partial(_oproj_kernel, tm=tm, n_lat=n_lat, k1=k1),
        grid=(rows // tm,),
        in_specs=[
            pl.BlockSpec((tm, k1), lambda i: (i, 0)),
            pl.BlockSpec((tm, k1), lambda i: (i, col2)),
            pl.BlockSpec((None,) + w.shape[1:], lambda i: (layer, 0, 0)),
            pl.BlockSpec((tm, d), lambda i: (i, 0)),
            pl.BlockSpec((9, 8, d), lambda i: (0, 0, 0)),
        ],
        out_specs=pl.BlockSpec((tm, d), lambda i: (i, 0)),
        out_shape=jax.ShapeDtypeStruct((rows, d), F32),
        compiler_params=_params("arbitrary"),
        name="oproj",
    )(a1, a2, w, h, mod)


def _oproj_hgrn_kernel(a1_ref, of_ref, ob_ref, g_ref, ng_ref, w_ref, h_ref, mod_ref, o_ref, yb_scr,
                       *, tm, n_lat, k1):
    for c0 in range(0, k1, HEAD_DIM):
        sl = slice(c0, c0 + HEAD_DIM)
        o = of_ref[:, sl] + ob_ref[:, sl]
        o = o * lax.rsqrt(jnp.mean(o * o, axis=-1, keepdims=True) + NORM_EPS)
        yb_scr[:, sl] = (o * ng_ref[:, sl] * _silu(g_ref[:, sl])).astype(BF16)
    is_ctx = _is_ctx(pl.program_id(0), tm, n_lat)
    y = _dot(a1_ref[...], w_ref[:k1, :]) + _dot(yb_scr[...], w_ref[k1:, :])
    o_ref[...] = h_ref[...] + _mod_rows(mod_ref, 1, 2, is_ctx) * y


def _oproj_hgrn(a1, o_f, o_b, p32, gcol0, norm_g, w, h, mod, *, layer, tm, n_lat):
    rows, d = h.shape
    k1 = w.shape[1] // 2
    half = lambda i: (i, 0)
    return pl.pallas_call(
        functools.partial(_oproj_hgrn_kernel, tm=tm, n_lat=n_lat, k1=k1),
        grid=(rows // tm,),
        in_specs=[
            pl.BlockSpec((tm, k1), half),
            pl.BlockSpec((tm, k1), half),
            pl.BlockSpec((tm, k1), half),
            pl.BlockSpec((tm, k1), lambda i: (i, gcol0 // k1)),
            pl.BlockSpec((1, k1), lambda i: (0, 0)),
            pl.BlockSpec((None,) + w.shape[1:], lambda i: (layer, 0, 0)),
            pl.BlockSpec((tm, d), half),
            pl.BlockSpec((9, 8, d), lambda i: (0, 0, 0)),
        ],
        out_specs=pl.BlockSpec((tm, d), half),
        out_shape=jax.ShapeDtypeStruct((rows, d), F32),
        scratch_shapes=[pltpu.VMEM((tm, k1), BF16)],
        compiler_params=_params("arbitrary"),
        name="oproj_hgrn",
    )(a1, o_f, o_b, p32, norm_g, w, h, mod)


def _na_bias(rpb, n_rows):
    nh, n_a, n_b = rpb.shape
    nb = n_rows // NA_Q_ROWS
    w = jnp.full((nh, n_a, 128), MASKED, F32)
    w = w.at[..., :NA_COLS].set(rpb[..., NA_COLS - 1:]).at[..., 128 - (NA_COLS - 1):].set(rpb[..., :NA_COLS - 1])
    toep = jnp.tile(w, (1, 1, GRID_W))[..., :GRID_W * 127].reshape(nh, n_a, GRID_W, 127)[..., :GRID_W]
    toep = jnp.pad(toep.transpose(0, 2, 1, 3), ((0, 0), (0, 0), (NA_K_ROWS, NA_K_ROWS), (0, 0)),
                   constant_values=MASKED).reshape(nh, GRID_W, -1)

    q = np.arange(ATT_BLOCK)
    k = np.arange(NA_K_ROWS * GRID_W)
    dr, c = q // GRID_W, q % GRID_W
    kr, kc = k // GRID_W, k % GRID_W
    c0 = np.clip(c - NA_COLS // 2, 0, GRID_W - NA_COLS)
    col_ok = (kc[None, :] >= c0[:, None]) & (kc[None, :] < c0[:, None] + NA_COLS)
    tables, ok = [], []
    for b in (0, 1, nb - 1):
        r = b * NA_Q_ROWS + dr
        r0 = np.clip(r - NA_ROWS // 2, 0, n_rows - NA_ROWS)
        base = int(np.clip(b * NA_Q_ROWS - NA_ROWS // 2, 0, n_rows - NA_K_ROWS))
        kabs = base + kr
        ok.append((kabs[None, :] >= r0[:, None]) & (kabs[None, :] < r0[:, None] + NA_ROWS) & col_ok)
        per_row = []
        for j in range(NA_Q_ROWS):
            lo = base - (b * NA_Q_ROWS + j) + NA_ROWS - 1 + NA_K_ROWS
            per_row.append(toep[:, :, lo * GRID_W:(lo + NA_K_ROWS) * GRID_W])
        tables.append(jnp.concatenate(per_row, axis=1))
    return jnp.where(np.stack(ok)[:, None], jnp.stack(tables), MASKED)


def _na_kernel(q_ref, k_ref, v_ref, bias_ref, o_ref, *, n_lat, n_ctx, nb):
    b = pl.program_id(1)
    scale = HEAD_DIM ** -0.5
    nk = NA_K_ROWS * GRID_W
    start = pl.multiple_of(jnp.clip(b - 1, 0, nb - 3) * ATT_BLOCK, ATT_BLOCK)

    def one_head(hh, local):
        cols = slice(hh * HEAD_DIM, (hh + 1) * HEAD_DIM)
        q = q_ref[:, cols]
        s_ctx = _dot_nt(q, k_ref[n_lat:n_lat + n_ctx, cols]) * scale
        vc1 = _with_ones(v_ref[n_lat:n_lat + n_ctx, cols])
        if local:
            kl = k_ref[pl.ds(start, nk), cols]
            vl1 = _with_ones(v_ref[pl.ds(start, nk), cols])
            s_loc = _dot_nt(q, kl) * scale + bias_ref[0, hh]
            m = _row_max(s_loc, s_ctx)
            acc = _dot(jnp.exp(s_loc - m).astype(BF16), vl1) + _dot(jnp.exp(s_ctx - m).astype(BF16), vc1)
        else:
            acc = _dot(jnp.exp(s_ctx - _row_max(s_ctx)).astype(BF16), vc1)
        o_ref[:, cols] = (acc[:, :HEAD_DIM] / acc[:, HEAD_DIM:]).astype(o_ref.dtype)

    @pl.when(b < nb)
    def _():
        for hh in range(NA_HEADS_PER_STEP):
            one_head(hh, True)

    @pl.when(b >= nb)
    def _():
        for hh in range(NA_HEADS_PER_STEP):
            one_head(hh, False)


def _na_attention(p, bias, *, layer, n_lat, n_ctx, heads):
    rows = n_lat + n_ctx
    nb = n_lat // ATT_BLOCK
    nblk = rows // ATT_BLOCK
    hps = NA_HEADS_PER_STEP
    width = hps * HEAD_DIM
    groups = heads // hps

    def pattern(b):
        return jnp.where(b == 0, 0, jnp.where(b == nb - 1, 2, 1))

    return pl.pallas_call(
        functools.partial(_na_kernel, n_lat=n_lat, n_ctx=n_ctx, nb=nb),
        grid=(groups, nblk),
        in_specs=[
            pl.BlockSpec((ATT_BLOCK, width), lambda h, b: (b, h)),
            pl.BlockSpec((rows, width), lambda h, b: (0, groups + h)),
            pl.BlockSpec((rows, width), lambda h, b: (0, 2 * groups + h)),
            pl.BlockSpec((1, hps, ATT_BLOCK, NA_K_ROWS * GRID_W),
                         lambda h, b: (pattern(b), layer * groups + h, 0, 0)),
        ],
        out_specs=pl.BlockSpec((ATT_BLOCK, width), lambda h, b: (b, h)),
        out_shape=jax.ShapeDtypeStruct((rows, heads * HEAD_DIM), BF16),
        compiler_params=_params("arbitrary", "arbitrary"),
        name="na_attention",
    )(p, p, p, bias)


def _hgrn_levels():
    levels, m = [], 1
    while m < HG_CHUNK:
        levels.append(m)
        m *= 2
    return levels


def _hgrn_consts():
    c = HG_CHUNK
    levels = _hgrn_levels()
    fine = [m for m in levels if m < HG_ROW_LEVEL]
    idx = np.arange(c)
    t, j = idx[:, None], idx[None, :]
    n = np.zeros((2, 1 + len(fine), c, c), np.float32)
    msk = np.zeros((2, 1 + len(levels), c, c), np.float32)
    n[0, 0], n[1, 0] = j <= t, j >= t
    msk[0, 0] = msk[1, 0] = np.eye(c)
    for li, m in enumerate(levels):
        seg = idx // (2 * m)
        right = ((idx % (2 * m)) >= m)[:, None]
        last_left = (seg * 2 * m + m - 1)[:, None]
        first_right = last_left + 1
        if m < HG_ROW_LEVEL:
            n[0, 1 + li] = np.where(right, (j > last_left) & (j <= t), (j > t) & (j <= last_left))
            n[1, 1 + li] = np.where(right, (j >= first_right) & (j < t), (j >= t) & (j < first_right))
        same = seg[:, None] == seg[None, :]
        msk[0, 1 + li] = same & right & ~right.T
        msk[1, 1 + li] = same & ~right & right.T
    return n.reshape(2, -1, c), msk


def _hgrn_chunk(d, q, fx, v, lb, n_ref, msk_ref, s_scr, b_scr):
    c = HG_CHUNK
    t = jnp.exp(-jnp.abs(fx))
    r = 1.0 / (1.0 + t)
    log_sig = jnp.minimum(fx, 0.0) + jnp.log(r)
    sig_neg = jnp.where(fx >= 0.0, t * r, r)
    if lb is None:
        log_f, kk = log_sig, sig_neg
    else:
        la = jnp.log(lb)
        lc = jnp.log1p(-lb) + log_sig
        log_f = jnp.maximum(la, lc) + jnp.log(1.0 + jnp.exp(-jnp.abs(la - lc)))
        kk = (1.0 - lb) * sig_neg
    qf = q.astype(F32)

    hi = log_f.astype(BF16)
    mid = (log_f - hi.astype(F32)).astype(BF16)
    dd = _dot(n_ref[d], jnp.concatenate([hi, mid], axis=1))
    dd = dd[:, :HEAD_DIM] + dd[:, HEAD_DIM:]
    b = dd[:c]
    b_scr[d] = b

    attn = _dot_nt(q, kk.astype(BF16)) * msk_ref[d, 0]
    for li, m in enumerate(_hgrn_levels()):
        if m < HG_ROW_LEVEL:
            ex = dd[(1 + li) * c:(2 + li) * c]
        else:
            row = m - 1 + d
            ref = jnp.concatenate(
                [jnp.broadcast_to(b_scr[d, s + row:s + row + 1, :], (2 * m, HEAD_DIM)) for s in range(0, c, 2 * m)],
                axis=0)
            ex = -jnp.abs(b - ref)
        e = jnp.exp(ex)
        attn += _dot_nt((qf * e).astype(BF16), (kk * e).astype(BF16)) * msk_ref[d, 1 + li]

    total = b[c - 1:c] if d == 0 else b[0:1]
    state_t = s_scr[d]
    o = _dot_nt((qf * jnp.exp(b)).astype(BF16), state_t.astype(BF16)) + _dot(attn.astype(BF16), v)
    v_t = v.astype(F32).T.astype(BF16)
    s_scr[d] = jnp.exp(total) * state_t + _dot(v_t, (kk * jnp.exp(total - b)).astype(BF16))
    return o


def _hgrn_kernel(qf_ref, qb_ref, ff_ref, fb_ref, if_ref, ib_ref, lbl_ref, n_ref, msk_ref,
                 of_ref, ob_ref, s_scr, b_scr, *, layer_e):
    @pl.when(pl.program_id(1) == 0)
    def _():
        s_scr[...] = jnp.zeros_like(s_scr)

    refs = ((qf_ref, ff_ref, if_ref, of_ref), (qb_ref, fb_ref, ib_ref, ob_ref))
    lbs = [None, None]
    if layer_e > 0:
        for d in range(2):
            lg = lbl_ref[d]
            ex = jnp.exp(lg - jnp.max(lg, axis=0, keepdims=True))
            num = ex[1:2]
            for e in range(2, layer_e + 1):
                num = num + ex[e:e + 1]
            lbs[d] = num / jnp.sum(ex, axis=0, keepdims=True)

    n_sub = HG_STEP // HG_CHUNK
    for k in range(n_sub):
        for hh in range(HG_HEADS_PER_STEP):
            cols = slice(hh * HEAD_DIM, (hh + 1) * HEAD_DIM)
            for d, (q_ref, f_ref, i_ref, o_ref) in enumerate(refs):
                sub = k if d == 0 else n_sub - 1 - k
                rows = slice(sub * HG_CHUNK, (sub + 1) * HG_CHUNK)
                lb = None if lbs[d] is None else lbs[d][:, cols]
                o_ref[rows, cols] = _hgrn_chunk(d, q_ref[rows, cols], f_ref[rows, cols], i_ref[rows, cols], lb,
                                                n_ref, msk_ref, s_scr.at[hh], b_scr.at[hh, k])


def _hgrn_scan(p16, p32, lb_logits, consts, *, layer_e, n_lat, n_ctx, heads, qcol, icol):
    rows = n_lat + n_ctx
    n_mat, msk = consts
    lat_chunks = n_lat // HG_STEP
    qc, ic = qcol // HEAD_DIM, icol // HEAD_DIM

    def fwd(s):
        return jnp.where(s == 0, lat_chunks, s - 1)

    def bwd(s):
        return lat_chunks - s

    hps = HG_HEADS_PER_STEP
    width = hps * HEAD_DIM

    def spec(blk, col):
        return pl.BlockSpec((HG_STEP, width), lambda h, s: (blk(s), col // hps + h))

    out = jax.ShapeDtypeStruct((rows, heads * HEAD_DIM), F32)
    return pl.pallas_call(
        functools.partial(_hgrn_kernel, layer_e=layer_e),
        grid=(heads // hps, 1 + lat_chunks),
        in_specs=[
            spec(fwd, qc), spec(bwd, qc),
            spec(fwd, 0), spec(bwd, heads),
            spec(fwd, ic), spec(bwd, ic),
            pl.BlockSpec((2, lb_logits.shape[1], width), lambda h, s: (0, 0, h)),
            pl.BlockSpec(n_mat.shape, lambda h, s: (0, 0, 0)),
            pl.BlockSpec(msk.shape, lambda h, s: (0, 0, 0, 0)),
        ],
        out_specs=[spec(fwd, 0), spec(bwd, 0)],
        out_shape=[out, out],
        scratch_shapes=[pltpu.VMEM((hps, 2, HEAD_DIM, HEAD_DIM), F32),
                        pltpu.VMEM((hps, HG_STEP // HG_CHUNK, 2, HG_CHUNK, HEAD_DIM), F32)],
        compiler_params=_params("arbitrary", "arbitrary"),
        name="hgrn_scan",
    )(p16, p16, p32, p32, p16, p16, lb_logits, n_mat, msk)


def _swa_kernel(q_ref, k_ref, v_ref, sink_ref, bias_ref, o_ref, *, n_lat, n_ctx, nb, group, span):
    b = pl.program_id(1)
    scale = HEAD_DIM ** -0.5
    start = pl.multiple_of(jnp.clip(b * ATT_BLOCK - WINDOW, 0, n_lat - span), WINDOW)

    def one_kv_head(kv, local):
        kcols = slice(kv * HEAD_DIM, (kv + 1) * HEAD_DIM)
        kc = k_ref[n_lat:n_lat + n_ctx, kcols]
        vc1 = _with_ones(v_ref[n_lat:n_lat + n_ctx, kcols])
        if local:
            kl = k_ref[pl.ds(start, span), kcols]
            vl1 = _with_ones(v_ref[pl.ds(start, span), kcols])
        for g in range(group):
            qcols = slice((kv * group + g) * HEAD_DIM, (kv * group + g + 1) * HEAD_DIM)
            q = q_ref[:, qcols]
            sink = sink_ref[kv, g:g + 1, 0:1]
            s_ctx = _dot_nt(q, kc) * scale
            if local:
                s_loc = _dot_nt(q, kl) * scale + bias_ref[0]
                m = jnp.maximum(_row_max(s_ctx, s_loc), sink)
                acc = _dot(jnp.exp(s_ctx - m).astype(BF16), vc1) + _dot(jnp.exp(s_loc - m).astype(BF16), vl1)
            else:
                m = jnp.maximum(_row_max(s_ctx), sink)
                acc = _dot(jnp.exp(s_ctx - m).astype(BF16), vc1)
            den = acc[:, HEAD_DIM:] + jnp.exp(sink - m)
            o_ref[:, qcols] = (acc[:, :HEAD_DIM] / den).astype(o_ref.dtype)

    @pl.when(b < nb)
    def _():
        for kv in range(SWA_KV_PER_STEP):
            one_kv_head(kv, True)

    @pl.when(b >= nb)
    def _():
        for kv in range(SWA_KV_PER_STEP):
            one_kv_head(kv, False)


def _swa_bias(n_lat):
    span = ATT_BLOCK + 2 * WINDOW
    nb = n_lat // ATT_BLOCK
    i = np.arange(ATT_BLOCK)[:, None]
    j = np.arange(span)[None, :]
    tables = []
    for b in (0, 1, nb - 1):
        start = int(np.clip(b * ATT_BLOCK - WINDOW, 0, n_lat - span))
        rel = (start + j) - (b * ATT_BLOCK + i)
        tables.append(np.where(np.abs(rel) <= WINDOW, 0.0, MASKED))
    return jnp.asarray(np.stack(tables), F32)


def _swa_attention(p, sink, *, n_lat, n_ctx, heads, kv_heads):
    rows = n_lat + n_ctx
    group = heads // kv_heads
    nb = n_lat // ATT_BLOCK
    span = ATT_BLOCK + 2 * WINDOW
    kps = SWA_KV_PER_STEP
    sink_tab = jnp.broadcast_to(sink.astype(F32).reshape(kv_heads, group, 1), (kv_heads, group, HEAD_DIM))

    def pattern(b):
        return jnp.where(b == 0, 0, jnp.where(b == nb - 1, 2, 1))

    return pl.pallas_call(
        functools.partial(_swa_kernel, n_lat=n_lat, n_ctx=n_ctx, nb=nb, group=group, span=span),
        grid=(kv_heads // kps, rows // ATT_BLOCK),
        in_specs=[
            pl.BlockSpec((ATT_BLOCK, kps * group * HEAD_DIM), lambda k, b: (b, k)),
            pl.BlockSpec((rows, kps * HEAD_DIM), lambda k, b: (0, heads // kps + k)),
            pl.BlockSpec((rows, kps * HEAD_DIM), lambda k, b: (0, (heads + kv_heads) // kps + k)),
            pl.BlockSpec((kps, group, HEAD_DIM), lambda k, b: (k, 0, 0)),
            pl.BlockSpec((1, ATT_BLOCK, span), lambda k, b: (pattern(b), 0, 0)),
        ],
        out_specs=pl.BlockSpec((ATT_BLOCK, kps * group * HEAD_DIM), lambda k, b: (b, k)),
        out_shape=jax.ShapeDtypeStruct((rows, heads * HEAD_DIM), BF16),
        compiler_params=_params("arbitrary", "arbitrary"),
        name="swa_attention",
    )(p, p, p, sink_tab, _swa_bias(n_lat))


def kernel(x, c, ctx, c_ctx, w_mod, b_mod, norm_g, w_ff_in, w_ff_out, w_in_even, w_out_even,
           na_rpb, hg_lb_logits, hg_norm_g, w_qkv_odd, w_o_odd, sink_odd, final_norm_g):
    assert x.shape[0] == 1
    depth, d = w_mod.shape[0], x.shape[2]
    n_lat, n_ctx = x.shape[1], ctx.shape[1]
    rows = n_lat + n_ctx
    tm = 768
    assert rows % tm == 0 and n_lat % 1024 == 0 and n_ctx == ATT_BLOCK
    na_heads = na_rpb.shape[1]
    a_w = na_heads * HEAD_DIM
    hg_heads = hg_norm_g.shape[1] // HEAD_DIM
    heads = sink_odd.shape[1]
    kv_heads = (w_qkv_odd.shape[2] // HEAD_DIM - heads) // 2

    cc =jnp.zeros((8, d), F32).at[0].set(c[0]).at[1].set(c_ctx)
    mod_all = _modulation(cc, w_mod, b_mod)

    cos, sin = _rope_tables(n_lat, n_ctx)
    na_bias = _na_bias(na_rpb.reshape((-1,) + na_rpb.shape[2:]), n_lat // GRID_W)
    hg_consts = _hgrn_consts()
    hg_consts = (jnp.asarray(hg_consts[0], BF16), jnp.asarray(hg_consts[1], F32))
    fg = final_norm_g.reshape(1, d)
    b_w = hg_heads * HEAD_DIM
    c_f, c_i, c_g = 3 * a_w + b_w, 3 * a_w + 3 * b_w, 3 * a_w + 4 * b_w
    w_in_even = jnp.concatenate([w_in_even[:, :, :c_f], w_in_even[:, :, c_i:c_g], w_in_even[:, :, c_f:c_i],
                                 w_in_even[:, :, c_g:]], axis=2).astype(BF16)
    w_out_even = w_out_even.astype(BF16)
    w_qkv_odd, w_o_odd = w_qkv_odd.astype(BF16), w_o_odd.astype(BF16)

    h = x[0]
    h_tail = jnp.concatenate([x[0, rows - tm:], ctx[0]], axis=0)
    for l in range(depth):
        mod = mod_all[l]
        last = l == depth - 1
        h = _ffn(h, mod, norm_g[l, 0].reshape(1, d), w_ff_in, w_ff_out, fg, layer=l, half=0,
                 sub=0, rows=rows, tm=tm, n_lat=n_lat, h_tail=h_tail if l == 0 else None)
        g1 = norm_g[l, 1].reshape(1, d)
        if l % 2 == 0:
            e = l // 2
            p16, p32 = _proj(h, mod, g1, w_in_even, layer=e, tm=tm, tn=1024, n_lat=n_lat, cols16=3 * a_w + 2 * b_w)
            ya = _na_attention(p16, na_bias, layer=e, n_lat=n_lat, n_ctx=n_ctx, heads=na_heads)
            o_f, o_b = _hgrn_scan(p16, p32, hg_lb_logits, hg_consts, layer_e=e, n_lat=n_lat, n_ctx=n_ctx,
                                  heads=hg_heads, qcol=3 * a_w, icol=3 * a_w + b_w)
            h = _oproj_hgrn(ya, o_f, o_b, p32, 2 * b_w, hg_norm_g[e].reshape(1, -1), w_out_even, h, mod,
                            layer=e, tm=tm // 2, n_lat=n_lat)
        else:
            o = l // 2
            p16 = _qkv_proj(h, mod, g1, w_qkv_odd, cos, sin, layer=o, tm=tm, n_lat=n_lat,
                            rope_cols=(heads + kv_heads) * HEAD_DIM)
            y = _swa_attention(p16, sink_odd[o], n_lat=n_lat, n_ctx=n_ctx, heads=heads, kv_heads=kv_heads)
            h = _oproj(y, y, 1, w_o_odd, h, mod, layer=o, tm=tm, n_lat=n_lat)
        g2 = norm_g[l, 2].reshape(1, d)
        if last:
            h = _ffn(h, mod, g2, w_ff_in, w_ff_out, fg, layer=l, half=1,
                     sub=2, rows=n_lat, tm=tm, n_lat=n_lat, final_norm=True)
        else:
            h = _ffn(h, mod, g2, w_ff_in, w_ff_out, fg, layer=l, half=1,
                     sub=2, rows=rows, tm=tm, n_lat=n_lat)
    return h[None]
```

```python
import functools

import numpy as np
import jax
import jax.numpy as jnp
from jax import lax
from jax.experimental import pallas as pl
from jax.experimental.pallas import tpu as pltpu

F32 = jnp.float32
BF16 = jnp.bfloat16

GRID_W = 64
NORM_EPS = 1e-6
ROPE_THETA = 10000.0
HEAD_DIM = 128
NA_ROWS = 8
NA_COLS = 16
NA_Q_ROWS = 4
NA_K_ROWS = NA_Q_ROWS + NA_ROWS
ATT_BLOCK = NA_Q_ROWS * GRID_W
NA_HEADS_PER_STEP = 4
SWA_KV_PER_STEP = 4
WINDOW = 128
HG_STEP = 256
HG_CHUNK = 128
HG_HEADS_PER_STEP = 4
ROW_CHUNK = 16
FFN_COLS = 512
FFN_TAIL_COLS = 256
QKV_SLAB = 512
FFN_OUT_COLS = 512
ROW_UNROLL = 4
HG_ROW_LEVEL = 8
MASKED = -1e30
VMEM_LIMIT = 56 * 1024 * 1024


def _params(*sem):
    return pltpu.CompilerParams(dimension_semantics=sem, vmem_limit_bytes=VMEM_LIMIT)


def _dot(a, b):
    return jnp.dot(a, b, preferred_element_type=F32)


def _dot_nt(a, b):
    return lax.dot_general(a, b, (((1,), (1,)), ((), ())), preferred_element_type=F32)


def _sigmoid(x):
    return 1.0 / (1.0 + jnp.exp(-x))


def _silu(x):
    return x * _sigmoid(x)


def _mod_kernel(c_ref, w_ref, b_ref, o_ref):
    s = _silu(c_ref[...]).astype(BF16)
    o_ref[0, 0] = _dot(s, w_ref[0].astype(BF16)) + b_ref[0]


def _modulation(cc, w_mod, b_mod):
    depth, d, n = w_mod.shape
    return pl.pallas_call(
        _mod_kernel,
        grid=(depth, n // d),
        in_specs=[
            pl.BlockSpec((8, d), lambda l, j: (0, 0)),
            pl.BlockSpec((1, d, d), lambda l, j: (l, 0, j)),
            pl.BlockSpec((1, 1, d), lambda l, j: (l, 0, j)),
        ],
        out_specs=pl.BlockSpec((1, 1, 8, d), lambda l, j: (l, j, 0, 0)),
        out_shape=jax.ShapeDtypeStruct((depth, n // d, 8, d), F32),
        compiler_params=_params("arbitrary", "arbitrary"),
        name="modulation",
    )(cc, w_mod, b_mod.reshape(depth, 1, n))


def _mod_rows(mod_ref, sub, k, is_ctx):
    r = 3 * sub + k
    return jnp.where(is_ctx, mod_ref[r, 1:2, :], mod_ref[r, 0:1, :])


def _row_max(*blocks):
    acc = None
    for s in blocks:
        for c0 in range(0, s.shape[1], HEAD_DIM):
            t = s[:, c0:c0 + HEAD_DIM]
            acc = t if acc is None else jnp.maximum(acc, t)
    return jnp.max(acc, axis=-1, keepdims=True)


def _with_ones(v):
    return jnp.concatenate([v, jnp.ones_like(v)], axis=1)


def _is_ctx(tile, tm, n_lat):
    row = tile * tm + lax.broadcasted_iota(jnp.int32, (tm, 1), 0)
    return row >= n_lat


def _row_chunks(row0, tm, n_lat, body):
    def step(c, carry):
        r0 = pl.multiple_of(c * ROW_CHUNK, ROW_CHUNK)
        body(pl.ds(r0, ROW_CHUNK), (row0 + r0 >= n_lat).astype(jnp.int32))
        return carry

    lax.fori_loop(0, tm // ROW_CHUNK, step, 0, unroll=ROW_UNROLL)


def _adaln_rows(h_ref, mod_ref, g_ref, u_scr, gs_scr, *, sub, row0, tm, n_lat):
    for which in range(2):
        gs_scr[which] = g_ref[...] * (1.0 + mod_ref[3 * sub + 1, which:which + 1, :])

    def body(rows, which):
        x = h_ref[rows, :]
        r = lax.rsqrt(jnp.mean(x * x, axis=-1, keepdims=True) + NORM_EPS)
        u_scr[rows, :] = ((x * r) * gs_scr[which] + mod_ref[3 * sub, pl.ds(which, 1), :]).astype(BF16)

    _row_chunks(row0, tm, n_lat, body)


def _ffn_kernel(h_ref, mod_ref, g_ref, wg_ref, wu_ref, wo_ref, fg_ref, *rest,
                sub, tm, row0, n_lat, final_norm, emit16, aliased):
    rest = rest[1:] if aliased else rest
    o_ref = rest[0]
    u_scr, gs_scr = rest[-2:]
    i, j = pl.program_id(0), pl.program_id(1)
    first_row = row0 + i * tm

    @pl.when(j == 0)
    def _():
        _adaln_rows(h_ref, mod_ref, g_ref, u_scr, gs_scr, sub=sub, row0=first_row, tm=tm, n_lat=n_lat)
        o_ref[...] = jnp.zeros_like(o_ref)

    if emit16:
        wg16_ref, wu16_ref, wo16_ref = rest[1:4]
        wg16_ref[...] = wg_ref[...].astype(BF16)
        wu16_ref[...] = wu_ref[...].astype(BF16)
        wo16_ref[...] = wo_ref[...].astype(BF16)
        wg_ref, wu_ref, wo_ref = wg16_ref, wu16_ref, wo16_ref
    u = u_scr[...]
    a = (_silu(_dot(u, wg_ref[...])) * _dot(u, wu_ref[...])).astype(BF16)
    d = o_ref.shape[1]
    for c0 in range(0, d, FFN_OUT_COLS):
        o_ref[:, c0:c0 + FFN_OUT_COLS] += _dot(a, wo_ref[:, c0:c0 + FFN_OUT_COLS])

    @pl.when(j == pl.num_programs(1) - 1)
    def _():
        def body(rows, which):
            h = h_ref[rows, :] + (0.5 * mod_ref[3 * sub + 2, pl.ds(which, 1), :]) * o_ref[rows, :]
            if final_norm:
                h = h * lax.rsqrt(jnp.mean(h * h, axis=-1, keepdims=True) + NORM_EPS) * fg_ref[...]
            o_ref[rows, :] = h

        _row_chunks(first_row, tm, n_lat, body)


def _ffn(h, mod, g, w_in, w_out, fg, *, layer, half, sub, rows, tm, n_lat, final_norm=False, h_tail=None):
    d = h.shape[1]
    f = w_out.shape[2]
    tail = rows % tm or tm
    n_main = (rows - tail) // tm
    assert (rows - tail) % tail == 0
    tail_src, tail_blk = (h, (rows - tail) // tail) if h_tail is None else (h_tail, 0)
    common = dict(sub=sub, n_lat=n_lat, final_norm=final_norm)
    small = [
        pl.BlockSpec((9, 8, d), lambda i, j: (0, 0, 0)),
        pl.BlockSpec((1, d), lambda i, j: (0, 0)),
    ]
    fg_spec = pl.BlockSpec((1, d), lambda i, j: (0, 0))

    tf, t0 = FFN_TAIL_COLS, (rows - tail) // tail
    nf = f // tf
    out, wg16, wu16, wo16 = pl.pallas_call(
        functools.partial(_ffn_kernel, tm=tail, row0=rows - tail, emit16=True, aliased=False, **common),
        grid=(1, nf),
        in_specs=[pl.BlockSpec((tail, d), lambda i, j: (tail_blk, 0))] + small + [
            pl.BlockSpec((None, None, d, tf), lambda i, j: (layer, half, 0, j)),
            pl.BlockSpec((None, None, d, tf), lambda i, j: (layer, half, 0, nf + j)),
            pl.BlockSpec((None, None, tf, d), lambda i, j: (layer, half, j, 0)),
            fg_spec,
        ],
        out_specs=[
            pl.BlockSpec((tail, d), lambda i, j: (t0, 0)),
            pl.BlockSpec((d, tf), lambda i, j: (0, j)),
            pl.BlockSpec((d, tf), lambda i, j: (0, j)),
            pl.BlockSpec((tf, d), lambda i, j: (j, 0)),
        ],
        out_shape=[
            jax.ShapeDtypeStruct((rows, d), F32),
            jax.ShapeDtypeStruct((d, f), BF16),
            jax.ShapeDtypeStruct((d, f), BF16),
            jax.ShapeDtypeStruct((f, d), BF16),
        ],
        scratch_shapes=[pltpu.VMEM((tail, d), BF16), pltpu.VMEM((2, 1, d), F32)],
        compiler_params=_params("arbitrary", "arbitrary"),
        name="ffn_tail",
    )(tail_src, mod, g, w_in, w_in, w_out, fg)

    tf = FFN_COLS
    return pl.pallas_call(
        functools.partial(_ffn_kernel, tm=tm, row0=0, emit16=False, aliased=True, **common),
        grid=(n_main, f // tf),
        in_specs=[pl.BlockSpec((tm, d), lambda i, j: (i, 0))] + small + [
            pl.BlockSpec((d, tf), lambda i, j: (0, j)),
            pl.BlockSpec((d, tf), lambda i, j: (0, j)),
            pl.BlockSpec((tf, d), lambda i, j: (j, 0)),
            fg_spec,
            pl.BlockSpec(memory_space=pl.ANY),
        ],
        out_specs=pl.BlockSpec((tm, d), lambda i, j: (i, 0)),
        out_shape=jax.ShapeDtypeStruct((rows, d), F32),
        input_output_aliases={7: 0},
        scratch_shapes=[pltpu.VMEM((tm, d), BF16), pltpu.VMEM((2, 1, d), F32)],
        compiler_params=_params("arbitrary", "arbitrary"),
        name="ffn",
    )(h, mod, g, wg16, wu16, wo16, fg, out)


def _rope(x, cos, sin):
    lane = lax.broadcasted_iota(jnp.int32, x.shape, 1)
    first = (lane % 64) < 32
    swapped = jnp.where(first, pltpu.roll(x, 96, 1), pltpu.roll(x, 32, 1))
    return x * cos + swapped * sin


def _proj_kernel(h_ref, mod_ref, g_ref, w_ref, o16_ref, o32_ref, u_scr, gs_scr, *, tm, n_lat, n16):
    i, j = pl.program_id(0), pl.program_id(1)

    @pl.when(j == 0)
    def _():
        _adaln_rows(h_ref, mod_ref, g_ref, u_scr, gs_scr, sub=1, row0=i * tm, tm=tm, n_lat=n_lat)

    y = _dot(u_scr[...], w_ref[...])

    @pl.when(j < n16)
    def _():
        o16_ref[...] = y.astype(BF16)

    @pl.when(j >= n16)
    def _():
        o32_ref[...] = y


def _proj(h, mod, g, w, *, layer, tm, tn, n_lat, cols16):
    rows, d = h.shape
    n = w.shape[2]
    n16 = cols16 // tn
    return pl.pallas_call(
        functools.partial(_proj_kernel, tm=tm, n_lat=n_lat, n16=n16),
        grid=(rows // tm, n // tn),
        in_specs=[
            pl.BlockSpec((tm, d), lambda i, j: (i, 0)),
            pl.BlockSpec((9, 8, d), lambda i, j: (0, 0, 0)),
            pl.BlockSpec((1, d), lambda i, j: (0, 0)),
            pl.BlockSpec((None, d, tn), lambda i, j: (layer, 0, j)),
        ],
        out_specs=[
            pl.BlockSpec((tm, tn), lambda i, j: (i, jnp.minimum(j, n16 - 1))),
            pl.BlockSpec((tm, tn), lambda i, j: (i, jnp.maximum(j - n16, 0))),
        ],
        out_shape=[jax.ShapeDtypeStruct((rows, cols16), BF16), jax.ShapeDtypeStruct((rows, n - cols16), F32)],
        scratch_shapes=[pltpu.VMEM((tm, d), BF16), pltpu.VMEM((2, 1, d), F32)],
        compiler_params=_params("arbitrary", "arbitrary"),
        name="proj",
    )(h, mod, g, w)


def _qkv_kernel(h_ref, mod_ref, g_ref, w_ref, cos_ref, sin_ref, o_ref, u_scr, gs_scr, *, tm, n_lat, rope_cols):
    _adaln_rows(h_ref, mod_ref, g_ref, u_scr, gs_scr, sub=1, row0=pl.program_id(0) * tm, tm=tm, n_lat=n_lat)
    u = u_scr[...]
    cos, sin = cos_ref[...], sin_ref[...]
    for c0 in range(0, o_ref.shape[1], QKV_SLAB):
        y = _dot(u, w_ref[:, c0:c0 + QKV_SLAB])
        for hd in range(0, QKV_SLAB, HEAD_DIM):
            x = y[:, hd:hd + HEAD_DIM]
            if c0 + hd < rope_cols:
                x = _rope(x, cos, sin)
            o_ref[:, c0 + hd:c0 + hd + HEAD_DIM] = x.astype(BF16)


def _qkv_proj(h, mod, g, w, cos, sin, *, layer, tm, n_lat, rope_cols):
    rows, d = h.shape
    n = w.shape[2]
    return pl.pallas_call(
        functools.partial(_qkv_kernel, tm=tm, n_lat=n_lat, rope_cols=rope_cols),
        grid=(rows // tm,),
        in_specs=[
            pl.BlockSpec((tm, d), lambda i: (i, 0)),
            pl.BlockSpec((9, 8, d), lambda i: (0, 0, 0)),
            pl.BlockSpec((1, d), lambda i: (0, 0)),
            pl.BlockSpec((None, d, n), lambda i: (layer, 0, 0), pipeline_mode=pl.Buffered(1)),
            pl.BlockSpec((tm, HEAD_DIM), lambda i: (i, 0)),
            pl.BlockSpec((tm, HEAD_DIM), lambda i: (i, 0)),
        ],
        out_specs=pl.BlockSpec((tm, n), lambda i: (i, 0)),
        out_shape=jax.ShapeDtypeStruct((rows, n), BF16),
        scratch_shapes=[pltpu.VMEM((tm, d), BF16), pltpu.VMEM((2, 1, d), F32)],
        compiler_params=_params("arbitrary"),
        name="qkv_proj",
    )(h, mod, g, w, cos, sin)


def _rope_tables(n_lat, n_ctx):
    t = np.arange(n_lat)
    inv = (ROPE_THETA ** (-np.arange(0, 64, 2, dtype=np.float32) / 64)).astype(np.float32)
    ang_r = (t // GRID_W).astype(np.float32)[:, None] * inv[None, :]
    ang_c = (t % GRID_W).astype(np.float32)[:, None] * inv[None, :]
    cr, sr, cc, sc = np.cos(ang_r), np.sin(ang_r), np.cos(ang_c), np.sin(ang_c)
    cos = np.concatenate([cr, cr, cc, cc], axis=1)
    sin = np.concatenate([-sr, sr, -sc, sc], axis=1)
    cos = np.concatenate([cos, np.ones((n_ctx, HEAD_DIM), np.float32)], axis=0)
    sin = np.concatenate([sin, np.zeros((n_ctx, HEAD_DIM), np.float32)], axis=0)
    return jnp.asarray(cos, F32), jnp.asarray(sin, F32)


def _oproj_kernel(a1_ref, a2_ref, w_ref, h_ref, mod_ref, o_ref, *, tm, n_lat, k1):
    is_ctx = _is_ctx(pl.program_id(0), tm, n_lat)
    y = _dot(a1_ref[...], w_ref[:k1, :]) + _dot(a2_ref[...], w_ref[k1:, :])
    o_ref[...] = h_ref[...] + _mod_rows(mod_ref, 1, 2, is_ctx) * y


def _oproj(a1, a2, col2, w, h, mod, *, layer, tm, n_lat):
    rows, d = h.shape
    k1 = w.shape[1] // 2
    return pl.pallas_call(
        functools.partial(_oproj_kernel, tm=tm, n_lat=n_lat, k1=k1),
        grid=(rows // tm,),
        in_specs=[
            pl.BlockSpec((tm, k1), lambda i: (i, 0)),
            pl.BlockSpec((tm, k1), lambda i: (i, col2)),
            pl.BlockSpec((None,) + w.shape[1:], lambda i: (layer, 0, 0)),
            pl.BlockSpec((tm, d), lambda i: (i, 0)),
            pl.BlockSpec((9, 8, d), lambda i: (0, 0, 0)),
        ],
        out_specs=pl.BlockSpec((tm, d), lambda i: (i, 0)),
        out_shape=jax.ShapeDtypeStruct((rows, d), F32),
        compiler_params=_params("arbitrary"),
        name="oproj",
    )(a1, a2, w, h, mod)


def _oproj_hgrn_kernel(a1_ref, of_ref, ob_ref, g_ref, ng_ref, w_ref, h_ref, mod_ref, o_ref, yb_scr,
                       *, tm, n_lat, k1):
    for c0 in range(0, k1, HEAD_DIM):
        sl = slice(c0, c0 + HEAD_DIM)
        o = of_ref[:, sl] + ob_ref[:, sl]
        o = o * lax.rsqrt(jnp.mean(o * o, axis=-1, keepdims=True) + NORM_EPS)
        yb_scr[:, sl] = (o * ng_ref[:, sl] * _silu(g_ref[:, sl])).astype(BF16)
    is_ctx = _is_ctx(pl.program_id(0), tm, n_lat)
    y = _dot(a1_ref[...], w_ref[:k1, :]) + _dot(yb_scr[...], w_ref[k1:, :])
    o_ref[...] = h_ref[...] + _mod_rows(mod_ref, 1, 2, is_ctx) * y


def _oproj_hgrn(a1, o_f, o_b, p32, gcol0, norm_g, w, h, mod, *, layer, tm, n_lat):
    rows, d = h.shape
    k1 = w.shape[1] // 2
    half = lambda i: (i, 0)
    return pl.pallas_call(
        functools.partial(_oproj_hgrn_kernel, tm=tm, n_lat=n_lat, k1=k1),
        grid=(rows // tm,),
        in_specs=[
            pl.BlockSpec((tm, k1), half),
            pl.BlockSpec((tm, k1), half),
            pl.BlockSpec((tm, k1), half),
            pl.BlockSpec((tm, k1), lambda i: (i, gcol0 // k1)),
            pl.BlockSpec((1, k1), lambda i: (0, 0)),
            pl.BlockSpec((None,) + w.shape[1:], lambda i: (layer, 0, 0)),
            pl.BlockSpec((tm, d), half),
            pl.BlockSpec((9, 8, d), lambda i: (0, 0, 0)),
        ],
        out_specs=pl.BlockSpec((tm, d), half),
        out_shape=jax.ShapeDtypeStruct((rows, d), F32),
        scratch_shapes=[pltpu.VMEM((tm, k1), BF16)],
        compiler_params=_params("arbitrary"),
        name="oproj_hgrn",
    )(a1, o_f, o_b, p32, norm_g, w, h, mod)


def _na_bias(rpb, n_rows):
    nh, n_a, n_b = rpb.shape
    nb = n_rows // NA_Q_ROWS
    w = jnp.full((nh, n_a, 128), MASKED, F32)
    w = w.at[..., :NA_COLS].set(rpb[..., NA_COLS - 1:]).at[..., 128 - (NA_COLS - 1):].set(rpb[..., :NA_COLS - 1])
    toep = jnp.tile(w, (1, 1, GRID_W))[..., :GRID_W * 127].reshape(nh, n_a, GRID_W, 127)[..., :GRID_W]
    toep = jnp.pad(toep.transpose(0, 2, 1, 3), ((0, 0), (0, 0), (NA_K_ROWS, NA_K_ROWS), (0, 0)),
                   constant_values=MASKED).reshape(nh, GRID_W, -1)

    q = np.arange(ATT_BLOCK)
    k = np.arange(NA_K_ROWS * GRID_W)
    dr, c = q // GRID_W, q % GRID_W
    kr, kc = k // GRID_W, k % GRID_W
    c0 = np.clip(c - NA_COLS // 2, 0, GRID_W - NA_COLS)
    col_ok = (kc[None, :] >= c0[:, None]) & (kc[None, :] < c0[:, None] + NA_COLS)
    tables, ok = [], []
    for b in (0, 1, nb - 1):
        r = b * NA_Q_ROWS + dr
        r0 = np.clip(r - NA_ROWS // 2, 0, n_rows - NA_ROWS)
        base = int(np.clip(b * NA_Q_ROWS - NA_ROWS // 2, 0, n_rows - NA_K_ROWS))
        kabs = base + kr
        ok.append((kabs[None, :] >= r0[:, None]) & (kabs[None, :] < r0[:, None] + NA_ROWS) & col_ok)
        per_row = []
        for j in range(NA_Q_ROWS):
            lo = base - (b * NA_Q_ROWS + j) + NA_ROWS - 1 + NA_K_ROWS
            per_row.append(toep[:, :, lo * GRID_W:(lo + NA_K_ROWS) * GRID_W])
        tables.append(jnp.concatenate(per_row, axis=1))
    return jnp.where(np.stack(ok)[:, None], jnp.stack(tables), MASKED)


def _na_kernel(q_ref, k_ref, v_ref, bias_ref, o_ref, *, n_lat, n_ctx, nb):
    b = pl.program_id(1)
    scale = HEAD_DIM ** -0.5
    nk = NA_K_ROWS * GRID_W
    start = pl.multiple_of(jnp.clip(b - 1, 0, nb - 3) * ATT_BLOCK, ATT_BLOCK)

    def one_head(hh, local):
        cols = slice(hh * HEAD_DIM, (hh + 1) * HEAD_DIM)
        q = q_ref[:, cols]
        s_ctx = _dot_nt(q, k_ref[n_lat:n_lat + n_ctx, cols]) * scale
        vc1 = _with_ones(v_ref[n_lat:n_lat + n_ctx, cols])
        if local:
            kl = k_ref[pl.ds(start, nk), cols]
            vl1 = _with_ones(v_ref[pl.ds(start, nk), cols])
            s_loc = _dot_nt(q, kl) * scale + bias_ref[0, hh]
            m = _row_max(s_loc, s_ctx)
            acc = _dot(jnp.exp(s_loc - m).astype(BF16), vl1) + _dot(jnp.exp(s_ctx - m).astype(BF16), vc1)
        else:
            acc = _dot(jnp.exp(s_ctx - _row_max(s_ctx)).astype(BF16), vc1)
        o_ref[:, cols] = (acc[:, :HEAD_DIM] / acc[:, HEAD_DIM:]).astype(o_ref.dtype)

    @pl.when(b < nb)
    def _():
        for hh in range(NA_HEADS_PER_STEP):
            one_head(hh, True)

    @pl.when(b >= nb)
    def _():
        for hh in range(NA_HEADS_PER_STEP):
            one_head(hh, False)


def _na_attention(p, bias, *, layer, n_lat, n_ctx, heads):
    rows = n_lat + n_ctx
    nb = n_lat // ATT_BLOCK
    nblk = rows // ATT_BLOCK
    hps = NA_HEADS_PER_STEP
    width = hps * HEAD_DIM
    groups = heads // hps

    def pattern(b):
        return jnp.where(b == 0, 0, jnp.where(b == nb - 1, 2, 1))

    return pl.pallas_call(
        functools.partial(_na_kernel, n_lat=n_lat, n_ctx=n_ctx, nb=nb),
        grid=(groups, nblk),
        in_specs=[
            pl.BlockSpec((ATT_BLOCK, width), lambda h, b: (b, h)),
            pl.BlockSpec((rows, width), lambda h, b: (0, groups + h)),
            pl.BlockSpec((rows, width), lambda h, b: (0, 2 * groups + h)),
            pl.BlockSpec((1, hps, ATT_BLOCK, NA_K_ROWS * GRID_W),
                         lambda h, b: (pattern(b), layer * groups + h, 0, 0)),
        ],
        out_specs=pl.BlockSpec((ATT_BLOCK, width), lambda h, b: (b, h)),
        out_shape=jax.ShapeDtypeStruct((rows, heads * HEAD_DIM), BF16),
        compiler_params=_params("arbitrary", "arbitrary"),
        name="na_attention",
    )(p, p, p, bias)


def _hgrn_levels():
    levels, m = [], 1
    while m < HG_CHUNK:
        levels.append(m)
        m *= 2
    return levels


def _hgrn_consts():
    c = HG_CHUNK
    levels = _hgrn_levels()
    fine = [m for m in levels if m < HG_ROW_LEVEL]
    idx = np.arange(c)
    t, j = idx[:, None], idx[None, :]
    n = np.zeros((2, 1 + len(fine), c, c), np.float32)
    msk = np.zeros((2, 1 + len(levels), c, c), np.float32)
    n[0, 0], n[1, 0] = j <= t, j >= t
    msk[0, 0] = msk[1, 0] = np.eye(c)
    for li, m in enumerate(levels):
        seg = idx // (2 * m)
        right = ((idx % (2 * m)) >= m)[:, None]
        last_left = (seg * 2 * m + m - 1)[:, None]
        first_right = last_left + 1
        if m < HG_ROW_LEVEL:
            n[0, 1 + li] = np.where(right, (j > last_left) & (j <= t), (j > t) & (j <= last_left))
            n[1, 1 + li] = np.where(right, (j >= first_right) & (j < t), (j >= t) & (j < first_right))
        same = seg[:, None] == seg[None, :]
        msk[0, 1 + li] = same & right & ~right.T
        msk[1, 1 + li] = same & ~right & right.T
    return n.reshape(2, -1, c), msk


def _hgrn_chunk(d, q, fx, v, lb, n_ref, msk_ref, s_scr, b_scr):
    c = HG_CHUNK
    t = jnp.exp(-jnp.abs(fx))
    r = 1.0 / (1.0 + t)
    log_sig = jnp.minimum(fx, 0.0) + jnp.log(r)
    sig_neg = jnp.where(fx >= 0.0, t * r, r)
    if lb is None:
        log_f, kk = log_sig, sig_neg
    else:
        la = jnp.log(lb)
        lc = jnp.log1p(-lb) + log_sig
        log_f = jnp.maximum(la, lc) + jnp.log(1.0 + jnp.exp(-jnp.abs(la - lc)))
        kk = (1.0 - lb) * sig_neg
    qf = q.astype(F32)

    hi = log_f.astype(BF16)
    mid = (log_f - hi.astype(F32)).astype(BF16)
    dd = _dot(n_ref[d], jnp.concatenate([hi, mid], axis=1))
    dd = dd[:, :HEAD_DIM] + dd[:, HEAD_DIM:]
    b = dd[:c]
    b_scr[d] = b

    attn = _dot_nt(q, kk.astype(BF16)) * msk_ref[d, 0]
    for li, m in enumerate(_hgrn_levels()):
        if m < HG_ROW_LEVEL:
            ex = dd[(1 + li) * c:(2 + li) * c]
        else:
            row = m - 1 + d
            ref = jnp.concatenate(
                [jnp.broadcast_to(b_scr[d, s + row:s + row + 1, :], (2 * m, HEAD_DIM)) for s in range(0, c, 2 * m)],
                axis=0)
            ex = -jnp.abs(b - ref)
        e = jnp.exp(ex)
        attn += _dot_nt((qf * e).astype(BF16), (kk * e).astype(BF16)) * msk_ref[d, 1 + li]

    total = b[c - 1:c] if d == 0 else b[0:1]
    state_t = s_scr[d]
    o = _dot_nt((qf * jnp.exp(b)).astype(BF16), state_t.astype(BF16)) + _dot(attn.astype(BF16), v)
    v_t = v.astype(F32).T.astype(BF16)
    s_scr[d] = jnp.exp(total) * state_t + _dot(v_t, (kk * jnp.exp(total - b)).astype(BF16))
    return o


def _hgrn_kernel(qf_ref, qb_ref, ff_ref, fb_ref, if_ref, ib_ref, lbl_ref, n_ref, msk_ref,
                 of_ref, ob_ref, s_scr, b_scr, *, layer_e):
    @pl.when(pl.program_id(1) == 0)
    def _():
        s_scr[...] = jnp.zeros_like(s_scr)

    refs = ((qf_ref, ff_ref, if_ref, of_ref), (qb_ref, fb_ref, ib_ref, ob_ref))
    lbs = [None, None]
    if layer_e > 0:
        for d in range(2):
            lg = lbl_ref[d]
            ex = jnp.exp(lg - jnp.max(lg, axis=0, keepdims=True))
            num = ex[1:2]
            for e in range(2, layer_e + 1):
                num = num + ex[e:e + 1]
            lbs[d] = num / jnp.sum(ex, axis=0, keepdims=True)

    n_sub = HG_STEP // HG_CHUNK
    for k in range(n_sub):
        for hh in range(HG_HEADS_PER_STEP):
            cols = slice(hh * HEAD_DIM, (hh + 1) * HEAD_DIM)
            for d, (q_ref, f_ref, i_ref, o_ref) in enumerate(refs):
                sub = k if d == 0 else n_sub - 1 - k
                rows = slice(sub * HG_CHUNK, (sub + 1) * HG_CHUNK)
                lb = None if lbs[d] is None else lbs[d][:, cols]
                o_ref[rows, cols] = _hgrn_chunk(d, q_ref[rows, cols], f_ref[rows, cols], i_ref[rows, cols], lb,
                                                n_ref, msk_ref, s_scr.at[hh], b_scr.at[hh, k])


def _hgrn_scan(p16, p32, lb_logits, consts, *, layer_e, n_lat, n_ctx, heads, qcol, icol):
    rows = n_lat + n_ctx
    n_mat, msk = consts
    lat_chunks = n_lat // HG_STEP
    qc, ic = qcol // HEAD_DIM, icol // HEAD_DIM

    def fwd(s):
        return jnp.where(s == 0, lat_chunks, s - 1)

    def bwd(s):
        return lat_chunks - s

    hps = HG_HEADS_PER_STEP
    width = hps * HEAD_DIM

    def spec(blk, col):
        return pl.BlockSpec((HG_STEP, width), lambda h, s: (blk(s), col // hps + h))

    out = jax.ShapeDtypeStruct((rows, heads * HEAD_DIM), F32)
    return pl.pallas_call(
        functools.partial(_hgrn_kernel, layer_e=layer_e),
        grid=(heads // hps, 1 + lat_chunks),
        in_specs=[
            spec(fwd, qc), spec(bwd, qc),
            spec(fwd, 0), spec(bwd, heads),
            spec(fwd, ic), spec(bwd, ic),
            pl.BlockSpec((2, lb_logits.shape[1], width), lambda h, s: (0, 0, h)),
            pl.BlockSpec(n_mat.shape, lambda h, s: (0, 0, 0)),
            pl.BlockSpec(msk.shape, lambda h, s: (0, 0, 0, 0)),
        ],
        out_specs=[spec(fwd, 0), spec(bwd, 0)],
        out_shape=[out, out],
        scratch_shapes=[pltpu.VMEM((hps, 2, HEAD_DIM, HEAD_DIM), F32),
                        pltpu.VMEM((hps, HG_STEP // HG_CHUNK, 2, HG_CHUNK, HEAD_DIM), F32)],
        compiler_params=_params("arbitrary", "arbitrary"),
        name="hgrn_scan",
    )(p16, p16, p32, p32, p16, p16, lb_logits, n_mat, msk)


def _swa_kernel(q_ref, k_ref, v_ref, sink_ref, bias_ref, o_ref, *, n_lat, n_ctx, nb, group, span):
    b = pl.program_id(1)
    scale = HEAD_DIM ** -0.5
    start = pl.multiple_of(jnp.clip(b * ATT_BLOCK - WINDOW, 0, n_lat - span), WINDOW)

    def one_kv_head(kv, local):
        kcols = slice(kv * HEAD_DIM, (kv + 1) * HEAD_DIM)
        kc = k_ref[n_lat:n_lat + n_ctx, kcols]
        vc1 = _with_ones(v_ref[n_lat:n_lat + n_ctx, kcols])
        if local:
            kl = k_ref[pl.ds(start, span), kcols]
            vl1 = _with_ones(v_ref[pl.ds(start, span), kcols])
        for g in range(group):
            qcols = slice((kv * group + g) * HEAD_DIM, (kv * group + g + 1) * HEAD_DIM)
            q = q_ref[:, qcols]
            sink = sink_ref[kv, g:g + 1, 0:1]
            s_ctx = _dot_nt(q, kc) * scale
            if local:
                s_loc = _dot_nt(q, kl) * scale + bias_ref[0]
                m = jnp.maximum(_row_max(s_ctx, s_loc), sink)
                acc = _dot(jnp.exp(s_ctx - m).astype(BF16), vc1) + _dot(jnp.exp(s_loc - m).astype(BF16), vl1)
            else:
                m = jnp.maximum(_row_max(s_ctx), sink)
                acc = _dot(jnp.exp(s_ctx - m).astype(BF16), vc1)
            den = acc[:, HEAD_DIM:] + jnp.exp(sink - m)
            o_ref[:, qcols] = (acc[:, :HEAD_DIM] / den).astype(o_ref.dtype)

    @pl.when(b < nb)
    def _():
        for kv in range(SWA_KV_PER_STEP):
            one_kv_head(kv, True)

    @pl.when(b >= nb)
    def _():
        for kv in range(SWA_KV_PER_STEP):
            one_kv_head(kv, False)


def _swa_bias(n_lat):
    span = ATT_BLOCK + 2 * WINDOW
    nb = n_lat // ATT_BLOCK
    i = np.arange(ATT_BLOCK)[:, None]
    j = np.arange(span)[None, :]
    tables = []
    for b in (0, 1, nb - 1):
        start = int(np.clip(b * ATT_BLOCK - WINDOW, 0, n_lat - span))
        rel = (start + j) - (b * ATT_BLOCK + i)
        tables.append(np.where(np.abs(rel) <= WINDOW, 0.0, MASKED))
    return jnp.asarray(np.stack(tables), F32)


def _swa_attention(p, sink, *, n_lat, n_ctx, heads, kv_heads):
    rows = n_lat + n_ctx
    group = heads // kv_heads
    nb = n_lat // ATT_BLOCK
    span = ATT_BLOCK + 2 * WINDOW
    kps = SWA_KV_PER_STEP
    sink_tab = jnp.broadcast_to(sink.astype(F32).reshape(kv_heads, group, 1), (kv_heads, group, HEAD_DIM))

    def pattern(b):
        return jnp.where(b == 0, 0, jnp.where(b == nb - 1, 2, 1))

    return pl.pallas_call(
        functools.partial(_swa_kernel, n_lat=n_lat, n_ctx=n_ctx, nb=nb, group=group, span=span),
        grid=(kv_heads // kps, rows // ATT_BLOCK),
        in_specs=[
            pl.BlockSpec((ATT_BLOCK, kps * group * HEAD_DIM), lambda k, b: (b, k)),
            pl.BlockSpec((rows, kps * HEAD_DIM), lambda k, b: (0, heads // kps + k)),
            pl.BlockSpec((rows, kps * HEAD_DIM), lambda k, b: (0, (heads + kv_heads) // kps + k)),
            pl.BlockSpec((kps, group, HEAD_DIM), lambda k, b: (k, 0, 0)),
            pl.BlockSpec((1, ATT_BLOCK, span), lambda k, b: (pattern(b), 0, 0)),
        ],
        out_specs=pl.BlockSpec((ATT_BLOCK, kps * group * HEAD_DIM), lambda k, b: (b, k)),
        out_shape=jax.ShapeDtypeStruct((rows, heads * HEAD_DIM), BF16),
        compiler_params=_params("arbitrary", "arbitrary"),
        name="swa_attention",
    )(p, p, p, sink_tab, _swa_bias(n_lat))


def kernel(x, c, ctx, c_ctx, w_mod, b_mod, norm_g, w_ff_in, w_ff_out, w_in_even, w_out_even,
           na_rpb, hg_lb_logits, hg_norm_g, w_qkv_odd, w_o_odd, sink_odd, final_norm_g):
    assert x.shape[0] == 1
    depth, d = w_mod.shape[0], x.shape[2]
    n_lat, n_ctx = x.shape[1], ctx.shape[1]
    rows = n_lat + n_ctx
    tm = 768
    assert rows % tm == 0 and n_lat % 1024 == 0 and n_ctx == ATT_BLOCK
    na_heads = na_rpb.shape[1]
    a_w = na_heads * HEAD_DIM
    hg_heads = hg_norm_g.shape[1] // HEAD_DIM
    heads = sink_odd.shape[1]
    kv_heads = (w_qkv_odd.shape[2] // HEAD_DIM - heads) // 2

    cc = jnp.zeros((8, d), F32).at[0].set(c[0]).at[1].set(c_ctx)
    mod_all = _modulation(cc, w_mod, b_mod)

    cos, sin = _rope_tables(n_lat, n_ctx)
    na_bias = _na_bias(na_rpb.reshape((-1,) + na_rpb.shape[2:]), n_lat // GRID_W)
    hg_consts = _hgrn_consts()
    hg_consts = (jnp.asarray(hg_consts[0], BF16), jnp.asarray(hg_consts[1], F32))
    fg = final_norm_g.reshape(1, d)
    b_w = hg_heads * HEAD_DIM
    c_f, c_i, c_g = 3 * a_w + b_w, 3 * a_w + 3 * b_w, 3 * a_w + 4 * b_w
    w_in_even = jnp.concatenate([w_in_even[:, :, :c_f], w_in_even[:, :, c_i:c_g], w_in_even[:, :, c_f:c_i],
                                 w_in_even[:, :, c_g:]], axis=2).astype(BF16)
    w_out_even = w_out_even.astype(BF16)
    w_qkv_odd, w_o_odd = w_qkv_odd.astype(BF16), w_o_odd.astype(BF16)

    h = x.reshape(n_lat, d)
    h_tail = jnp.concatenate([h[rows - tm:], ctx.reshape(n_ctx, d)], axis=0)
    for l in range(depth):
        mod = mod_all[l]
        last = l == depth - 1
        h = _ffn(h, mod, norm_g[l, 0].reshape(1, d), w_ff_in, w_ff_out, fg, layer=l, half=0,
                 sub=0, rows=rows, tm=tm, n_lat=n_lat, h_tail=h_tail if l == 0 else None)
        g1 = norm_g[l, 1].reshape(1, d)
        if l % 2 == 0:
            e = l // 2
            p16, p32 = _proj(h, mod, g1, w_in_even, layer=e, tm=tm, tn=1024, n_lat=n_lat, cols16=3 * a_w + 2 * b_w)
            ya = _na_attention(p16, na_bias, layer=e, n_lat=n_lat, n_ctx=n_ctx, heads=na_heads)
            o_f, o_b = _hgrn_scan(p16, p32, hg_lb_logits, hg_consts, layer_e=e, n_lat=n_lat, n_ctx=n_ctx,
                                  heads=hg_heads, qcol=3 * a_w, icol=3 * a_w + b_w)
            h = _oproj_hgrn(ya, o_f, o_b, p32, 2 * b_w, hg_norm_g[e].reshape(1, -1), w_out_even, h, mod,
                            layer=e, tm=tm // 2, n_lat=n_lat)
        else:
            o = l // 2
            p16 = _qkv_proj(h, mod, g1, w_qkv_odd, cos, sin, layer=o, tm=tm, n_lat=n_lat,
                            rope_cols=(heads + kv_heads) * HEAD_DIM)
            y = _swa_attention(p16, sink_odd[o], n_lat=n_lat, n_ctx=n_ctx, heads=heads, kv_heads=kv_heads)
            h = _oproj(y, y, 1, w_o_odd, h, mod, layer=o, tm=tm, n_lat=n_lat)
        g2 = norm_g[l, 2].reshape(1, d)
        if last:
            h = _ffn(h, mod, g2, w_ff_in, w_ff_out, fg, layer=l, half=1,
                     sub=2, rows=n_lat, tm=tm, n_lat=n_lat, final_norm=True)
        else:
            h = _ffn(h, mod, g2, w_ff_in, w_ff_out, fg, layer=l, half=1,
                     sub=2, rows=rows, tm=tm, n_lat=n_lat)
    return h.reshape(1, n_lat, d)
```

```python
import functools

import numpy as np
import jax
import jax.numpy as jnp
from jax import lax
from jax.experimental import pallas as pl
from jax.experimental.pallas import tpu as pltpu

F32 = jnp.float32
BF16 = jnp.bfloat16

GRID_W = 64
NORM_EPS = 1e-6
ROPE_THETA = 10000.0
HEAD_DIM = 128
NA_ROWS = 8
NA_COLS = 16
NA_Q_ROWS = 4
NA_K_ROWS = NA_Q_ROWS + NA_ROWS
ATT_BLOCK = NA_Q_ROWS * GRID_W
NA_HEADS_PER_STEP = 4
SWA_KV_PER_STEP = 4
WINDOW = 128
HG_STEP = 256
HG_CHUNK = 128
HG_HEADS_PER_STEP = 4
ROW_CHUNK = 16
FFN_ROWS = 960
FFN_COLS = 512
FFN_TAIL_COLS = 256
QKV_SLAB = 512
FFN_OUT_COLS = 512
ROW_UNROLL = 4
HG_ROW_LEVEL = 8
MASKED = -1e30
VMEM_LIMIT = 56 * 1024 * 1024


def _params(*sem):
    return pltpu.CompilerParams(dimension_semantics=sem, vmem_limit_bytes=VMEM_LIMIT)


def _dot(a, b):
    return jnp.dot(a, b, preferred_element_type=F32)


def _dot_nt(a, b):
    return lax.dot_general(a, b, (((1,), (1,)), ((), ())), preferred_element_type=F32)


def _sigmoid(x):
    return 1.0 / (1.0 + jnp.exp(-x))


def _silu(x):
    return x * _sigmoid(x)


def _mod_kernel(c_ref, w_ref, b_ref, o_ref):
    s = _silu(c_ref[...]).astype(BF16)
    o_ref[0, 0] = _dot(s, w_ref[0].astype(BF16)) + b_ref[0]


def _modulation(cc, w_mod, b_mod):
    depth, d, n = w_mod.shape
    return pl.pallas_call(
        _mod_kernel,
        grid=(depth, n // d),
        in_specs=[
            pl.BlockSpec((8, d), lambda l, j: (0, 0)),
            pl.BlockSpec((1, d, d), lambda l, j: (l, 0, j)),
            pl.BlockSpec((1, 1, d), lambda l, j: (l, 0, j)),
        ],
        out_specs=pl.BlockSpec((1, 1, 8, d), lambda l, j: (l, j, 0, 0)),
        out_shape=jax.ShapeDtypeStruct((depth, n // d, 8, d), F32),
        compiler_params=_params("arbitrary", "arbitrary"),
        name="modulation",
    )(cc, w_mod, b_mod.reshape(depth, 1, n))


def _mod_rows(mod_ref, sub, k, is_ctx):
    r = 3 * sub + k
    return jnp.where(is_ctx, mod_ref[r, 1:2, :], mod_ref[r, 0:1, :])


def _row_max(*blocks):
    acc = None
    for s in blocks:
        for c0 in range(0, s.shape[1], HEAD_DIM):
            t = s[:, c0:c0 + HEAD_DIM]
            acc = t if acc is None else jnp.maximum(acc, t)
    return jnp.max(acc, axis=-1, keepdims=True)


def _with_ones(v):
    return jnp.concatenate([v, jnp.ones_like(v)], axis=1)


def _is_ctx(tile, tm, n_lat):
    row = tile * tm + lax.broadcasted_iota(jnp.int32, (tm, 1), 0)
    return row >= n_lat


def _row_chunks(row0, tm, n_lat, body):
    def step(c, carry):
        r0 = pl.multiple_of(c * ROW_CHUNK, ROW_CHUNK)
        body(pl.ds(r0, ROW_CHUNK), (row0 + r0 >= n_lat).astype(jnp.int32))
        return carry

    lax.fori_loop(0, tm // ROW_CHUNK, step, 0, unroll=ROW_UNROLL)


def _adaln_rows(h_ref, mod_ref, g_ref, u_scr, gs_scr, *, sub, row0, tm, n_lat):
    for which in range(2):
        gs_scr[which] = g_ref[...] * (1.0 + mod_ref[3 * sub + 1, which:which + 1, :])

    def body(rows, which):
        x = h_ref[rows, :]
        r = lax.rsqrt(jnp.mean(x * x, axis=-1, keepdims=True) + NORM_EPS)
        u_scr[rows, :] = ((x * r) * gs_scr[which] + mod_ref[3 * sub, pl.ds(which, 1), :]).astype(BF16)

    _row_chunks(row0, tm, n_lat, body)


def _ffn_kernel(h_ref, mod_ref, g_ref, wg_ref, wu_ref, wo_ref, fg_ref, *rest,
                sub, tm, row0, n_lat, final_norm, emit16, aliased):
    rest = rest[1:] if aliased else rest
    o_ref = rest[0]
    u_scr, gs_scr = rest[-2:]
    i, j = pl.program_id(0), pl.program_id(1)
    first_row = row0 + i * tm

    @pl.when(j == 0)
    def _():
        _adaln_rows(h_ref, mod_ref, g_ref, u_scr, gs_scr, sub=sub, row0=first_row, tm=tm, n_lat=n_lat)
        o_ref[...] = jnp.zeros_like(o_ref)

    if emit16:
        wg16_ref, wu16_ref, wo16_ref = rest[1:4]
        wg16_ref[...] = wg_ref[...].astype(BF16)
        wu16_ref[...] = wu_ref[...].astype(BF16)
        wo16_ref[...] = wo_ref[...].astype(BF16)
        wg_ref, wu_ref, wo_ref = wg16_ref, wu16_ref, wo16_ref
    u = u_scr[...]
    a = (_silu(_dot(u, wg_ref[...])) * _dot(u, wu_ref[...])).astype(BF16)
    d = o_ref.shape[1]
    for c0 in range(0, d, FFN_OUT_COLS):
        o_ref[:, c0:c0 + FFN_OUT_COLS] += _dot(a, wo_ref[:, c0:c0 + FFN_OUT_COLS])

    @pl.when(j == pl.num_programs(1) - 1)
    def _():
        def body(rows, which):
            h = h_ref[rows, :] + (0.5 * mod_ref[3 * sub + 2, pl.ds(which, 1), :]) * o_ref[rows, :]
            if final_norm:
                h = h * lax.rsqrt(jnp.mean(h * h, axis=-1, keepdims=True) + NORM_EPS) * fg_ref[...]
            o_ref[rows, :] = h

        _row_chunks(first_row, tm, n_lat, body)


def _ffn(h, mod, g, w_in, w_out, fg, *, layer, half, sub, rows, tm, n_lat, final_norm=False, h_tail=None):
    d = h.shape[1]
    f = w_out.shape[2]
    tail = rows % tm or tm
    n_main = (rows - tail) // tm
    assert (rows - tail) % tail == 0
    tail_src, tail_blk = (h, (rows - tail) // tail) if h_tail is None else (h_tail, 0)
    common = dict(sub=sub, n_lat=n_lat, final_norm=final_norm)
    small = [
        pl.BlockSpec((9, 8, d), lambda i, j: (0, 0, 0)),
        pl.BlockSpec((1, d), lambda i, j: (0, 0)),
    ]
    fg_spec = pl.BlockSpec((1, d), lambda i, j: (0, 0))

    tf, t0 = FFN_TAIL_COLS, (rows - tail) // tail
    nf = f // tf
    out, wg16, wu16, wo16 = pl.pallas_call(
        functools.partial(_ffn_kernel, tm=tail, row0=rows - tail, emit16=True, aliased=False, **common),
        grid=(1, nf),
        in_specs=[pl.BlockSpec((tail, d), lambda i, j: (tail_blk, 0))] + small + [
            pl.BlockSpec((None, None, d, tf), lambda i, j: (layer, half, 0, j)),
            pl.BlockSpec((None, None, d, tf), lambda i, j: (layer, half, 0, nf + j)),
            pl.BlockSpec((None, None, tf, d), lambda i, j: (layer, half, j, 0)),
            fg_spec,
        ],
        out_specs=[
            pl.BlockSpec((tail, d), lambda i, j: (t0, 0)),
            pl.BlockSpec((d, tf), lambda i, j: (0, j)),
            pl.BlockSpec((d, tf), lambda i, j: (0, j)),
            pl.BlockSpec((tf, d), lambda i, j: (j, 0)),
        ],
        out_shape=[
            jax.ShapeDtypeStruct((rows, d), F32),
            jax.ShapeDtypeStruct((d, f), BF16),
            jax.ShapeDtypeStruct((d, f), BF16),
            jax.ShapeDtypeStruct((f, d), BF16),
        ],
        scratch_shapes=[pltpu.VMEM((tail, d), BF16), pltpu.VMEM((2, 1, d), F32)],
        compiler_params=_params("arbitrary", "arbitrary"),
        name="ffn_tail",
    )(tail_src, mod, g, w_in, w_in, w_out, fg)

    tf = FFN_COLS
    return pl.pallas_call(
        functools.partial(_ffn_kernel, tm=tm, row0=0, emit16=False, aliased=True, **common),
        grid=(n_main, f // tf),
        in_specs=[pl.BlockSpec((tm, d), lambda i, j: (i, 0))] + small + [
            pl.BlockSpec((d, tf), lambda i, j: (0, j)),
            pl.BlockSpec((d, tf), lambda i, j: (0, j)),
            pl.BlockSpec((tf, d), lambda i, j: (j, 0)),
            fg_spec,
            pl.BlockSpec(memory_space=pl.ANY),
        ],
        out_specs=pl.BlockSpec((tm, d), lambda i, j: (i, 0)),
        out_shape=jax.ShapeDtypeStruct((rows, d), F32),
        input_output_aliases={7: 0},
        scratch_shapes=[pltpu.VMEM((tm, d), BF16), pltpu.VMEM((2, 1, d), F32)],
        compiler_params=_params("arbitrary", "arbitrary"),
        name="ffn",
    )(h, mod, g, wg16, wu16, wo16, fg, out)


def _rope(x, cos, sin):
    lane = lax.broadcasted_iota(jnp.int32, x.shape, 1)
    first = (lane % 64) < 32
    swapped = jnp.where(first, pltpu.roll(x, 96, 1), pltpu.roll(x, 32, 1))
    return x * cos + swapped * sin


def _proj_kernel(h_ref, mod_ref, g_ref, w_ref, o16_ref, o32_ref, u_scr, gs_scr, *, tm, n_lat, n16):
    i, j = pl.program_id(0), pl.program_id(1)

    @pl.when(j == 0)
    def _():
        _adaln_rows(h_ref, mod_ref, g_ref, u_scr, gs_scr, sub=1, row0=i * tm, tm=tm, n_lat=n_lat)

    y = _dot(u_scr[...], w_ref[...])

    @pl.when(j < n16)
    def _():
        o16_ref[...] = y.astype(BF16)

    @pl.when(j >= n16)
    def _():
        o32_ref[...] = y


def _proj(h, mod, g, w, *, layer, tm, tn, n_lat, cols16):
    rows, d = h.shape
    n = w.shape[2]
    n16 = cols16 // tn
    return pl.pallas_call(
        functools.partial(_proj_kernel, tm=tm, n_lat=n_lat, n16=n16),
        grid=(rows // tm, n // tn),
        in_specs=[
            pl.BlockSpec((tm, d), lambda i, j: (i, 0)),
            pl.BlockSpec((9, 8, d), lambda i, j: (0, 0, 0)),
            pl.BlockSpec((1, d), lambda i, j: (0, 0)),
            pl.BlockSpec((None, d, tn), lambda i, j: (layer, 0, j)),
        ],
        out_specs=[
            pl.BlockSpec((tm, tn), lambda i, j: (i, jnp.minimum(j, n16 - 1))),
            pl.BlockSpec((tm, tn), lambda i, j: (i, jnp.maximum(j - n16, 0))),
        ],
        out_shape=[jax.ShapeDtypeStruct((rows, cols16), BF16), jax.ShapeDtypeStruct((rows, n - cols16), F32)],
        scratch_shapes=[pltpu.VMEM((tm, d), BF16), pltpu.VMEM((2, 1, d), F32)],
        compiler_params=_params("arbitrary", "arbitrary"),
        name="proj",
    )(h, mod, g, w)


def _qkv_kernel(h_ref, mod_ref, g_ref, w_ref, cos_ref, sin_ref, o_ref, u_scr, gs_scr, *, tm, n_lat, rope_cols):
    _adaln_rows(h_ref, mod_ref, g_ref, u_scr, gs_scr, sub=1, row0=pl.program_id(0) * tm, tm=tm, n_lat=n_lat)
    u = u_scr[...]
    cos, sin = cos_ref[...], sin_ref[...]
    for c0 in range(0, o_ref.shape[1], QKV_SLAB):
        y = _dot(u, w_ref[:, c0:c0 + QKV_SLAB])
        for hd in range(0, QKV_SLAB, HEAD_DIM):
            x = y[:, hd:hd + HEAD_DIM]
            if c0 + hd < rope_cols:
                x = _rope(x, cos, sin)
            o_ref[:, c0 + hd:c0 + hd + HEAD_DIM] = x.astype(BF16)


def _qkv_proj(h, mod, g, w, cos, sin, *, layer, tm, n_lat, rope_cols):
    rows, d = h.shape
    n = w.shape[2]
    return pl.pallas_call(
        functools.partial(_qkv_kernel, tm=tm, n_lat=n_lat, rope_cols=rope_cols),
        grid=(rows // tm,),
        in_specs=[
            pl.BlockSpec((tm, d), lambda i: (i, 0)),
            pl.BlockSpec((9, 8, d), lambda i: (0, 0, 0)),
            pl.BlockSpec((1, d), lambda i: (0, 0)),
            pl.BlockSpec((None, d, n), lambda i: (layer, 0, 0), pipeline_mode=pl.Buffered(1)),
            pl.BlockSpec((tm, HEAD_DIM), lambda i: (i, 0)),
            pl.BlockSpec((tm, HEAD_DIM), lambda i: (i, 0)),
        ],
        out_specs=pl.BlockSpec((tm, n), lambda i: (i, 0)),
        out_shape=jax.ShapeDtypeStruct((rows, n), BF16),
        scratch_shapes=[pltpu.VMEM((tm, d), BF16), pltpu.VMEM((2, 1, d), F32)],
        compiler_params=_params("arbitrary"),
        name="qkv_proj",
    )(h, mod, g, w, cos, sin)


def _rope_tables(n_lat, n_ctx):
    t = np.arange(n_lat)
    inv = (ROPE_THETA ** (-np.arange(0, 64, 2, dtype=np.float32) / 64)).astype(np.float32)
    ang_r = (t // GRID_W).astype(np.float32)[:, None] * inv[None, :]
    ang_c = (t % GRID_W).astype(np.float32)[:, None] * inv[None, :]
    cr, sr, cc, sc = np.cos(ang_r), np.sin(ang_r), np.cos(ang_c), np.sin(ang_c)
    cos = np.concatenate([cr, cr, cc, cc], axis=1)
    sin = np.concatenate([-sr, sr, -sc, sc], axis=1)
    cos = np.concatenate([cos, np.ones((n_ctx, HEAD_DIM), np.float32)], axis=0)
    sin = np.concatenate([sin, np.zeros((n_ctx, HEAD_DIM), np.float32)], axis=0)
    return jnp.asarray(cos, F32), jnp.asarray(sin, F32)


def _oproj_kernel(a1_ref, a2_ref, w_ref, h_ref, mod_ref, o_ref, *, tm, n_lat, k1):
    is_ctx = _is_ctx(pl.program_id(0), tm, n_lat)
    y = _dot(a1_ref[...], w_ref[:k1, :]) + _dot(a2_ref[...], w_ref[k1:, :])
    o_ref[...] = h_ref[...] + _mod_rows(mod_ref, 1, 2, is_ctx) * y


def _oproj(a1, a2, col2, w, h, mod, *, layer, tm, n_lat):
    rows, d = h.shape
    k1 = w.shape[1] // 2
    return pl.pallas_call(
        functools.partial(_oproj_kernel, tm=tm, n_lat=n_lat, k1=k1),
        grid=(rows // tm,),
        in_specs=[
            pl.BlockSpec((tm, k1), lambda i: (i, 0)),
            pl.BlockSpec((tm, k1), lambda i: (i, col2)),
            pl.BlockSpec((None,) + w.shape[1:], lambda i: (layer, 0, 0)),
            pl.BlockSpec((tm, d), lambda i: (i, 0)),
            pl.BlockSpec((9, 8, d), lambda i: (0, 0, 0)),
        ],
        out_specs=pl.BlockSpec((tm, d), lambda i: (i, 0)),
        out_shape=jax.ShapeDtypeStruct((rows, d), F32),
        compiler_params=_params("arbitrary"),
        name="oproj",
    )(a1, a2, w, h, mod)


def _oproj_hgrn_kernel(a1_ref, of_ref, ob_ref, g_ref, ng_ref, w_ref, h_ref, mod_ref, o_ref, yb_scr,
                       *, tm, n_lat, k1):
    for c0 in range(0, k1, HEAD_DIM):
        sl = slice(c0, c0 + HEAD_DIM)
        o = of_ref[:, sl] + ob_ref[:, sl]
        o = o * lax.rsqrt(jnp.mean(o * o, axis=-1, keepdims=True) + NORM_EPS)
        yb_scr[:, sl] = (o * ng_ref[:, sl] * _silu(g_ref[:, sl])).astype(BF16)
    is_ctx = _is_ctx(pl.program_id(0), tm, n_lat)
    y = _dot(a1_ref[...], w_ref[:k1, :]) + _dot(yb_scr[...], w_ref[k1:, :])
    o_ref[...] = h_ref[...] + _mod_rows(mod_ref, 1, 2, is_ctx) * y


def _oproj_hgrn(a1, o_f, o_b, p32, gcol0, norm_g, w, h, mod, *, layer, tm, n_lat):
    rows, d = h.shape
    k1 = w.shape[1] // 2
    half = lambda i: (i, 0)
    return pl.pallas_call(
        functools.partial(_oproj_hgrn_kernel, tm=tm, n_lat=n_lat, k1=k1),
        grid=(rows // tm,),
        in_specs=[
            pl.BlockSpec((tm, k1), half),
            pl.BlockSpec((tm, k1), half),
            pl.BlockSpec((tm, k1), half),
            pl.BlockSpec((tm, k1), lambda i: (i, gcol0 // k1)),
            pl.BlockSpec((1, k1), lambda i: (0, 0)),
            pl.BlockSpec((None,) + w.shape[1:], lambda i: (layer, 0, 0)),
            pl.BlockSpec((tm, d), half),
            pl.BlockSpec((9, 8, d), lambda i: (0, 0, 0)),
        ],
        out_specs=pl.BlockSpec((tm, d), half),
        out_shape=jax.ShapeDtypeStruct((rows, d), F32),
        scratch_shapes=[pltpu.VMEM((tm, k1), BF16)],
        compiler_params=_params("arbitrary"),
        name="oproj_hgrn",
    )(a1, o_f, o_b, p32, norm_g, w, h, mod)


def _na_bias(rpb, n_rows):
    nh, n_a, n_b = rpb.shape
    nb = n_rows // NA_Q_ROWS
    w = jnp.full((nh, n_a, 128), MASKED, F32)
    w = w.at[..., :NA_COLS].set(rpb[..., NA_COLS - 1:]).at[..., 128 - (NA_COLS - 1):].set(rpb[..., :NA_COLS - 1])
    toep = jnp.tile(w, (1, 1, GRID_W))[..., :GRID_W * 127].reshape(nh, n_a, GRID_W, 127)[..., :GRID_W]
    toep = jnp.pad(toep.transpose(0, 2, 1, 3), ((0, 0), (0, 0), (NA_K_ROWS, NA_K_ROWS), (0, 0)),
                   constant_values=MASKED).reshape(nh, GRID_W, -1)

    q = np.arange(ATT_BLOCK)
    k = np.arange(NA_K_ROWS * GRID_W)
    dr, c = q // GRID_W, q % GRID_W
    kr, kc = k // GRID_W, k % GRID_W
    c0 = np.clip(c - NA_COLS // 2, 0, GRID_W - NA_COLS)
    col_ok = (kc[None, :] >= c0[:, None]) & (kc[None, :] < c0[:, None] + NA_COLS)
    tables, ok = [], []
    for b in (0, 1, nb - 1):
        r = b * NA_Q_ROWS + dr
        r0 = np.clip(r - NA_ROWS // 2, 0, n_rows - NA_ROWS)
        base = int(np.clip(b * NA_Q_ROWS - NA_ROWS // 2, 0, n_rows - NA_K_ROWS))
        kabs = base + kr
        ok.append((kabs[None, :] >= r0[:, None]) & (kabs[None, :] < r0[:, None] + NA_ROWS) & col_ok)
        per_row = []
        for j in range(NA_Q_ROWS):
            lo = base - (b * NA_Q_ROWS + j) + NA_ROWS - 1 + NA_K_ROWS
            per_row.append(toep[:, :, lo * GRID_W:(lo + NA_K_ROWS) * GRID_W])
        tables.append(jnp.concatenate(per_row, axis=1))
    return jnp.where(np.stack(ok)[:, None], jnp.stack(tables), MASKED)


def _na_kernel(q_ref, k_ref, v_ref, bias_ref, o_ref, *, n_lat, n_ctx, nb):
    b = pl.program_id(1)
    scale = HEAD_DIM ** -0.5
    nk = NA_K_ROWS * GRID_W
    start = pl.multiple_of(jnp.clip(b - 1, 0, nb - 3) * ATT_BLOCK, ATT_BLOCK)

    def one_head(hh, local):
        cols = slice(hh * HEAD_DIM, (hh + 1) * HEAD_DIM)
        q = q_ref[:, cols]
        s_ctx = _dot_nt(q, k_ref[n_lat:n_lat + n_ctx, cols]) * scale
        vc1 = _with_ones(v_ref[n_lat:n_lat + n_ctx, cols])
        if local:
            kl = k_ref[pl.ds(start, nk), cols]
            vl1 = _with_ones(v_ref[pl.ds(start, nk), cols])
            s_loc = _dot_nt(q, kl) * scale + bias_ref[0, hh]
            m = _row_max(s_loc, s_ctx)
            acc = _dot(jnp.exp(s_loc - m).astype(BF16), vl1) + _dot(jnp.exp(s_ctx - m).astype(BF16), vc1)
        else:
            acc = _dot(jnp.exp(s_ctx - _row_max(s_ctx)).astype(BF16), vc1)
        o_ref[:, cols] = (acc[:, :HEAD_DIM] / acc[:, HEAD_DIM:]).astype(o_ref.dtype)

    @pl.when(b < nb)
    def _():
        for hh in range(NA_HEADS_PER_STEP):
            one_head(hh, True)

    @pl.when(b >= nb)
    def _():
        for hh in range(NA_HEADS_PER_STEP):
            one_head(hh, False)


def _na_attention(p, bias, *, layer, n_lat, n_ctx, heads):
    rows = n_lat + n_ctx
    nb = n_lat // ATT_BLOCK
    nblk = rows // ATT_BLOCK
    hps = NA_HEADS_PER_STEP
    width = hps * HEAD_DIM
    groups = heads // hps

    def pattern(b):
        return jnp.where(b == 0, 0, jnp.where(b == nb - 1, 2, 1))

    return pl.pallas_call(
        functools.partial(_na_kernel, n_lat=n_lat, n_ctx=n_ctx, nb=nb),
        grid=(groups, nblk),
        in_specs=[
            pl.BlockSpec((ATT_BLOCK, width), lambda h, b: (b, h)),
            pl.BlockSpec((rows, width), lambda h, b: (0, groups + h)),
            pl.BlockSpec((rows, width), lambda h, b: (0, 2 * groups + h)),
            pl.BlockSpec((1, hps, ATT_BLOCK, NA_K_ROWS * GRID_W),
                         lambda h, b: (pattern(b), layer * groups + h, 0, 0)),
        ],
        out_specs=pl.BlockSpec((ATT_BLOCK, width), lambda h, b: (b, h)),
        out_shape=jax.ShapeDtypeStruct((rows, heads * HEAD_DIM), BF16),
        compiler_params=_params("arbitrary", "arbitrary"),
        name="na_attention",
    )(p, p, p, bias)


def _hgrn_levels():
    levels, m = [], 1
    while m < HG_CHUNK:
        levels.append(m)
        m *= 2
    return levels


def _hgrn_consts():
    c = HG_CHUNK
    levels = _hgrn_levels()
    fine = [m for m in levels if m < HG_ROW_LEVEL]
    idx = np.arange(c)
    t, j = idx[:, None], idx[None, :]
    n = np.zeros((2, 1 + len(fine), c, c), np.float32)
    msk = np.zeros((2, 1 + len(levels), c, c), np.float32)
    n[0, 0], n[1, 0] = j <= t, j >= t
    msk[0, 0] = msk[1, 0] = np.eye(c)
    for li, m in enumerate(levels):
        seg = idx // (2 * m)
        right = ((idx % (2 * m)) >= m)[:, None]
        last_left = (seg * 2 * m + m - 1)[:, None]
        first_right = last_left + 1
        if m < HG_ROW_LEVEL:
            n[0, 1 + li] = np.where(right, (j > last_left) & (j <= t), (j > t) & (j <= last_left))
            n[1, 1 + li] = np.where(right, (j >= first_right) & (j < t), (j >= t) & (j < first_right))
        same = seg[:, None] == seg[None, :]
        msk[0, 1 + li] = same & right & ~right.T
        msk[1, 1 + li] = same & ~right & right.T
    return n.reshape(2, -1, c), msk


def _hgrn_chunk(d, q, fx, v, lb, n_ref, msk_ref, s_scr, b_scr):
    c = HG_CHUNK
    t = jnp.exp(-jnp.abs(fx))
    r = 1.0 / (1.0 + t)
    log_sig = jnp.minimum(fx, 0.0) + jnp.log(r)
    sig_neg = jnp.where(fx >= 0.0, t * r, r)
    if lb is None:
        log_f, kk = log_sig, sig_neg
    else:
        la = jnp.log(lb)
        lc = jnp.log1p(-lb) + log_sig
        log_f = jnp.maximum(la, lc) + jnp.log(1.0 + jnp.exp(-jnp.abs(la - lc)))
        kk = (1.0 - lb) * sig_neg
    qf = q.astype(F32)

    hi = log_f.astype(BF16)
    mid = (log_f - hi.astype(F32)).astype(BF16)
    dd = _dot(n_ref[d], jnp.concatenate([hi, mid], axis=1))
    dd = dd[:, :HEAD_DIM] + dd[:, HEAD_DIM:]
    b = dd[:c]
    b_scr[d] = b

    attn = _dot_nt(q, kk.astype(BF16)) * msk_ref[d, 0]
    for li, m in enumerate(_hgrn_levels()):
        if m < HG_ROW_LEVEL:
            ex = dd[(1 + li) * c:(2 + li) * c]
        else:
            row = m - 1 + d
            ref = jnp.concatenate(
                [jnp.broadcast_to(b_scr[d, s + row:s + row + 1, :], (2 * m, HEAD_DIM)) for s in range(0, c, 2 * m)],
                axis=0)
            ex = -jnp.abs(b - ref)
        e = jnp.exp(ex)
        attn += _dot_nt((qf * e).astype(BF16), (kk * e).astype(BF16)) * msk_ref[d, 1 + li]

    total = b[c - 1:c] if d == 0 else b[0:1]
    state_t = s_scr[d]
    o = _dot_nt((qf * jnp.exp(b)).astype(BF16), state_t.astype(BF16)) + _dot(attn.astype(BF16), v)
    v_t = v.astype(F32).T.astype(BF16)
    s_scr[d] = jnp.exp(total) * state_t + _dot(v_t, (kk * jnp.exp(total - b)).astype(BF16))
    return o


def _hgrn_kernel(qf_ref, qb_ref, ff_ref, fb_ref, if_ref, ib_ref, lbl_ref, n_ref, msk_ref,
                 of_ref, ob_ref, s_scr, b_scr, *, layer_e):
    @pl.when(pl.program_id(1) == 0)
    def _():
        s_scr[...] = jnp.zeros_like(s_scr)

    refs = ((qf_ref, ff_ref, if_ref, of_ref), (qb_ref, fb_ref, ib_ref, ob_ref))
    lbs = [None, None]
    if layer_e > 0:
        for d in range(2):
            lg = lbl_ref[d]
            ex = jnp.exp(lg - jnp.max(lg, axis=0, keepdims=True))
            num = ex[1:2]
            for e in range(2, layer_e + 1):
                num = num + ex[e:e + 1]
            lbs[d] = num / jnp.sum(ex, axis=0, keepdims=True)

    n_sub = HG_STEP // HG_CHUNK
    for k in range(n_sub):
        for hh in range(HG_HEADS_PER_STEP):
            cols = slice(hh * HEAD_DIM, (hh + 1) * HEAD_DIM)
            for d, (q_ref, f_ref, i_ref, o_ref) in enumerate(refs):
                sub = k if d == 0 else n_sub - 1 - k
                rows = slice(sub * HG_CHUNK, (sub + 1) * HG_CHUNK)
                lb = None if lbs[d] is None else lbs[d][:, cols]
                o_ref[rows, cols] = _hgrn_chunk(d, q_ref[rows, cols], f_ref[rows, cols], i_ref[rows, cols], lb,
                                                n_ref, msk_ref, s_scr.at[hh], b_scr.at[hh, k])


def _hgrn_scan(p16, p32, lb_logits, consts, *, layer_e, n_lat, n_ctx, heads, qcol, icol):
    rows = n_lat + n_ctx
    n_mat, msk = consts
    lat_chunks = n_lat // HG_STEP
    qc, ic = qcol // HEAD_DIM, icol // HEAD_DIM

    def fwd(s):
        return jnp.where(s == 0, lat_chunks, s - 1)

    def bwd(s):
        return lat_chunks - s

    hps = HG_HEADS_PER_STEP
    width = hps * HEAD_DIM

    def spec(blk, col):
        return pl.BlockSpec((HG_STEP, width), lambda h, s: (blk(s), col // hps + h))

    out = jax.ShapeDtypeStruct((rows, heads * HEAD_DIM), F32)
    return pl.pallas_call(
        functools.partial(_hgrn_kernel, layer_e=layer_e),
        grid=(heads // hps, 1 + lat_chunks),
        in_specs=[
            spec(fwd, qc), spec(bwd, qc),
            spec(fwd, 0), spec(bwd, heads),
            spec(fwd, ic), spec(bwd, ic),
            pl.BlockSpec((2, lb_logits.shape[1], width), lambda h, s: (0, 0, h)),
            pl.BlockSpec(n_mat.shape, lambda h, s: (0, 0, 0)),
            pl.BlockSpec(msk.shape, lambda h, s: (0, 0, 0, 0)),
        ],
        out_specs=[spec(fwd, 0), spec(bwd, 0)],
        out_shape=[out, out],
        scratch_shapes=[pltpu.VMEM((hps, 2, HEAD_DIM, HEAD_DIM), F32),
                        pltpu.VMEM((hps, HG_STEP // HG_CHUNK, 2, HG_CHUNK, HEAD_DIM), F32)],
        compiler_params=_params("arbitrary", "arbitrary"),
        name="hgrn_scan",
    )(p16, p16, p32, p32, p16, p16, lb_logits, n_mat, msk)


def _swa_kernel(q_ref, k_ref, v_ref, sink_ref, bias_ref, o_ref, *, n_lat, n_ctx, nb, group, span):
    b = pl.program_id(1)
    scale = HEAD_DIM ** -0.5
    start = pl.multiple_of(jnp.clip(b * ATT_BLOCK - WINDOW, 0, n_lat - span), WINDOW)

    def one_kv_head(kv, local):
        kcols = slice(kv * HEAD_DIM, (kv + 1) * HEAD_DIM)
        kc = k_ref[n_lat:n_lat + n_ctx, kcols]
        vc1 = _with_ones(v_ref[n_lat:n_lat + n_ctx, kcols])
        if local:
            kl = k_ref[pl.ds(start, span), kcols]
            vl1 = _with_ones(v_ref[pl.ds(start, span), kcols])
        for g in range(group):
            qcols = slice((kv * group + g) * HEAD_DIM, (kv * group + g + 1) * HEAD_DIM)
            q = q_ref[:, qcols]
            sink = sink_ref[kv, g:g + 1, 0:1]
            s_ctx = _dot_nt(q, kc) * scale
            if local:
                s_loc = _dot_nt(q, kl) * scale + bias_ref[0]
                m = jnp.maximum(_row_max(s_ctx, s_loc), sink)
                acc = _dot(jnp.exp(s_ctx - m).astype(BF16), vc1) + _dot(jnp.exp(s_loc - m).astype(BF16), vl1)
            else:
                m = jnp.maximum(_row_max(s_ctx), sink)
                acc = _dot(jnp.exp(s_ctx - m).astype(BF16), vc1)
            den = acc[:, HEAD_DIM:] + jnp.exp(sink - m)
            o_ref[:, qcols] = (acc[:, :HEAD_DIM] / den).astype(o_ref.dtype)

    @pl.when(b < nb)
    def _():
        for kv in range(SWA_KV_PER_STEP):
            one_kv_head(kv, True)

    @pl.when(b >= nb)
    def _():
        for kv in range(SWA_KV_PER_STEP):
            one_kv_head(kv, False)


def _swa_bias(n_lat):
    span = ATT_BLOCK + 2 * WINDOW
    nb = n_lat // ATT_BLOCK
    i = np.arange(ATT_BLOCK)[:, None]
    j = np.arange(span)[None, :]
    tables = []
    for b in (0, 1, nb - 1):
        start = int(np.clip(b * ATT_BLOCK - WINDOW, 0, n_lat - span))
        rel = (start + j) - (b * ATT_BLOCK + i)
        tables.append(np.where(np.abs(rel) <= WINDOW, 0.0, MASKED))
    return jnp.asarray(np.stack(tables), F32)


def _swa_attention(p, sink, *, n_lat, n_ctx, heads, kv_heads):
    rows = n_lat + n_ctx
    group = heads // kv_heads
    nb = n_lat // ATT_BLOCK
    span = ATT_BLOCK + 2 * WINDOW
    kps = SWA_KV_PER_STEP
    sink_tab = jnp.broadcast_to(sink.astype(F32).reshape(kv_heads, group, 1), (kv_heads, group, HEAD_DIM))

    def pattern(b):
        return jnp.where(b == 0, 0, jnp.where(b == nb - 1, 2, 1))

    return pl.pallas_call(
        functools.partial(_swa_kernel, n_lat=n_lat, n_ctx=n_ctx, nb=nb, group=group, span=span),
        grid=(kv_heads // kps, rows // ATT_BLOCK),
        in_specs=[
            pl.BlockSpec((ATT_BLOCK, kps * group * HEAD_DIM), lambda k, b: (b, k)),
            pl.BlockSpec((rows, kps * HEAD_DIM), lambda k, b: (0, heads // kps + k)),
            pl.BlockSpec((rows, kps * HEAD_DIM), lambda k, b: (0, (heads + kv_heads) // kps + k)),
            pl.BlockSpec((kps, group, HEAD_DIM), lambda k, b: (k, 0, 0)),
            pl.BlockSpec((1, ATT_BLOCK, span), lambda k, b: (pattern(b), 0, 0)),
        ],
        out_specs=pl.BlockSpec((ATT_BLOCK, kps * group * HEAD_DIM), lambda k, b: (b, k)),
        out_shape=jax.ShapeDtypeStruct((rows, heads * HEAD_DIM), BF16),
        compiler_params=_params("arbitrary", "arbitrary"),
        name="swa_attention",
    )(p, p, p, sink_tab, _swa_bias(n_lat))


def kernel(x, c, ctx, c_ctx, w_mod, b_mod, norm_g, w_ff_in, w_ff_out, w_in_even, w_out_even,
           na_rpb, hg_lb_logits, hg_norm_g, w_qkv_odd, w_o_odd, sink_odd, final_norm_g):
    assert x.shape[0] == 1
    depth, d = w_mod.shape[0], x.shape[2]
    n_lat, n_ctx = x.shape[1], ctx.shape[1]
    rows = n_lat + n_ctx
    tm = 768
    assert rows % tm == 0 and n_lat % 1024 == 0 and n_ctx == ATT_BLOCK
    na_heads = na_rpb.shape[1]
    a_w = na_heads * HEAD_DIM
    hg_heads = hg_norm_g.shape[1] // HEAD_DIM
    heads = sink_odd.shape[1]
    kv_heads = (w_qkv_odd.shape[2] // HEAD_DIM - heads) // 2

    cc = jnp.zeros((8, d), F32).at[0].set(c[0]).at[1].set(c_ctx)
    mod_all = _modulation(cc, w_mod, b_mod)

    cos, sin = _rope_tables(n_lat, n_ctx)
    na_bias = _na_bias(na_rpb.reshape((-1,) + na_rpb.shape[2:]), n_lat // GRID_W)
    hg_consts = _hgrn_consts()
    hg_consts = (jnp.asarray(hg_consts[0], BF16), jnp.asarray(hg_consts[1], F32))
    fg = final_norm_g.reshape(1, d)
    b_w = hg_heads * HEAD_DIM
    c_f, c_i, c_g = 3 * a_w + b_w, 3 * a_w + 3 * b_w, 3 * a_w + 4 * b_w
    w_in_even = jnp.concatenate([w_in_even[:, :, :c_f], w_in_even[:, :, c_i:c_g], w_in_even[:, :, c_f:c_i],
                                 w_in_even[:, :, c_g:]], axis=2).astype(BF16)
    w_out_even = w_out_even.astype(BF16)
    w_qkv_odd, w_o_odd = w_qkv_odd.astype(BF16), w_o_odd.astype(BF16)

    h = x.reshape(n_lat, d)
    ffn_tm = FFN_ROWS
    ffn_tail = rows % ffn_tm or ffn_tm
    h_tail = jnp.concatenate([h[rows - ffn_tail:], ctx.reshape(n_ctx, d)], axis=0)
    for l in range(depth):
        mod = mod_all[l]
        last = l == depth - 1
        h = _ffn(h, mod, norm_g[l, 0].reshape(1, d), w_ff_in, w_ff_out, fg, layer=l, half=0,
                 sub=0, rows=rows, tm=ffn_tm, n_lat=n_lat, h_tail=h_tail if l == 0 else None)
        g1 = norm_g[l, 1].reshape(1, d)
        if l % 2 == 0:
            e = l // 2
            p16, p32 = _proj(h, mod, g1, w_in_even, layer=e, tm=tm, tn=1024, n_lat=n_lat, cols16=3 * a_w + 2 * b_w)
            ya = _na_attention(p16, na_bias, layer=e, n_lat=n_lat, n_ctx=n_ctx, heads=na_heads)
            o_f, o_b = _hgrn_scan(p16, p32, hg_lb_logits, hg_consts, layer_e=e, n_lat=n_lat, n_ctx=n_ctx,
                                  heads=hg_heads, qcol=3 * a_w, icol=3 * a_w + b_w)
            h = _oproj_hgrn(ya, o_f, o_b, p32, 2 * b_w, hg_norm_g[e].reshape(1, -1), w_out_even, h, mod,
                            layer=e, tm=tm // 2, n_lat=n_lat)
        else:
            o = l // 2
            p16 = _qkv_proj(h, mod, g1, w_qkv_odd, cos, sin, layer=o, tm=tm, n_lat=n_lat,
                            rope_cols=(heads + kv_heads) * HEAD_DIM)
            y = _swa_attention(p16, sink_odd[o], n_lat=n_lat, n_ctx=n_ctx, heads=heads, kv_heads=kv_heads)
            h = _oproj(y, y, 1, w_o_odd, h, mod, layer=o, tm=tm, n_lat=n_lat)
        g2 = norm_g[l, 2].reshape(1, d)
        if last:
            h = _ffn(h, mod, g2, w_ff_in, w_ff_out, fg, layer=l, half=1,
                     sub=2, rows=n_lat, tm=ffn_tm, n_lat=n_lat, final_norm=True)
        else:
            h = _ffn(h, mod, g2, w_ff_in, w_ff_out, fg, layer=l, half=1,
                     sub=2, rows=rows, tm=ffn_tm, n_lat=n_lat)
    return h.reshape(1, n_lat, d)
```

```python
import functools

import numpy as np
import jax
import jax.numpy as jnp
from jax import lax
from jax.experimental import pallas as pl
from jax.experimental.pallas import tpu as pltpu

F32 = jnp.float32
BF16 = jnp.bfloat16

GRID_W = 64
NORM_EPS = 1e-6
ROPE_THETA = 10000.0
HEAD_DIM = 128
NA_ROWS = 8
NA_COLS = 16
NA_Q_ROWS = 4
NA_K_ROWS = NA_Q_ROWS + NA_ROWS
ATT_BLOCK = NA_Q_ROWS * GRID_W
NA_HEADS_PER_STEP = 4
SWA_KV_PER_STEP = 4
WINDOW = 128
HG_STEP = 256
HG_CHUNK = 128
HG_HEADS_PER_STEP = 4
ROW_CHUNK = 16
FFN_ROWS = 960
FFN_COLS = 512
FFN_TAIL_COLS = 256
QKV_SLAB = 512
FFN_OUT_COLS = 512
ROW_UNROLL = 4
HG_ROW_LEVEL = 8
MASKED = -1e30
VMEM_LIMIT = 56 * 1024 * 1024


def _params(*sem):
    return pltpu.CompilerParams(dimension_semantics=sem, vmem_limit_bytes=VMEM_LIMIT)


def _dot(a, b):
    return jnp.dot(a, b, preferred_element_type=F32)


def _dot_nt(a, b):
    return lax.dot_general(a, b, (((1,), (1,)), ((), ())), preferred_element_type=F32)


def _sigmoid(x):
    return 1.0 / (1.0 + jnp.exp(-x))


def _silu(x):
    return x * _sigmoid(x)


def _mod_kernel(c_ref, w_ref, b_ref, o_ref):
    s = _silu(c_ref[...]).astype(BF16)
    o_ref[0, 0] = _dot(s, w_ref[0].astype(BF16)) + b_ref[0]


def _modulation(cc, w_mod, b_mod):
    depth, d, n = w_mod.shape
    return pl.pallas_call(
        _mod_kernel,
        grid=(depth, n // d),
        in_specs=[
            pl.BlockSpec((8, d), lambda l, j: (0, 0)),
            pl.BlockSpec((1, d, d), lambda l, j: (l, 0, j)),
            pl.BlockSpec((1, 1, d), lambda l, j: (l, 0, j)),
        ],
        out_specs=pl.BlockSpec((1, 1, 8, d), lambda l, j: (l, j, 0, 0)),
        out_shape=jax.ShapeDtypeStruct((depth, n // d, 8, d), F32),
        compiler_params=_params("arbitrary", "arbitrary"),
        name="modulation",
    )(cc, w_mod, b_mod.reshape(depth, 1, n))


def _mod_rows(mod_ref, sub, k, is_ctx):
    r = 3 * sub + k
    return jnp.where(is_ctx, mod_ref[r, 1:2, :], mod_ref[r, 0:1, :])


def _row_max(*blocks):
    acc = None
    for s in blocks:
        for c0 in range(0, s.shape[1], HEAD_DIM):
            t = s[:, c0:c0 + HEAD_DIM]
            acc = t if acc is None else jnp.maximum(acc, t)
    return jnp.max(acc, axis=-1, keepdims=True)


def _with_ones(v):
    return jnp.concatenate([v, jnp.ones_like(v)], axis=1)


def _is_ctx(tile, tm, n_lat):
    row = tile * tm + lax.broadcasted_iota(jnp.int32, (tm, 1), 0)
    return row >= n_lat


def _row_chunks(row0, tm, n_lat, body):
    def step(c, carry):
        r0 = pl.multiple_of(c * ROW_CHUNK, ROW_CHUNK)
        body(pl.ds(r0, ROW_CHUNK), (row0 + r0 >= n_lat).astype(jnp.int32))
        return carry

    lax.fori_loop(0, tm // ROW_CHUNK, step, 0, unroll=ROW_UNROLL)


def _adaln_rows(h_ref, mod_ref, g_ref, u_scr, gs_scr, *, sub, row0, tm, n_lat):
    for which in range(2):
        gs_scr[which] = g_ref[...] * (1.0 + mod_ref[3 * sub + 1, which:which + 1, :])

    def body(rows, which):
        x = h_ref[rows, :]
        r = lax.rsqrt(jnp.mean(x * x, axis=-1, keepdims=True) + NORM_EPS)
        u_scr[rows, :] = ((x * r) * gs_scr[which] + mod_ref[3 * sub, pl.ds(which, 1), :]).astype(BF16)

    _row_chunks(row0, tm, n_lat, body)


def _ffn_kernel(h_ref, mod_ref, g_ref, wg_ref, wu_ref, wo_ref, fg_ref, *rest,
                sub, tm, row0, n_lat, final_norm, emit16, aliased):
    rest = rest[1:] if aliased else rest
    o_ref = rest[0]
    u_scr, gs_scr = rest[-2:]
    i, j = pl.program_id(0), pl.program_id(1)
    first_row = row0 + i * tm

    @pl.when(j == 0)
    def _():
        _adaln_rows(h_ref, mod_ref, g_ref, u_scr, gs_scr, sub=sub, row0=first_row, tm=tm, n_lat=n_lat)
        o_ref[...] = jnp.zeros_like(o_ref)

    if emit16:
        wg16_ref, wu16_ref, wo16_ref = rest[1:4]
        wg16_ref[...] = wg_ref[...].astype(BF16)
        wu16_ref[...] = wu_ref[...].astype(BF16)
        wo16_ref[...] = wo_ref[...].astype(BF16)
        wg_ref, wu_ref, wo_ref = wg16_ref, wu16_ref, wo16_ref
    u = u_scr[...]
    a = (_silu(_dot(u, wg_ref[...])) * _dot(u, wu_ref[...])).astype(BF16)
    d = o_ref.shape[1]
    for c0 in range(0, d, FFN_OUT_COLS):
        o_ref[:, c0:c0 + FFN_OUT_COLS] += _dot(a, wo_ref[:, c0:c0 + FFN_OUT_COLS])

    @pl.when(j == pl.num_programs(1) - 1)
    def _():
        def body(rows, which):
            h = h_ref[rows, :] + (0.5 * mod_ref[3 * sub + 2, pl.ds(which, 1), :]) * o_ref[rows, :]
            if final_norm:
                h = h * lax.rsqrt(jnp.mean(h * h, axis=-1, keepdims=True) + NORM_EPS) * fg_ref[...]
            o_ref[rows, :] = h

        _row_chunks(first_row, tm, n_lat, body)


def _ffn(h, mod, g, w_in, w_out, fg, *, layer, half, sub, rows, tm, n_lat, final_norm=False, h_tail=None):
    d = h.shape[1]
    f = w_out.shape[2]
    tail = rows % tm or tm
    n_main = (rows - tail) // tm
    assert (rows - tail) % tail == 0
    tail_src, tail_blk = (h, (rows - tail) // tail) if h_tail is None else (h_tail, 0)
    common = dict(sub=sub, n_lat=n_lat, final_norm=final_norm)
    small = [
        pl.BlockSpec((9, 8, d), lambda i, j: (0, 0, 0)),
        pl.BlockSpec((1, d), lambda i, j: (0, 0)),
    ]
    fg_spec = pl.BlockSpec((1, d), lambda i, j: (0, 0))

    tf, t0 = FFN_TAIL_COLS, (rows - tail) // tail
    nf = f // tf
    out, wg16, wu16, wo16 = pl.pallas_call(
        functools.partial(_ffn_kernel, tm=tail, row0=rows - tail, emit16=True, aliased=False, **common),
        grid=(1, nf),
        in_specs=[pl.BlockSpec((tail, d), lambda i, j: (tail_blk, 0))] + small + [
            pl.BlockSpec((None, None, d, tf), lambda i, j: (layer, half, 0, j)),
            pl.BlockSpec((None, None, d, tf), lambda i, j: (layer, half, 0, nf + j)),
            pl.BlockSpec((None, None, tf, d), lambda i, j: (layer, half, j, 0)),
            fg_spec,
        ],
        out_specs=[
            pl.BlockSpec((tail, d), lambda i, j: (t0, 0)),
            pl.BlockSpec((d, tf), lambda i, j: (0, j)),
            pl.BlockSpec((d, tf), lambda i, j: (0, j)),
            pl.BlockSpec((tf, d), lambda i, j: (j, 0)),
        ],
        out_shape=[
            jax.ShapeDtypeStruct((rows, d), F32),
            jax.ShapeDtypeStruct((d, f), BF16),
            jax.ShapeDtypeStruct((d, f), BF16),
            jax.ShapeDtypeStruct((f, d), BF16),
        ],
        scratch_shapes=[pltpu.VMEM((tail, d), BF16), pltpu.VMEM((2, 1, d), F32)],
        compiler_params=_params("arbitrary", "arbitrary"),
        name="ffn_tail",
    )(tail_src, mod, g, w_in, w_in, w_out, fg)

    tf = FFN_COLS
    return pl.pallas_call(
        functools.partial(_ffn_kernel, tm=tm, row0=0, emit16=False, aliased=True, **common),
        grid=(n_main, f // tf),
        in_specs=[pl.BlockSpec((tm, d), lambda i, j: (i, 0))] + small + [
            pl.BlockSpec((d, tf), lambda i, j: (0, j)),
            pl.BlockSpec((d, tf), lambda i, j: (0, j)),
            pl.BlockSpec((tf, d), lambda i, j: (j, 0)),
            fg_spec,
            pl.BlockSpec(memory_space=pl.ANY),
        ],
        out_specs=pl.BlockSpec((tm, d), lambda i, j: (i, 0)),
        out_shape=jax.ShapeDtypeStruct((rows, d), F32),
        input_output_aliases={7: 0},
        scratch_shapes=[pltpu.VMEM((tm, d), BF16), pltpu.VMEM((2, 1, d), F32)],
        compiler_params=_params("arbitrary", "arbitrary"),
        name="ffn",
    )(h, mod, g, wg16, wu16, wo16, fg, out)


def _rope(x, cos, sin):
    lane = lax.broadcasted_iota(jnp.int32, x.shape, 1)
    first = (lane % 64) < 32
    swapped = jnp.where(first, pltpu.roll(x, 96, 1), pltpu.roll(x, 32, 1))
    return x * cos + swapped * sin


def _proj_kernel(h_ref, mod_ref, g_ref, w_ref, o16_ref, o32_ref, u_scr, gs_scr, *, tm, n_lat, n16):
    i, j = pl.program_id(0), pl.program_id(1)

    @pl.when(j == 0)
    def _():
        _adaln_rows(h_ref, mod_ref, g_ref, u_scr, gs_scr, sub=1, row0=i * tm, tm=tm, n_lat=n_lat)

    y = _dot(u_scr[...], w_ref[...])

    @pl.when(j < n16)
    def _():
        o16_ref[...] = y.astype(BF16)

    @pl.when(j >= n16)
    def _():
        o32_ref[...] = y


def _proj(h, mod, g, w, *, layer, tm, tn, n_lat, cols16, col_blocks):
    rows, d = h.shape
    n = w.shape[2]
    n16 = cols16 // tn

    def src_block(j):
        blk = j
        for dst, src in enumerate(col_blocks):
            if dst != src:
                blk = jnp.where(j == dst, src, blk)
        return blk

    return pl.pallas_call(
        functools.partial(_proj_kernel, tm=tm, n_lat=n_lat, n16=n16),
        grid=(rows // tm, n // tn),
        in_specs=[
            pl.BlockSpec((tm, d), lambda i, j: (i, 0)),
            pl.BlockSpec((9, 8, d), lambda i, j: (0, 0, 0)),
            pl.BlockSpec((1, d), lambda i, j: (0, 0)),
            pl.BlockSpec((None, d, tn), lambda i, j: (layer, 0, src_block(j))),
        ],
        out_specs=[
            pl.BlockSpec((tm, tn), lambda i, j: (i, jnp.minimum(j, n16 - 1))),
            pl.BlockSpec((tm, tn), lambda i, j: (i, jnp.maximum(j - n16, 0))),
        ],
        out_shape=[jax.ShapeDtypeStruct((rows, cols16), BF16), jax.ShapeDtypeStruct((rows, n - cols16), F32)],
        scratch_shapes=[pltpu.VMEM((tm, d), BF16), pltpu.VMEM((2, 1, d), F32)],
        compiler_params=_params("arbitrary", "arbitrary"),
        name="proj",
    )(h, mod, g, w)


def _qkv_kernel(h_ref, mod_ref, g_ref, w_ref, cos_ref, sin_ref, o_ref, u_scr, gs_scr, *, tm, n_lat, rope_cols):
    _adaln_rows(h_ref, mod_ref, g_ref, u_scr, gs_scr, sub=1, row0=pl.program_id(0) * tm, tm=tm, n_lat=n_lat)
    u = u_scr[...]
    cos, sin = cos_ref[...], sin_ref[...]
    for c0 in range(0, o_ref.shape[1], QKV_SLAB):
        y = _dot(u, w_ref[:, c0:c0 + QKV_SLAB])
        for hd in range(0, QKV_SLAB, HEAD_DIM):
            x = y[:, hd:hd + HEAD_DIM]
            if c0 + hd < rope_cols:
                x = _rope(x, cos, sin)
            o_ref[:, c0 + hd:c0 + hd + HEAD_DIM] = x.astype(BF16)


def _qkv_proj(h, mod, g, w, cos, sin, *, layer, tm, n_lat, rope_cols):
    rows, d = h.shape
    n = w.shape[2]
    return pl.pallas_call(
        functools.partial(_qkv_kernel, tm=tm, n_lat=n_lat, rope_cols=rope_cols),
        grid=(rows // tm,),
        in_specs=[
            pl.BlockSpec((tm, d), lambda i: (i, 0)),
            pl.BlockSpec((9, 8, d), lambda i: (0, 0, 0)),
            pl.BlockSpec((1, d), lambda i: (0, 0)),
            pl.BlockSpec((None, d, n), lambda i: (layer, 0, 0), pipeline_mode=pl.Buffered(1)),
            pl.BlockSpec((tm, HEAD_DIM), lambda i: (i, 0)),
            pl.BlockSpec((tm, HEAD_DIM), lambda i: (i, 0)),
        ],
        out_specs=pl.BlockSpec((tm, n), lambda i: (i, 0)),
        out_shape=jax.ShapeDtypeStruct((rows, n), BF16),
        scratch_shapes=[pltpu.VMEM((tm, d), BF16), pltpu.VMEM((2, 1, d), F32)],
        compiler_params=_params("arbitrary"),
        name="qkv_proj",
    )(h, mod, g, w, cos, sin)


def _rope_tables(n_lat, n_ctx):
    t = np.arange(n_lat)
    inv = (ROPE_THETA ** (-np.arange(0, 64, 2, dtype=np.float32) / 64)).astype(np.float32)
    ang_r = (t // GRID_W).astype(np.float32)[:, None] * inv[None, :]
    ang_c = (t % GRID_W).astype(np.float32)[:, None] * inv[None, :]
    cr, sr, cc, sc = np.cos(ang_r), np.sin(ang_r), np.cos(ang_c), np.sin(ang_c)
    cos = np.concatenate([cr, cr, cc, cc], axis=1)
    sin = np.concatenate([-sr, sr, -sc, sc], axis=1)
    cos = np.concatenate([cos, np.ones((n_ctx, HEAD_DIM), np.float32)], axis=0)
    sin = np.concatenate([sin, np.zeros((n_ctx, HEAD_DIM), np.float32)], axis=0)
    return jnp.asarray(cos, F32), jnp.asarray(sin, F32)


def _oproj_kernel(a1_ref, a2_ref, w_ref, h_ref, mod_ref, o_ref, *, tm, n_lat, k1):
    is_ctx = _is_ctx(pl.program_id(0), tm, n_lat)
    y = _dot(a1_ref[...], w_ref[:k1, :]) + _dot(a2_ref[...], w_ref[k1:, :])
    o_ref[...] = h_ref[...] + _mod_rows(mod_ref, 1, 2, is_ctx) * y


def _oproj(a1, a2, col2, w, h, mod, *, layer, tm, n_lat):
    rows, d = h.shape
    k1 = w.shape[1] // 2
    return pl.pallas_call(
        functools.partial(_oproj_kernel, tm=tm, n_lat=n_lat, k1=k1),
        grid=(rows // tm,),
        in_specs=[
            pl.BlockSpec((tm, k1), lambda i: (i, 0)),
            pl.BlockSpec((tm, k1), lambda i: (i, col2)),
            pl.BlockSpec((None,) + w.shape[1:], lambda i: (layer, 0, 0)),
            pl.BlockSpec((tm, d), lambda i: (i, 0)),
            pl.BlockSpec((9, 8, d), lambda i: (0, 0, 0)),
        ],
        out_specs=pl.BlockSpec((tm, d), lambda i: (i, 0)),
        out_shape=jax.ShapeDtypeStruct((rows, d), F32),
        compiler_params=_params("arbitrary"),
        name="oproj",
    )(a1, a2, w, h, mod)


def _oproj_hgrn_kernel(a1_ref, of_ref, ob_ref, g_ref, ng_ref, w_ref, h_ref, mod_ref, o_ref, yb_scr,
                       *, tm, n_lat, k1):
    for c0 in range(0, k1, HEAD_DIM):
        sl = slice(c0, c0 + HEAD_DIM)
        o = of_ref[:, sl] + ob_ref[:, sl]
        o = o * lax.rsqrt(jnp.mean(o * o, axis=-1, keepdims=True) + NORM_EPS)
        yb_scr[:, sl] = (o * ng_ref[:, sl] * _silu(g_ref[:, sl])).astype(BF16)
    is_ctx = _is_ctx(pl.program_id(0), tm, n_lat)
    y = _dot(a1_ref[...], w_ref[:k1, :]) + _dot(yb_scr[...], w_ref[k1:, :])
    o_ref[...] = h_ref[...] + _mod_rows(mod_ref, 1, 2, is_ctx) * y


def _oproj_hgrn(a1, o_f, o_b, p32, gcol0, norm_g, w, h, mod, *, layer, tm, n_lat):
    rows, d = h.shape
    k1 = w.shape[1] // 2
    half = lambda i: (i, 0)
    return pl.pallas_call(
        functools.partial(_oproj_hgrn_kernel, tm=tm, n_lat=n_lat, k1=k1),
        grid=(rows // tm,),
        in_specs=[
            pl.BlockSpec((tm, k1), half),
            pl.BlockSpec((tm, k1), half),
            pl.BlockSpec((tm, k1), half),
            pl.BlockSpec((tm, k1), lambda i: (i, gcol0 // k1)),
            pl.BlockSpec((1, k1), lambda i: (0, 0)),
            pl.BlockSpec((None,) + w.shape[1:], lambda i: (layer, 0, 0)),
            pl.BlockSpec((tm, d), half),
            pl.BlockSpec((9, 8, d), lambda i: (0, 0, 0)),
        ],
        out_specs=pl.BlockSpec((tm, d), half),
        out_shape=jax.ShapeDtypeStruct((rows, d), F32),
        scratch_shapes=[pltpu.VMEM((tm, k1), BF16)],
        compiler_params=_params("arbitrary"),
        name="oproj_hgrn",
    )(a1, o_f, o_b, p32, norm_g, w, h, mod)


def _na_bias(rpb, n_rows):
    nh, n_a, n_b = rpb.shape
    nb = n_rows // NA_Q_ROWS
    w = jnp.full((nh, n_a, 128), MASKED, F32)
    w = w.at[..., :NA_COLS].set(rpb[..., NA_COLS - 1:]).at[..., 128 - (NA_COLS - 1):].set(rpb[..., :NA_COLS - 1])
    toep = jnp.tile(w, (1, 1, GRID_W))[..., :GRID_W * 127].reshape(nh, n_a, GRID_W, 127)[..., :GRID_W]
    toep = jnp.pad(toep.transpose(0, 2, 1, 3), ((0, 0), (0, 0), (NA_K_ROWS, NA_K_ROWS), (0, 0)),
                   constant_values=MASKED).reshape(nh, GRID_W, -1)

    q = np.arange(ATT_BLOCK)
    k = np.arange(NA_K_ROWS * GRID_W)
    dr, c = q // GRID_W, q % GRID_W
    kr, kc = k // GRID_W, k % GRID_W
    c0 = np.clip(c - NA_COLS // 2, 0, GRID_W - NA_COLS)
    col_ok = (kc[None, :] >= c0[:, None]) & (kc[None, :] < c0[:, None] + NA_COLS)
    tables, ok = [], []
    for b in (0, 1, nb - 1):
        r = b * NA_Q_ROWS + dr
        r0 = np.clip(r - NA_ROWS // 2, 0, n_rows - NA_ROWS)
        base = int(np.clip(b * NA_Q_ROWS - NA_ROWS // 2, 0, n_rows - NA_K_ROWS))
        kabs = base + kr
        ok.append((kabs[None, :] >= r0[:, None]) & (kabs[None, :] < r0[:, None] + NA_ROWS) & col_ok)
        per_row = []
        for j in range(NA_Q_ROWS):
            lo = base - (b * NA_Q_ROWS + j) + NA_ROWS - 1 + NA_K_ROWS
            per_row.append(toep[:, :, lo * GRID_W:(lo + NA_K_ROWS) * GRID_W])
        tables.append(jnp.concatenate(per_row, axis=1))
    return jnp.where(np.stack(ok)[:, None], jnp.stack(tables), MASKED)


def _na_kernel(q_ref, k_ref, v_ref, bias_ref, o_ref, *, n_lat, n_ctx, nb):
    b = pl.program_id(1)
    scale = HEAD_DIM ** -0.5
    nk = NA_K_ROWS * GRID_W
    start = pl.multiple_of(jnp.clip(b - 1, 0, nb - 3) * ATT_BLOCK, ATT_BLOCK)

    def one_head(hh, local):
        cols = slice(hh * HEAD_DIM, (hh + 1) * HEAD_DIM)
        q = q_ref[:, cols]
        s_ctx = _dot_nt(q, k_ref[n_lat:n_lat + n_ctx, cols]) * scale
        vc1 = _with_ones(v_ref[n_lat:n_lat + n_ctx, cols])
        if local:
            kl = k_ref[pl.ds(start, nk), cols]
            vl1 = _with_ones(v_ref[pl.ds(start, nk), cols])
            s_loc = _dot_nt(q, kl) * scale + bias_ref[0, hh]
            m = _row_max(s_loc, s_ctx)
            acc = _dot(jnp.exp(s_loc - m).astype(BF16), vl1) + _dot(jnp.exp(s_ctx - m).astype(BF16), vc1)
        else:
            acc = _dot(jnp.exp(s_ctx - _row_max(s_ctx)).astype(BF16), vc1)
        o_ref[:, cols] = (acc[:, :HEAD_DIM] / acc[:, HEAD_DIM:]).astype(o_ref.dtype)

    @pl.when(b < nb)
    def _():
        for hh in range(NA_HEADS_PER_STEP):
            one_head(hh, True)

    @pl.when(b >= nb)
    def _():
        for hh in range(NA_HEADS_PER_STEP):
            one_head(hh, False)


def _na_attention(p, bias, *, layer, n_lat, n_ctx, heads):
    rows = n_lat + n_ctx
    nb = n_lat // ATT_BLOCK
    nblk = rows // ATT_BLOCK
    hps = NA_HEADS_PER_STEP
    width = hps * HEAD_DIM
    groups = heads // hps

    def pattern(b):
        return jnp.where(b == 0, 0, jnp.where(b == nb - 1, 2, 1))

    return pl.pallas_call(
        functools.partial(_na_kernel, n_lat=n_lat, n_ctx=n_ctx, nb=nb),
        grid=(groups, nblk),
        in_specs=[
            pl.BlockSpec((ATT_BLOCK, width), lambda h, b: (b, h)),
            pl.BlockSpec((rows, width), lambda h, b: (0, groups + h)),
            pl.BlockSpec((rows, width), lambda h, b: (0, 2 * groups + h)),
            pl.BlockSpec((1, hps, ATT_BLOCK, NA_K_ROWS * GRID_W),
                         lambda h, b: (pattern(b), layer * groups + h, 0, 0)),
        ],
        out_specs=pl.BlockSpec((ATT_BLOCK, width), lambda h, b: (b, h)),
        out_shape=jax.ShapeDtypeStruct((rows, heads * HEAD_DIM), BF16),
        compiler_params=_params("arbitrary", "arbitrary"),
        name="na_attention",
    )(p, p, p, bias)


def _hgrn_levels():
    levels, m = [], 1
    while m < HG_CHUNK:
        levels.append(m)
        m *= 2
    return levels


def _hgrn_consts():
    c = HG_CHUNK
    levels = _hgrn_levels()
    fine = [m for m in levels if m < HG_ROW_LEVEL]
    idx = np.arange(c)
    t, j = idx[:, None], idx[None, :]
    n = np.zeros((2, 1 + len(fine), c, c), np.float32)
    msk = np.zeros((2, 1 + len(levels), c, c), np.float32)
    n[0, 0], n[1, 0] = j <= t, j >= t
    msk[0, 0] = msk[1, 0] = np.eye(c)
    for li, m in enumerate(levels):
        seg = idx // (2 * m)
        right = ((idx % (2 * m)) >= m)[:, None]
        last_left = (seg * 2 * m + m - 1)[:, None]
        first_right = last_left + 1
        if m < HG_ROW_LEVEL:
            n[0, 1 + li] = np.where(right, (j > last_left) & (j <= t), (j > t) & (j <= last_left))
            n[1, 1 + li] = np.where(right, (j >= first_right) & (j < t), (j >= t) & (j < first_right))
        same = seg[:, None] == seg[None, :]
        msk[0, 1 + li] = same & right & ~right.T
        msk[1, 1 + li] = same & ~right & right.T
    return n.reshape(2, -1, c), msk


def _hgrn_chunk(d, q, fx, v, lb, n_ref, msk_ref, s_scr, b_scr):
    c = HG_CHUNK
    t = jnp.exp(-jnp.abs(fx))
    r = 1.0 / (1.0 + t)
    log_sig = jnp.minimum(fx, 0.0) + jnp.log(r)
    sig_neg = jnp.where(fx >= 0.0, t * r, r)
    if lb is None:
        log_f, kk = log_sig, sig_neg
    else:
        la = jnp.log(lb)
        lc = jnp.log1p(-lb) + log_sig
        log_f = jnp.maximum(la, lc) + jnp.log(1.0 + jnp.exp(-jnp.abs(la - lc)))
        kk = (1.0 - lb) * sig_neg
    qf = q.astype(F32)

    hi = log_f.astype(BF16)
    mid = (log_f - hi.astype(F32)).astype(BF16)
    dd = _dot(n_ref[d], jnp.concatenate([hi, mid], axis=1))
    dd = dd[:, :HEAD_DIM] + dd[:, HEAD_DIM:]
    b = dd[:c]
    b_scr[d] = b

    attn = _dot_nt(q, kk.astype(BF16)) * msk_ref[d, 0]
    for li, m in enumerate(_hgrn_levels()):
        if m < HG_ROW_LEVEL:
            ex = dd[(1 + li) * c:(2 + li) * c]
        else:
            row = m - 1 + d
            ref = jnp.concatenate(
                [jnp.broadcast_to(b_scr[d, s + row:s + row + 1, :], (2 * m, HEAD_DIM)) for s in range(0, c, 2 * m)],
                axis=0)
            ex = -jnp.abs(b - ref)
        e = jnp.exp(ex)
        attn += _dot_nt((qf * e).astype(BF16), (kk * e).astype(BF16)) * msk_ref[d, 1 + li]

    total = b[c - 1:c] if d == 0 else b[0:1]
    state_t = s_scr[d]
    o = _dot_nt((qf * jnp.exp(b)).astype(BF16), state_t.astype(BF16)) + _dot(attn.astype(BF16), v)
    v_t = v.astype(F32).T.astype(BF16)
    s_scr[d] = jnp.exp(total) * state_t + _dot(v_t, (kk * jnp.exp(total - b)).astype(BF16))
    return o


def _hgrn_kernel(qf_ref, qb_ref, ff_ref, fb_ref, if_ref, ib_ref, lbl_ref, n_ref, msk_ref,
                 of_ref, ob_ref, s_scr, b_scr, *, layer_e):
    @pl.when(pl.program_id(1) == 0)
    def _():
        s_scr[...] = jnp.zeros_like(s_scr)

    refs = ((qf_ref, ff_ref, if_ref, of_ref), (qb_ref, fb_ref, ib_ref, ob_ref))
    lbs = [None, None]
    if layer_e > 0:
        for d in range(2):
            lg = lbl_ref[d]
            ex = jnp.exp(lg - jnp.max(lg, axis=0, keepdims=True))
            num = ex[1:2]
            for e in range(2, layer_e + 1):
                num = num + ex[e:e + 1]
            lbs[d] = num / jnp.sum(ex, axis=0, keepdims=True)

    n_sub = HG_STEP // HG_CHUNK
    for k in range(n_sub):
        for hh in range(HG_HEADS_PER_STEP):
            cols = slice(hh * HEAD_DIM, (hh + 1) * HEAD_DIM)
            for d, (q_ref, f_ref, i_ref, o_ref) in enumerate(refs):
                sub = k if d == 0 else n_sub - 1 - k
                rows = slice(sub * HG_CHUNK, (sub + 1) * HG_CHUNK)
                lb = None if lbs[d] is None else lbs[d][:, cols]
                o_ref[rows, cols] = _hgrn_chunk(d, q_ref[rows, cols], f_ref[rows, cols], i_ref[rows, cols], lb,
                                                n_ref, msk_ref, s_scr.at[hh], b_scr.at[hh, k])


def _hgrn_scan(p16, p32, lb_logits, consts, *, layer_e, n_lat, n_ctx, heads, qcol, icol):
    rows = n_lat + n_ctx
    n_mat, msk = consts
    lat_chunks = n_lat // HG_STEP
    qc, ic = qcol // HEAD_DIM, icol // HEAD_DIM

    def fwd(s):
        return jnp.where(s == 0, lat_chunks, s - 1)

    def bwd(s):
        return lat_chunks - s

    hps = HG_HEADS_PER_STEP
    width = hps * HEAD_DIM

    def spec(blk, col):
        return pl.BlockSpec((HG_STEP, width), lambda h, s: (blk(s), col // hps + h))

    out = jax.ShapeDtypeStruct((rows, heads * HEAD_DIM), F32)
    return pl.pallas_call(
        functools.partial(_hgrn_kernel, layer_e=layer_e),
        grid=(heads // hps, 1 + lat_chunks),
        in_specs=[
            spec(fwd, qc), spec(bwd, qc),
            spec(fwd, 0), spec(bwd, heads),
            spec(fwd, ic), spec(bwd, ic),
            pl.BlockSpec((2, lb_logits.shape[1], width), lambda h, s: (0, 0, h)),
            pl.BlockSpec(n_mat.shape, lambda h, s: (0, 0, 0)),
            pl.BlockSpec(msk.shape, lambda h, s: (0, 0, 0, 0)),
        ],
        out_specs=[spec(fwd, 0), spec(bwd, 0)],
        out_shape=[out, out],
        scratch_shapes=[pltpu.VMEM((hps, 2, HEAD_DIM, HEAD_DIM), F32),
                        pltpu.VMEM((hps, HG_STEP // HG_CHUNK, 2, HG_CHUNK, HEAD_DIM), F32)],
        compiler_params=_params("arbitrary", "arbitrary"),
        name="hgrn_scan",
    )(p16, p16, p32, p32, p16, p16, lb_logits, n_mat, msk)


def _swa_kernel(q_ref, k_ref, v_ref, sink_ref, bias_ref, o_ref, *, n_lat, n_ctx, nb, group, span):
    b = pl.program_id(1)
    scale = HEAD_DIM ** -0.5
    start = pl.multiple_of(jnp.clip(b * ATT_BLOCK - WINDOW, 0, n_lat - span), WINDOW)

    def one_kv_head(kv, local):
        kcols = slice(kv * HEAD_DIM, (kv + 1) * HEAD_DIM)
        kc = k_ref[n_lat:n_lat + n_ctx, kcols]
        vc1 = _with_ones(v_ref[n_lat:n_lat + n_ctx, kcols])
        if local:
            kl = k_ref[pl.ds(start, span), kcols]
            vl1 = _with_ones(v_ref[pl.ds(start, span), kcols])
        for g in range(group):
            qcols = slice((kv * group + g) * HEAD_DIM, (kv * group + g + 1) * HEAD_DIM)
            q = q_ref[:, qcols]
            sink = sink_ref[kv, g:g + 1, 0:1]
            s_ctx = _dot_nt(q, kc) * scale
            if local:
                s_loc = _dot_nt(q, kl) * scale + bias_ref[0]
                m = jnp.maximum(_row_max(s_ctx, s_loc), sink)
                acc = _dot(jnp.exp(s_ctx - m).astype(BF16), vc1) + _dot(jnp.exp(s_loc - m).astype(BF16), vl1)
            else:
                m = jnp.maximum(_row_max(s_ctx), sink)
                acc = _dot(jnp.exp(s_ctx - m).astype(BF16), vc1)
            den = acc[:, HEAD_DIM:] + jnp.exp(sink - m)
            o_ref[:, qcols] = (acc[:, :HEAD_DIM] / den).astype(o_ref.dtype)

    @pl.when(b < nb)
    def _():
        for kv in range(SWA_KV_PER_STEP):
            one_kv_head(kv, True)

    @pl.when(b >= nb)
    def _():
        for kv in range(SWA_KV_PER_STEP):
            one_kv_head(kv, False)


def _swa_bias(n_lat):
    span = ATT_BLOCK + 2 * WINDOW
    nb = n_lat // ATT_BLOCK
    i = np.arange(ATT_BLOCK)[:, None]
    j = np.arange(span)[None, :]
    tables = []
    for b in (0, 1, nb - 1):
        start = int(np.clip(b * ATT_BLOCK - WINDOW, 0, n_lat - span))
        rel = (start + j) - (b * ATT_BLOCK + i)
        tables.append(np.where(np.abs(rel) <= WINDOW, 0.0, MASKED))
    return jnp.asarray(np.stack(tables), F32)


def _swa_attention(p, sink, *, n_lat, n_ctx, heads, kv_heads):
    rows = n_lat + n_ctx
    group = heads // kv_heads
    nb = n_lat // ATT_BLOCK
    span = ATT_BLOCK + 2 * WINDOW
    kps = SWA_KV_PER_STEP
    sink_tab = jnp.broadcast_to(sink.astype(F32).reshape(kv_heads, group, 1), (kv_heads, group, HEAD_DIM))

    def pattern(b):
        return jnp.where(b == 0, 0, jnp.where(b == nb - 1, 2, 1))

    return pl.pallas_call(
        functools.partial(_swa_kernel, n_lat=n_lat, n_ctx=n_ctx, nb=nb, group=group, span=span),
        grid=(kv_heads // kps, rows // ATT_BLOCK),
        in_specs=[
            pl.BlockSpec((ATT_BLOCK, kps * group * HEAD_DIM), lambda k, b: (b, k)),
            pl.BlockSpec((rows, kps * HEAD_DIM), lambda k, b: (0, heads // kps + k)),
            pl.BlockSpec((rows, kps * HEAD_DIM), lambda k, b: (0, (heads + kv_heads) // kps + k)),
            pl.BlockSpec((kps, group, HEAD_DIM), lambda k, b: (k, 0, 0)),
            pl.BlockSpec((1, ATT_BLOCK, span), lambda k, b: (pattern(b), 0, 0)),
        ],
        out_specs=pl.BlockSpec((ATT_BLOCK, kps * group * HEAD_DIM), lambda k, b: (b, k)),
        out_shape=jax.ShapeDtypeStruct((rows, heads * HEAD_DIM), BF16),
        compiler_params=_params("arbitrary", "arbitrary"),
        name="swa_attention",
    )(p, p, p, sink_tab, _swa_bias(n_lat))


def kernel(x, c, ctx, c_ctx, w_mod, b_mod, norm_g, w_ff_in, w_ff_out, w_in_even, w_out_even,
           na_rpb, hg_lb_logits, hg_norm_g, w_qkv_odd, w_o_odd, sink_odd, final_norm_g):
    assert x.shape[0] == 1
    depth, d = w_mod.shape[0], x.shape[2]
    n_lat, n_ctx = x.shape[1], ctx.shape[1]
    rows = n_lat + n_ctx
    tm = 768
    assert rows % tm == 0 and n_lat % 1024 == 0 and n_ctx == ATT_BLOCK
    na_heads = na_rpb.shape[1]
    a_w = na_heads * HEAD_DIM
    hg_heads = hg_norm_g.shape[1] // HEAD_DIM
    heads = sink_odd.shape[1]
    kv_heads = (w_qkv_odd.shape[2] // HEAD_DIM - heads) // 2

    cc = jnp.zeros((8, d), F32).at[0].set(c[0]).at[1].set(c_ctx)
    mod_all = _modulation(cc, w_mod, b_mod)

    cos, sin = _rope_tables(n_lat, n_ctx)
    na_bias = _na_bias(na_rpb.reshape((-1,) + na_rpb.shape[2:]), n_lat // GRID_W)
    hg_consts = _hgrn_consts()
    hg_consts = (jnp.asarray(hg_consts[0], BF16), jnp.asarray(hg_consts[1], F32))
    fg = final_norm_g.reshape(1, d)
    b_w = hg_heads * HEAD_DIM
    tn = 1024
    c_f, c_i, c_g, c_end = (3 * a_w + b_w) // tn, (3 * a_w + 3 * b_w) // tn, (3 * a_w + 4 * b_w) // tn, w_in_even.shape[2] // tn
    even_blocks = tuple(range(c_f)) + tuple(range(c_i, c_g)) + tuple(range(c_f, c_i)) + tuple(range(c_g, c_end))
    w_in_even, w_out_even = w_in_even.astype(BF16), w_out_even.astype(BF16)
    w_qkv_odd, w_o_odd = w_qkv_odd.astype(BF16), w_o_odd.astype(BF16)

    h = x.reshape(n_lat, d)
    ffn_tm = FFN_ROWS
    ffn_tail = rows % ffn_tm or ffn_tm
    h_tail = jnp.concatenate([h[rows - ffn_tail:], ctx.reshape(n_ctx, d)], axis=0)
    for l in range(depth):
        mod = mod_all[l]
        last = l == depth - 1
        h = _ffn(h, mod, norm_g[l, 0].reshape(1, d), w_ff_in, w_ff_out, fg, layer=l, half=0,
                 sub=0, rows=rows, tm=ffn_tm, n_lat=n_lat, h_tail=h_tail if l == 0 else None)
        g1 = norm_g[l, 1].reshape(1, d)
        if l % 2 == 0:
            e = l // 2
            p16, p32 = _proj(h, mod, g1, w_in_even, layer=e, tm=tm, tn=tn, n_lat=n_lat, cols16=3 * a_w + 2 * b_w,
                             col_blocks=even_blocks)
            ya = _na_attention(p16, na_bias, layer=e, n_lat=n_lat, n_ctx=n_ctx, heads=na_heads)
            o_f, o_b = _hgrn_scan(p16, p32, hg_lb_logits, hg_consts, layer_e=e, n_lat=n_lat, n_ctx=n_ctx,
                                  heads=hg_heads, qcol=3 * a_w, icol=3 * a_w + b_w)
            h = _oproj_hgrn(ya, o_f, o_b, p32, 2 * b_w, hg_norm_g[e].reshape(1, -1), w_out_even, h, mod,
                            layer=e, tm=tm // 2, n_lat=n_lat)
        else:
            o = l // 2
            p16 = _qkv_proj(h, mod, g1, w_qkv_odd, cos, sin, layer=o, tm=tm, n_lat=n_lat,
                            rope_cols=(heads + kv_heads) * HEAD_DIM)
            y = _swa_attention(p16, sink_odd[o], n_lat=n_lat, n_ctx=n_ctx, heads=heads, kv_heads=kv_heads)
            h = _oproj(y, y, 1, w_o_odd, h, mod, layer=o, tm=tm, n_lat=n_lat)
        g2 = norm_g[l, 2].reshape(1, d)
        if last:
            h = _ffn(h, mod, g2, w_ff_in, w_ff_out, fg, layer=l, half=1,
                     sub=2, rows=n_lat, tm=ffn_tm, n_lat=n_lat, final_norm=True)
        else:
            h = _ffn(h, mod, g2, w_ff_in, w_ff_out, fg, layer=l, half=1,
                     sub=2, rows=rows, tm=ffn_tm, n_lat=n_lat)
    return h.reshape(1, n_lat, d)
```

```python
import functools

import numpy as np
import jax
import jax.numpy as jnp
from jax import lax
from jax.experimental import pallas as pl
from jax.experimental.pallas import tpu as pltpu

F32 = jnp.float32
BF16 = jnp.bfloat16

GRID_W = 64
NORM_EPS = 1e-6
ROPE_THETA = 10000.0
HEAD_DIM = 128
NA_ROWS = 8
NA_COLS = 16
NA_Q_ROWS = 4
NA_K_ROWS = NA_Q_ROWS + NA_ROWS
ATT_BLOCK = NA_Q_ROWS * GRID_W
NA_HEADS_PER_STEP = 4
SWA_KV_PER_STEP = 4
WINDOW = 128
HG_STEP = 256
HG_CHUNK = 128
HG_HEADS_PER_STEP = 8
ROW_CHUNK = 16
FFN_ROWS = 960
FFN_COLS = 512
FFN_TAIL_COLS = 256
QKV_SLAB = 512
FFN_OUT_COLS = 512
ROW_UNROLL = 4
HG_ROW_LEVEL = 8
MASKED = -1e30
VMEM_LIMIT = 56 * 1024 * 1024


def _params(*sem):
    return pltpu.CompilerParams(dimension_semantics=sem, vmem_limit_bytes=VMEM_LIMIT)


def _dot(a, b):
    return jnp.dot(a, b, preferred_element_type=F32)


def _dot_nt(a, b):
    return lax.dot_general(a, b, (((1,), (1,)), ((), ())), preferred_element_type=F32)


def _sigmoid(x):
    return 1.0 / (1.0 + jnp.exp(-x))


def _silu(x):
    return x * _sigmoid(x)


def _mod_kernel(c_ref, w_ref, b_ref, o_ref):
    s = _silu(c_ref[...]).astype(BF16)
    o_ref[0, 0] = _dot(s, w_ref[0].astype(BF16)) + b_ref[0]


def _modulation(cc, w_mod, b_mod):
    depth, d, n = w_mod.shape
    return pl.pallas_call(
        _mod_kernel,
        grid=(depth, n // d),
        in_specs=[
            pl.BlockSpec((8, d), lambda l, j: (0, 0)),
            pl.BlockSpec((1, d, d), lambda l, j: (l, 0, j)),
            pl.BlockSpec((1, 1, d), lambda l, j: (l, 0, j)),
        ],
        out_specs=pl.BlockSpec((1, 1, 8, d), lambda l, j: (l, j, 0, 0)),
        out_shape=jax.ShapeDtypeStruct((depth, n // d, 8, d), F32),
        compiler_params=_params("arbitrary", "arbitrary"),
        name="modulation",
    )(cc, w_mod, b_mod.reshape(depth, 1, n))


def _mod_rows(mod_ref, sub, k, is_ctx):
    r = 3 * sub + k
    return jnp.where(is_ctx, mod_ref[r, 1:2, :], mod_ref[r, 0:1, :])


def _row_max(*blocks):
    acc = None
    for s in blocks:
        for c0 in range(0, s.shape[1], HEAD_DIM):
            t = s[:, c0:c0 + HEAD_DIM]
            acc = t if acc is None else jnp.maximum(acc, t)
    return jnp.max(acc, axis=-1, keepdims=True)


def _with_ones(v):
    return jnp.concatenate([v, jnp.ones_like(v)], axis=1)


def _is_ctx(tile, tm, n_lat):
    row = tile * tm + lax.broadcasted_iota(jnp.int32, (tm, 1), 0)
    return row >= n_lat


def _row_chunks(row0, tm, n_lat, body):
    def step(c, carry):
        r0 = pl.multiple_of(c * ROW_CHUNK, ROW_CHUNK)
        body(pl.ds(r0, ROW_CHUNK), (row0 + r0 >= n_lat).astype(jnp.int32))
        return carry

    lax.fori_loop(0, tm // ROW_CHUNK, step, 0, unroll=ROW_UNROLL)


def _adaln_rows(h_ref, mod_ref, g_ref, u_scr, gs_scr, *, sub, row0, tm, n_lat):
    for which in range(2):
        gs_scr[which] = g_ref[...] * (1.0 + mod_ref[3 * sub + 1, which:which + 1, :])

    def body(rows, which):
        x = h_ref[rows, :]
        r = lax.rsqrt(jnp.mean(x * x, axis=-1, keepdims=True) + NORM_EPS)
        u_scr[rows, :] = ((x * r) * gs_scr[which] + mod_ref[3 * sub, pl.ds(which, 1), :]).astype(BF16)

    _row_chunks(row0, tm, n_lat, body)


def _ffn_kernel(h_ref, mod_ref, g_ref, wg_ref, wu_ref, wo_ref, fg_ref, *rest,
                sub, tm, row0, n_lat, final_norm, emit16, aliased):
    rest = rest[1:] if aliased else rest
    o_ref = rest[0]
    u_scr, gs_scr = rest[-2:]
    i, j = pl.program_id(0), pl.program_id(1)
    first_row = row0 + i * tm

    @pl.when(j == 0)
    def _():
        _adaln_rows(h_ref, mod_ref, g_ref, u_scr, gs_scr, sub=sub, row0=first_row, tm=tm, n_lat=n_lat)
        o_ref[...] = jnp.zeros_like(o_ref)

    if emit16:
        wg16_ref, wu16_ref, wo16_ref = rest[1:4]
        wg16_ref[...] = wg_ref[...].astype(BF16)
        wu16_ref[...] = wu_ref[...].astype(BF16)
        wo16_ref[...] = wo_ref[...].astype(BF16)
        wg_ref, wu_ref, wo_ref = wg16_ref, wu16_ref, wo16_ref
    u = u_scr[...]
    a = (_silu(_dot(u, wg_ref[...])) * _dot(u, wu_ref[...])).astype(BF16)
    d = o_ref.shape[1]
    for c0 in range(0, d, FFN_OUT_COLS):
        o_ref[:, c0:c0 + FFN_OUT_COLS] += _dot(a, wo_ref[:, c0:c0 + FFN_OUT_COLS])

    @pl.when(j == pl.num_programs(1) - 1)
    def _():
        def body(rows, which):
            h = h_ref[rows, :] + (0.5 * mod_ref[3 * sub + 2, pl.ds(which, 1), :]) * o_ref[rows, :]
            if final_norm:
                h = h * lax.rsqrt(jnp.mean(h * h, axis=-1, keepdims=True) + NORM_EPS) * fg_ref[...]
            o_ref[rows, :] = h

        _row_chunks(first_row, tm, n_lat, body)


def _ffn(h, mod, g, w_in, w_out, fg, *, layer, half, sub, rows, tm, n_lat, final_norm=False, h_tail=None):
    d = h.shape[1]
    f = w_out.shape[2]
    tail = rows % tm or tm
    n_main = (rows - tail) // tm
    assert (rows - tail) % tail == 0
    tail_src, tail_blk = (h, (rows - tail) // tail) if h_tail is None else (h_tail, 0)
    common = dict(sub=sub, n_lat=n_lat, final_norm=final_norm)
    small = [
        pl.BlockSpec((9, 8, d), lambda i, j: (0, 0, 0)),
        pl.BlockSpec((1, d), lambda i, j: (0, 0)),
    ]
    fg_spec = pl.BlockSpec((1, d), lambda i, j: (0, 0))

    tf, t0 = FFN_TAIL_COLS, (rows - tail) // tail
    nf = f // tf
    out, wg16, wu16, wo16 = pl.pallas_call(
        functools.partial(_ffn_kernel, tm=tail, row0=rows - tail, emit16=True, aliased=False, **common),
        grid=(1, nf),
        in_specs=[pl.BlockSpec((tail, d), lambda i, j: (tail_blk, 0))] + small + [
            pl.BlockSpec((None, None, d, tf), lambda i, j: (layer, half, 0, j)),
            pl.BlockSpec((None, None, d, tf), lambda i, j: (layer, half, 0, nf + j)),
            pl.BlockSpec((None, None, tf, d), lambda i, j: (layer, half, j, 0)),
            fg_spec,
        ],
        out_specs=[
            pl.BlockSpec((tail, d), lambda i, j: (t0, 0)),
            pl.BlockSpec((d, tf), lambda i, j: (0, j)),
            pl.BlockSpec((d, tf), lambda i, j: (0, j)),
            pl.BlockSpec((tf, d), lambda i, j: (j, 0)),
        ],
        out_shape=[
            jax.ShapeDtypeStruct((rows, d), F32),
            jax.ShapeDtypeStruct((d, f), BF16),
            jax.ShapeDtypeStruct((d, f), BF16),
            jax.ShapeDtypeStruct((f, d), BF16),
        ],
        scratch_shapes=[pltpu.VMEM((tail, d), BF16), pltpu.VMEM((2, 1, d), F32)],
        compiler_params=_params("arbitrary", "arbitrary"),
        name="ffn_tail",
    )(tail_src, mod, g, w_in, w_in, w_out, fg)

    tf = FFN_COLS
    return pl.pallas_call(
        functools.partial(_ffn_kernel, tm=tm, row0=0, emit16=False, aliased=True, **common),
        grid=(n_main, f // tf),
        in_specs=[pl.BlockSpec((tm, d), lambda i, j: (i, 0))] + small + [
            pl.BlockSpec((d, tf), lambda i, j: (0, j)),
            pl.BlockSpec((d, tf), lambda i, j: (0, j)),
            pl.BlockSpec((tf, d), lambda i, j: (j, 0)),
            fg_spec,
            pl.BlockSpec(memory_space=pl.ANY),
        ],
        out_specs=pl.BlockSpec((tm, d), lambda i, j: (i, 0)),
        out_shape=jax.ShapeDtypeStruct((rows, d), F32),
        input_output_aliases={7: 0},
        scratch_shapes=[pltpu.VMEM((tm, d), BF16), pltpu.VMEM((2, 1, d), F32)],
        compiler_params=_params("arbitrary", "arbitrary"),
        name="ffn",
    )(h, mod, g, wg16, wu16, wo16, fg, out)


def _rope(x, cos, sin):
    lane = lax.broadcasted_iota(jnp.int32, x.shape, 1)
    first = (lane % 64) < 32
    swapped = jnp.where(first, pltpu.roll(x, 96, 1), pltpu.roll(x, 32, 1))
    return x * cos + swapped * sin


def _proj_kernel(h_ref, mod_ref, g_ref, w_ref, o16_ref, o32_ref, u_scr, gs_scr, *, tm, n_lat, n16):
    i, j = pl.program_id(0), pl.program_id(1)

    @pl.when(j == 0)
    def _():
        _adaln_rows(h_ref, mod_ref, g_ref, u_scr, gs_scr, sub=1, row0=i * tm, tm=tm, n_lat=n_lat)

    y = _dot(u_scr[...], w_ref[...])

    @pl.when(j < n16)
    def _():
        o16_ref[...] = y.astype(BF16)

    @pl.when(j >= n16)
    def _():
        o32_ref[...] = y


def _proj(h, mod, g, w, *, layer, tm, tn, n_lat, cols16, col_blocks):
    rows, d = h.shape
    n = w.shape[2]
    n16 = cols16 // tn

    def src_block(j):
        blk = j
        for dst, src in enumerate(col_blocks):
            if dst != src:
                blk = jnp.where(j == dst, src, blk)
        return blk

    return pl.pallas_call(
        functools.partial(_proj_kernel, tm=tm, n_lat=n_lat, n16=n16),
        grid=(rows // tm, n // tn),
        in_specs=[
            pl.BlockSpec((tm, d), lambda i, j: (i, 0)),
            pl.BlockSpec((9, 8, d), lambda i, j: (0, 0, 0)),
            pl.BlockSpec((1, d), lambda i, j: (0, 0)),
            pl.BlockSpec((None, d, tn), lambda i, j: (layer, 0, src_block(j))),
        ],
        out_specs=[
            pl.BlockSpec((tm, tn), lambda i, j: (i, jnp.minimum(j, n16 - 1))),
            pl.BlockSpec((tm, tn), lambda i, j: (i, jnp.maximum(j - n16, 0))),
        ],
        out_shape=[jax.ShapeDtypeStruct((rows, cols16), BF16), jax.ShapeDtypeStruct((rows, n - cols16), F32)],
        scratch_shapes=[pltpu.VMEM((tm, d), BF16), pltpu.VMEM((2, 1, d), F32)],
        compiler_params=_params("arbitrary", "arbitrary"),
        name="proj",
    )(h, mod, g, w)


def _qkv_kernel(h_ref, mod_ref, g_ref, w_ref, cos_ref, sin_ref, o_ref, u_scr, gs_scr, *, tm, n_lat, rope_cols):
    _adaln_rows(h_ref, mod_ref, g_ref, u_scr, gs_scr, sub=1, row0=pl.program_id(0) * tm, tm=tm, n_lat=n_lat)
    u = u_scr[...]
    cos, sin = cos_ref[...], sin_ref[...]
    for c0 in range(0, o_ref.shape[1], QKV_SLAB):
        y = _dot(u, w_ref[:, c0:c0 + QKV_SLAB])
        for hd in range(0, QKV_SLAB, HEAD_DIM):
            x = y[:, hd:hd + HEAD_DIM]
            if c0 + hd < rope_cols:
                x = _rope(x, cos, sin)
            o_ref[:, c0 + hd:c0 + hd + HEAD_DIM] = x.astype(BF16)


def _qkv_proj(h, mod, g, w, cos, sin, *, layer, tm, n_lat, rope_cols):
    rows, d = h.shape
    n = w.shape[2]
    return pl.pallas_call(
        functools.partial(_qkv_kernel, tm=tm, n_lat=n_lat, rope_cols=rope_cols),
        grid=(rows // tm,),
        in_specs=[
            pl.BlockSpec((tm, d), lambda i: (i, 0)),
            pl.BlockSpec((9, 8, d), lambda i: (0, 0, 0)),
            pl.BlockSpec((1, d), lambda i: (0, 0)),
            pl.BlockSpec((None, d, n), lambda i: (layer, 0, 0), pipeline_mode=pl.Buffered(1)),
            pl.BlockSpec((tm, HEAD_DIM), lambda i: (i, 0)),
            pl.BlockSpec((tm, HEAD_DIM), lambda i: (i, 0)),
        ],
        out_specs=pl.BlockSpec((tm, n), lambda i: (i, 0)),
        out_shape=jax.ShapeDtypeStruct((rows, n), BF16),
        scratch_shapes=[pltpu.VMEM((tm, d), BF16), pltpu.VMEM((2, 1, d), F32)],
        compiler_params=_params("arbitrary"),
        name="qkv_proj",
    )(h, mod, g, w, cos, sin)


def _rope_tables(n_lat, n_ctx):
    t = np.arange(n_lat)
    inv = (ROPE_THETA ** (-np.arange(0, 64, 2, dtype=np.float32) / 64)).astype(np.float32)
    ang_r = (t // GRID_W).astype(np.float32)[:, None] * inv[None, :]
    ang_c = (t % GRID_W).astype(np.float32)[:, None] * inv[None, :]
    cr, sr, cc, sc = np.cos(ang_r), np.sin(ang_r), np.cos(ang_c), np.sin(ang_c)
    cos = np.concatenate([cr, cr, cc, cc], axis=1)
    sin = np.concatenate([-sr, sr, -sc, sc], axis=1)
    cos = np.concatenate([cos, np.ones((n_ctx, HEAD_DIM), np.float32)], axis=0)
    sin = np.concatenate([sin, np.zeros((n_ctx, HEAD_DIM), np.float32)], axis=0)
    return jnp.asarray(cos, F32), jnp.asarray(sin, F32)


def _oproj_kernel(a1_ref, a2_ref, w_ref, h_ref, mod_ref, o_ref, *, tm, n_lat, k1):
    is_ctx = _is_ctx(pl.program_id(0), tm, n_lat)
    y = _dot(a1_ref[...], w_ref[:k1, :]) + _dot(a2_ref[...], w_ref[k1:, :])
    o_ref[...] = h_ref[...] + _mod_rows(mod_ref, 1, 2, is_ctx) * y


def _oproj(a1, a2, col2, w, h, mod, *, layer, tm, n_lat):
    rows, d = h.shape
    k1 = w.shape[1] // 2
    return pl.pallas_call(
        functools.partial(_oproj_kernel, tm=tm, n_lat=n_lat, k1=k1),
        grid=(rows // tm,),
        in_specs=[
            pl.BlockSpec((tm, k1), lambda i: (i, 0)),
            pl.BlockSpec((tm, k1), lambda i: (i, col2)),
            pl.BlockSpec((None,) + w.shape[1:], lambda i: (layer, 0, 0)),
            pl.BlockSpec((tm, d), lambda i: (i, 0)),
            pl.BlockSpec((9, 8, d), lambda i: (0, 0, 0)),
        ],
        out_specs=pl.BlockSpec((tm, d), lambda i: (i, 0)),
        out_shape=jax.ShapeDtypeStruct((rows, d), F32),
        compiler_params=_params("arbitrary"),
        name="oproj",
    )(a1, a2, w, h, mod)


def _oproj_hgrn_kernel(a1_ref, of_ref, ob_ref, g_ref, ng_ref, w_ref, h_ref, mod_ref, o_ref, yb_scr,
                       *, tm, n_lat, k1):
    for c0 in range(0, k1, HEAD_DIM):
        sl = slice(c0, c0 + HEAD_DIM)
        o = of_ref[:, sl] + ob_ref[:, sl]
        o = o * lax.rsqrt(jnp.mean(o * o, axis=-1, keepdims=True) + NORM_EPS)
        yb_scr[:, sl] = (o * ng_ref[:, sl] * _silu(g_ref[:, sl])).astype(BF16)
    is_ctx = _is_ctx(pl.program_id(0), tm, n_lat)
    y = _dot(a1_ref[...], w_ref[:k1, :]) + _dot(yb_scr[...], w_ref[k1:, :])
    o_ref[...] = h_ref[...] + _mod_rows(mod_ref, 1, 2, is_ctx) * y


def _oproj_hgrn(a1, o_f, o_b, p32, gcol0, norm_g, w, h, mod, *, layer, tm, n_lat):
    rows, d = h.shape
    k1 = w.shape[1] // 2
    half = lambda i: (i, 0)
    return pl.pallas_call(
        functools.partial(_oproj_hgrn_kernel, tm=tm, n_lat=n_lat, k1=k1),
        grid=(rows // tm,),
        in_specs=[
            pl.BlockSpec((tm, k1), half),
            pl.BlockSpec((tm, k1), half),
            pl.BlockSpec((tm, k1), half),
            pl.BlockSpec((tm, k1), lambda i: (i, gcol0 // k1)),
            pl.BlockSpec((1, k1), lambda i: (0, 0)),
            pl.BlockSpec((None,) + w.shape[1:], lambda i: (layer, 0, 0)),
            pl.BlockSpec((tm, d), half),
            pl.BlockSpec((9, 8, d), lambda i: (0, 0, 0)),
        ],
        out_specs=pl.BlockSpec((tm, d), half),
        out_shape=jax.ShapeDtypeStruct((rows, d), F32),
        scratch_shapes=[pltpu.VMEM((tm, k1), BF16)],
        compiler_params=_params("arbitrary"),
        name="oproj_hgrn",
    )(a1, o_f, o_b, p32, norm_g, w, h, mod)


def _na_bias(rpb, n_rows):
    nh, n_a, n_b = rpb.shape
    nb = n_rows // NA_Q_ROWS
    w = jnp.full((nh, n_a, 128), MASKED, F32)
    w = w.at[..., :NA_COLS].set(rpb[..., NA_COLS - 1:]).at[..., 128 - (NA_COLS - 1):].set(rpb[..., :NA_COLS - 1])
    toep = jnp.tile(w, (1, 1, GRID_W))[..., :GRID_W * 127].reshape(nh, n_a, GRID_W, 127)[..., :GRID_W]
    toep = jnp.pad(toep.transpose(0, 2, 1, 3), ((0, 0), (0, 0), (NA_K_ROWS, NA_K_ROWS), (0, 0)),
                   constant_values=MASKED).reshape(nh, GRID_W, -1)

    q = np.arange(ATT_BLOCK)
    k = np.arange(NA_K_ROWS * GRID_W)
    dr, c = q // GRID_W, q % GRID_W
    kr, kc = k // GRID_W, k % GRID_W
    c0 = np.clip(c - NA_COLS // 2, 0, GRID_W - NA_COLS)
    col_ok = (kc[None, :] >= c0[:, None]) & (kc[None, :] < c0[:, None] + NA_COLS)
    tables, ok = [], []
    for b in (0, 1, nb - 1):
        r = b * NA_Q_ROWS + dr
        r0 = np.clip(r - NA_ROWS // 2, 0, n_rows - NA_ROWS)
        base = int(np.clip(b * NA_Q_ROWS - NA_ROWS // 2, 0, n_rows - NA_K_ROWS))
        kabs = base + kr
        ok.append((kabs[None, :] >= r0[:, None]) & (kabs[None, :] < r0[:, None] + NA_ROWS) & col_ok)
        per_row = []
        for j in range(NA_Q_ROWS):
            lo = base - (b * NA_Q_ROWS + j) + NA_ROWS - 1 + NA_K_ROWS
            per_row.append(toep[:, :, lo * GRID_W:(lo + NA_K_ROWS) * GRID_W])
        tables.append(jnp.concatenate(per_row, axis=1))
    return jnp.where(np.stack(ok)[:, None], jnp.stack(tables), MASKED)


def _na_kernel(q_ref, k_ref, v_ref, bias_ref, o_ref, *, n_lat, n_ctx, nb):
    b = pl.program_id(1)
    scale = HEAD_DIM ** -0.5
    nk = NA_K_ROWS * GRID_W
    start = pl.multiple_of(jnp.clip(b - 1, 0, nb - 3) * ATT_BLOCK, ATT_BLOCK)

    def one_head(hh, local):
        cols = slice(hh * HEAD_DIM, (hh + 1) * HEAD_DIM)
        q = q_ref[:, cols]
        s_ctx = _dot_nt(q, k_ref[n_lat:n_lat + n_ctx, cols]) * scale
        vc1 = _with_ones(v_ref[n_lat:n_lat + n_ctx, cols])
        if local:
            kl = k_ref[pl.ds(start, nk), cols]
            vl1 = _with_ones(v_ref[pl.ds(start, nk), cols])
            s_loc = _dot_nt(q, kl) * scale + bias_ref[0, hh]
            m = _row_max(s_loc, s_ctx)
            acc = _dot(jnp.exp(s_loc - m).astype(BF16), vl1) + _dot(jnp.exp(s_ctx - m).astype(BF16), vc1)
        else:
            acc = _dot(jnp.exp(s_ctx - _row_max(s_ctx)).astype(BF16), vc1)
        o_ref[:, cols] = (acc[:, :HEAD_DIM] / acc[:, HEAD_DIM:]).astype(o_ref.dtype)

    @pl.when(b < nb)
    def _():
        for hh in range(NA_HEADS_PER_STEP):
            one_head(hh, True)

    @pl.when(b >= nb)
    def _():
        for hh in range(NA_HEADS_PER_STEP):
            one_head(hh, False)


def _na_attention(p, bias, *, layer, n_lat, n_ctx, heads):
    rows = n_lat + n_ctx
    nb = n_lat // ATT_BLOCK
    nblk = rows // ATT_BLOCK
    hps = NA_HEADS_PER_STEP
    width = hps * HEAD_DIM
    groups = heads // hps

    def pattern(b):
        return jnp.where(b == 0, 0, jnp.where(b == nb - 1, 2, 1))

    return pl.pallas_call(
        functools.partial(_na_kernel, n_lat=n_lat, n_ctx=n_ctx, nb=nb),
        grid=(groups, nblk),
        in_specs=[
            pl.BlockSpec((ATT_BLOCK, width), lambda h, b: (b, h)),
            pl.BlockSpec((rows, width), lambda h, b: (0, groups + h)),
            pl.BlockSpec((rows, width), lambda h, b: (0, 2 * groups + h)),
            pl.BlockSpec((1, hps, ATT_BLOCK, NA_K_ROWS * GRID_W),
                         lambda h, b: (pattern(b), layer * groups + h, 0, 0)),
        ],
        out_specs=pl.BlockSpec((ATT_BLOCK, width), lambda h, b: (b, h)),
        out_shape=jax.ShapeDtypeStruct((rows, heads * HEAD_DIM), BF16),
        compiler_params=_params("arbitrary", "arbitrary"),
        name="na_attention",
    )(p, p, p, bias)


def _hgrn_levels():
    levels, m = [], 1
    while m < HG_CHUNK:
        levels.append(m)
        m *= 2
    return levels


def _hgrn_consts():
    c = HG_CHUNK
    levels = _hgrn_levels()
    fine = [m for m in levels if m < HG_ROW_LEVEL]
    idx = np.arange(c)
    t, j = idx[:, None], idx[None, :]
    n = np.zeros((2, 1 + len(fine), c, c), np.float32)
    msk = np.zeros((2, 1 + len(levels), c, c), np.float32)
    n[0, 0], n[1, 0] = j <= t, j >= t
    msk[0, 0] = msk[1, 0] = np.eye(c)
    for li, m in enumerate(levels):
        seg = idx // (2 * m)
        right = ((idx % (2 * m)) >= m)[:, None]
        last_left = (seg * 2 * m + m - 1)[:, None]
        first_right = last_left + 1
        if m < HG_ROW_LEVEL:
            n[0, 1 + li] = np.where(right, (j > last_left) & (j <= t), (j > t) & (j <= last_left))
            n[1, 1 + li] = np.where(right, (j >= first_right) & (j < t), (j >= t) & (j < first_right))
        same = seg[:, None] == seg[None, :]
        msk[0, 1 + li] = same & right & ~right.T
        msk[1, 1 + li] = same & ~right & right.T
    return n.reshape(2, -1, c), msk


def _hgrn_chunk(d, q, fx, v, lb, n_ref, msk_ref, s_scr, b_scr):
    c = HG_CHUNK
    t = jnp.exp(-jnp.abs(fx))
    r = 1.0 / (1.0 + t)
    log_sig = jnp.minimum(fx, 0.0) + jnp.log(r)
    sig_neg = jnp.where(fx >= 0.0, t * r, r)
    if lb is None:
        log_f, kk = log_sig, sig_neg
    else:
        la = jnp.log(lb)
        lc = jnp.log1p(-lb) + log_sig
        log_f = jnp.maximum(la, lc) + jnp.log(1.0 + jnp.exp(-jnp.abs(la - lc)))
        kk = (1.0 - lb) * sig_neg
    qf = q.astype(F32)

    hi = log_f.astype(BF16)
    mid = (log_f - hi.astype(F32)).astype(BF16)
    dd = _dot(n_ref[d], jnp.concatenate([hi, mid], axis=1))
    dd = dd[:, :HEAD_DIM] + dd[:, HEAD_DIM:]
    b = dd[:c]
    b_scr[d] = b

    attn = _dot_nt(q, kk.astype(BF16)) * msk_ref[d, 0]
    for li, m in enumerate(_hgrn_levels()):
        if m < HG_ROW_LEVEL:
            ex = dd[(1 + li) * c:(2 + li) * c]
        else:
            row = m - 1 + d
            ref = jnp.concatenate(
                [jnp.broadcast_to(b_scr[d, s + row:s + row + 1, :], (2 * m, HEAD_DIM)) for s in range(0, c, 2 * m)],
                axis=0)
            ex = -jnp.abs(b - ref)
        e = jnp.exp(ex)
        attn += _dot_nt((qf * e).astype(BF16), (kk * e).astype(BF16)) * msk_ref[d, 1 + li]

    total = b[c - 1:c] if d == 0 else b[0:1]
    state_t = s_scr[d]
    o = _dot_nt((qf * jnp.exp(b)).astype(BF16), state_t.astype(BF16)) + _dot(attn.astype(BF16), v)
    v_t = v.astype(F32).T.astype(BF16)
    s_scr[d] = jnp.exp(total) * state_t + _dot(v_t, (kk * jnp.exp(total - b)).astype(BF16))
    return o


def _hgrn_kernel(qf_ref, qb_ref, ff_ref, fb_ref, if_ref, ib_ref, lbl_ref, n_ref, msk_ref,
                 of_ref, ob_ref, s_scr, b_scr, *, layer_e):
    @pl.when(pl.program_id(1) == 0)
    def _():
        s_scr[...] = jnp.zeros_like(s_scr)

    refs = ((qf_ref, ff_ref, if_ref, of_ref), (qb_ref, fb_ref, ib_ref, ob_ref))
    lbs = [None, None]
    if layer_e > 0:
        for d in range(2):
            lg = lbl_ref[d]
            ex = jnp.exp(lg - jnp.max(lg, axis=0, keepdims=True))
            num = ex[1:2]
            for e in range(2, layer_e + 1):
                num = num + ex[e:e + 1]
            lbs[d] = num / jnp.sum(ex, axis=0, keepdims=True)

    n_sub = HG_STEP // HG_CHUNK
    for k in range(n_sub):
        for hh in range(HG_HEADS_PER_STEP):
            cols = slice(hh * HEAD_DIM, (hh + 1) * HEAD_DIM)
            for d, (q_ref, f_ref, i_ref, o_ref) in enumerate(refs):
                sub = k if d == 0 else n_sub - 1 - k
                rows = slice(sub * HG_CHUNK, (sub + 1) * HG_CHUNK)
                lb = None if lbs[d] is None else lbs[d][:, cols]
                o_ref[rows, cols] = _hgrn_chunk(d, q_ref[rows, cols], f_ref[rows, cols], i_ref[rows, cols], lb,
                                                n_ref, msk_ref, s_scr.at[hh], b_scr.at[hh, k])


def _hgrn_scan(p16, p32, lb_logits, consts, *, layer_e, n_lat, n_ctx, heads, qcol, icol):
    rows = n_lat + n_ctx
    n_mat, msk = consts
    lat_chunks = n_lat // HG_STEP
    qc, ic = qcol // HEAD_DIM, icol // HEAD_DIM

    def fwd(s):
        return jnp.where(s == 0, lat_chunks, s - 1)

    def bwd(s):
        return lat_chunks - s

    hps = HG_HEADS_PER_STEP
    width = hps * HEAD_DIM

    def spec(blk, col):
        return pl.BlockSpec((HG_STEP, width), lambda h, s: (blk(s), col // hps + h))

    out = jax.ShapeDtypeStruct((rows, heads * HEAD_DIM), F32)
    return pl.pallas_call(
        functools.partial(_hgrn_kernel, layer_e=layer_e),
        grid=(heads // hps, 1 + lat_chunks),
        in_specs=[
            spec(fwd, qc), spec(bwd, qc),
            spec(fwd, 0), spec(bwd, heads),
            spec(fwd, ic), spec(bwd, ic),
            pl.BlockSpec((2, lb_logits.shape[1], width), lambda h, s: (0, 0, h)),
            pl.BlockSpec(n_mat.shape, lambda h, s: (0, 0, 0)),
            pl.BlockSpec(msk.shape, lambda h, s: (0, 0, 0, 0)),
        ],
        out_specs=[spec(fwd, 0), spec(bwd, 0)],
        out_shape=[out, out],
        scratch_shapes=[pltpu.VMEM((hps, 2, HEAD_DIM, HEAD_DIM), F32),
                        pltpu.VMEM((hps, HG_STEP // HG_CHUNK, 2, HG_CHUNK, HEAD_DIM), F32)],
        compiler_params=_params("arbitrary", "arbitrary"),
        name="hgrn_scan",
    )(p16, p16, p32, p32, p16, p16, lb_logits, n_mat, msk)


def _swa_kernel(q_ref, k_ref, v_ref, sink_ref, bias_ref, o_ref, *, n_lat, n_ctx, nb, group, span):
    b = pl.program_id(1)
    scale = HEAD_DIM ** -0.5
    start = pl.multiple_of(jnp.clip(b * ATT_BLOCK - WINDOW, 0, n_lat - span), WINDOW)

    def one_kv_head(kv, local):
        kcols = slice(kv * HEAD_DIM, (kv + 1) * HEAD_DIM)
        kc = k_ref[n_lat:n_lat + n_ctx, kcols]
        vc1 = _with_ones(v_ref[n_lat:n_lat + n_ctx, kcols])
        if local:
            kl = k_ref[pl.ds(start, span), kcols]
            vl1 = _with_ones(v_ref[pl.ds(start, span), kcols])
        for g in range(group):
            qcols = slice((kv * group + g) * HEAD_DIM, (kv * group + g + 1) * HEAD_DIM)
            q = q_ref[:, qcols]
            sink = sink_ref[kv, g:g + 1, 0:1]
            s_ctx = _dot_nt(q, kc) * scale
            if local:
                s_loc = _dot_nt(q, kl) * scale + bias_ref[0]
                m = jnp.maximum(_row_max(s_ctx, s_loc), sink)
                acc = _dot(jnp.exp(s_ctx - m).astype(BF16), vc1) + _dot(jnp.exp(s_loc - m).astype(BF16), vl1)
            else:
                m = jnp.maximum(_row_max(s_ctx), sink)
                acc = _dot(jnp.exp(s_ctx - m).astype(BF16), vc1)
            den = acc[:, HEAD_DIM:] + jnp.exp(sink - m)
            o_ref[:, qcols] = (acc[:, :HEAD_DIM] / den).astype(o_ref.dtype)

    @pl.when(b < nb)
    def _():
        for kv in range(SWA_KV_PER_STEP):
            one_kv_head(kv, True)

    @pl.when(b >= nb)
    def _():
        for kv in range(SWA_KV_PER_STEP):
            one_kv_head(kv, False)


def _swa_bias(n_lat):
    span = ATT_BLOCK + 2 * WINDOW
    nb = n_lat // ATT_BLOCK
    i = np.arange(ATT_BLOCK)[:, None]
    j = np.arange(span)[None, :]
    tables = []
    for b in (0, 1, nb - 1):
        start = int(np.clip(b * ATT_BLOCK - WINDOW, 0, n_lat - span))
        rel = (start + j) - (b * ATT_BLOCK + i)
        tables.append(np.where(np.abs(rel) <= WINDOW, 0.0, MASKED))
    return jnp.asarray(np.stack(tables), F32)


def _swa_attention(p, sink, *, n_lat, n_ctx, heads, kv_heads):
    rows = n_lat + n_ctx
    group = heads // kv_heads
    nb = n_lat // ATT_BLOCK
    span = ATT_BLOCK + 2 * WINDOW
    kps = SWA_KV_PER_STEP
    sink_tab = jnp.broadcast_to(sink.astype(F32).reshape(kv_heads, group, 1), (kv_heads, group, HEAD_DIM))

    def pattern(b):
        return jnp.where(b == 0, 0, jnp.where(b == nb - 1, 2, 1))

    return pl.pallas_call(
        functools.partial(_swa_kernel, n_lat=n_lat, n_ctx=n_ctx, nb=nb, group=group, span=span),
        grid=(kv_heads // kps, rows // ATT_BLOCK),
        in_specs=[
            pl.BlockSpec((ATT_BLOCK, kps * group * HEAD_DIM), lambda k, b: (b, k)),
            pl.BlockSpec((rows, kps * HEAD_DIM), lambda k, b: (0, heads // kps + k)),
            pl.BlockSpec((rows, kps * HEAD_DIM), lambda k, b: (0, (heads + kv_heads) // kps + k)),
            pl.BlockSpec((kps, group, HEAD_DIM), lambda k, b: (k, 0, 0)),
            pl.BlockSpec((1, ATT_BLOCK, span), lambda k, b: (pattern(b), 0, 0)),
        ],
        out_specs=pl.BlockSpec((ATT_BLOCK, kps * group * HEAD_DIM), lambda k, b: (b, k)),
        out_shape=jax.ShapeDtypeStruct((rows, heads * HEAD_DIM), BF16),
        compiler_params=_params("arbitrary", "arbitrary"),
        name="swa_attention",
    )(p, p, p, sink_tab, _swa_bias(n_lat))


def kernel(x, c, ctx, c_ctx, w_mod, b_mod, norm_g, w_ff_in, w_ff_out, w_in_even, w_out_even,
           na_rpb, hg_lb_logits, hg_norm_g, w_qkv_odd, w_o_odd, sink_odd, final_norm_g):
    assert x.shape[0] == 1
    depth, d = w_mod.shape[0], x.shape[2]
    n_lat, n_ctx = x.shape[1], ctx.shape[1]
    rows = n_lat + n_ctx
    tm = 768
    assert rows % tm == 0 and n_lat % 1024 == 0 and n_ctx == ATT_BLOCK
    na_heads = na_rpb.shape[1]
    a_w = na_heads * HEAD_DIM
    hg_heads = hg_norm_g.shape[1] // HEAD_DIM
    heads = sink_odd.shape[1]
    kv_heads = (w_qkv_odd.shape[2] // HEAD_DIM - heads) // 2

    cc = jnp.zeros((8, d), F32).at[0].set(c[0]).at[1].set(c_ctx)
    mod_all = _modulation(cc, w_mod, b_mod)

    cos, sin = _rope_tables(n_lat, n_ctx)
    na_bias = _na_bias(na_rpb.reshape((-1,) + na_rpb.shape[2:]), n_lat // GRID_W)
    hg_consts = _hgrn_consts()
    hg_consts = (jnp.asarray(hg_consts[0], BF16), jnp.asarray(hg_consts[1], F32))
    fg = final_norm_g.reshape(1, d)
    b_w = hg_heads * HEAD_DIM
    tn = 1024
    c_f, c_i, c_g, c_end = (3 * a_w + b_w) // tn, (3 * a_w + 3 * b_w) // tn, (3 * a_w + 4 * b_w) // tn, w_in_even.shape[2] // tn
    even_blocks = tuple(range(c_f)) + tuple(range(c_i, c_g)) + tuple(range(c_f, c_i)) + tuple(range(c_g, c_end))
    w_in_even, w_out_even = w_in_even.astype(BF16), w_out_even.astype(BF16)
    w_qkv_odd, w_o_odd = w_qkv_odd.astype(BF16), w_o_odd.astype(BF16)

    h = x.reshape(n_lat, d)
    ffn_tm = FFN_ROWS
    ffn_tail = rows % ffn_tm or ffn_tm
    h_tail = jnp.concatenate([h[rows - ffn_tail:], ctx.reshape(n_ctx, d)], axis=0)
    for l in range(depth):
        mod = mod_all[l]
        last = l == depth - 1
        h = _ffn(h, mod, norm_g[l, 0].reshape(1, d), w_ff_in, w_ff_out, fg, layer=l, half=0,
                 sub=0, rows=rows, tm=ffn_tm, n_lat=n_lat, h_tail=h_tail if l == 0 else None)
        g1 = norm_g[l, 1].reshape(1, d)
        if l % 2 == 0:
            e = l // 2
            p16, p32 = _proj(h, mod, g1, w_in_even, layer=e, tm=tm, tn=tn, n_lat=n_lat, cols16=3 * a_w + 2 * b_w,
                             col_blocks=even_blocks)
            ya = _na_attention(p16, na_bias, layer=e, n_lat=n_lat, n_ctx=n_ctx, heads=na_heads)
            o_f, o_b = _hgrn_scan(p16, p32, hg_lb_logits, hg_consts, layer_e=e, n_lat=n_lat, n_ctx=n_ctx,
                                  heads=hg_heads, qcol=3 * a_w, icol=3 * a_w + b_w)
            h = _oproj_hgrn(ya, o_f, o_b, p32, 2 * b_w, hg_norm_g[e].reshape(1, -1), w_out_even, h, mod,
                            layer=e, tm=tm // 2, n_lat=n_lat)
        else:
            o = l // 2
            p16 = _qkv_proj(h, mod, g1, w_qkv_odd, cos, sin, layer=o, tm=tm, n_lat=n_lat,
                            rope_cols=(heads + kv_heads) * HEAD_DIM)
            y = _swa_attention(p16, sink_odd[o], n_lat=n_lat, n_ctx=n_ctx, heads=heads, kv_heads=kv_heads)
            h = _oproj(y, y, 1, w_o_odd, h, mod, layer=o, tm=tm, n_lat=n_lat)
        g2 = norm_g[l, 2].reshape(1, d)
        if last:
            h = _ffn(h, mod, g2, w_ff_in, w_ff_out, fg, layer=l, half=1,
                     sub=2, rows=n_lat, tm=ffn_tm, n_lat=n_lat, final_norm=True)
        else:
            h = _ffn(h, mod, g2, w_ff_in, w_ff_out, fg, layer=l, half=1,
                     sub=2, rows=rows, tm=ffn_tm, n_lat=n_lat)
    return h.reshape(1, n_lat, d)
```

```python
import functools

import numpy as np
import jax
import jax.numpy as jnp
from jax import lax
from jax.experimental import pallas as pl
from jax.experimental.pallas import tpu as pltpu

F32 = jnp.float32
BF16 = jnp.bfloat16

GRID_W = 64
NORM_EPS = 1e-6
ROPE_THETA = 10000.0
HEAD_DIM = 128
NA_ROWS = 8
NA_COLS = 16
NA_Q_ROWS = 4
NA_K_ROWS = NA_Q_ROWS + NA_ROWS
ATT_BLOCK = NA_Q_ROWS * GRID_W
NA_HEADS_PER_STEP = 8
SWA_KV_PER_STEP = 4
WINDOW = 128
HG_STEP = 256
HG_CHUNK = 128
HG_HEADS_PER_STEP = 8
ROW_CHUNK = 16
FFN_ROWS = 960
FFN_COLS = 512
FFN_TAIL_COLS = 256
QKV_SLAB = 512
FFN_OUT_COLS = 512
ROW_UNROLL = 4
HG_ROW_LEVEL = 8
MASKED = -1e30
VMEM_LIMIT = 56 * 1024 * 1024


def _params(*sem):
    return pltpu.CompilerParams(dimension_semantics=sem, vmem_limit_bytes=VMEM_LIMIT)


def _dot(a, b):
    return jnp.dot(a, b, preferred_element_type=F32)


def _dot_nt(a, b):
    return lax.dot_general(a, b, (((1,), (1,)), ((), ())), preferred_element_type=F32)


def _sigmoid(x):
    return 1.0 / (1.0 + jnp.exp(-x))


def _silu(x):
    return x * _sigmoid(x)


def _mod_kernel(c_ref, w_ref, b_ref, o_ref):
    s = _silu(c_ref[...]).astype(BF16)
    o_ref[0, 0] = _dot(s, w_ref[0].astype(BF16)) + b_ref[0]


def _modulation(cc, w_mod, b_mod):
    depth, d, n = w_mod.shape
    return pl.pallas_call(
        _mod_kernel,
        grid=(depth, n // d),
        in_specs=[
            pl.BlockSpec((8, d), lambda l, j: (0, 0)),
            pl.BlockSpec((1, d, d), lambda l, j: (l, 0, j)),
            pl.BlockSpec((1, 1, d), lambda l, j: (l, 0, j)),
        ],
        out_specs=pl.BlockSpec((1, 1, 8, d), lambda l, j: (l, j, 0, 0)),
        out_shape=jax.ShapeDtypeStruct((depth, n // d, 8, d), F32),
        compiler_params=_params("arbitrary", "arbitrary"),
        name="modulation",
    )(cc, w_mod, b_mod.reshape(depth, 1, n))


def _mod_rows(mod_ref, sub, k, is_ctx):
    r = 3 * sub + k
    return jnp.where(is_ctx, mod_ref[r, 1:2, :], mod_ref[r, 0:1, :])


def _row_max(*blocks):
    acc = None
    for s in blocks:
        for c0 in range(0, s.shape[1], HEAD_DIM):
            t = s[:, c0:c0 + HEAD_DIM]
            acc = t if acc is None else jnp.maximum(acc, t)
    return jnp.max(acc, axis=-1, keepdims=True)


def _with_ones(v):
    return jnp.concatenate([v, jnp.ones_like(v)], axis=1)


def _is_ctx(tile, tm, n_lat):
    row = tile * tm + lax.broadcasted_iota(jnp.int32, (tm, 1), 0)
    return row >= n_lat


def _row_chunks(row0, tm, n_lat, body):
    def step(c, carry):
        r0 = pl.multiple_of(c * ROW_CHUNK, ROW_CHUNK)
        body(pl.ds(r0, ROW_CHUNK), (row0 + r0 >= n_lat).astype(jnp.int32))
        return carry

    lax.fori_loop(0, tm // ROW_CHUNK, step, 0, unroll=ROW_UNROLL)


def _adaln_rows(h_ref, mod_ref, g_ref, u_scr, gs_scr, *, sub, row0, tm, n_lat):
    for which in range(2):
        gs_scr[which] = g_ref[...] * (1.0 + mod_ref[3 * sub + 1, which:which + 1, :])

    def body(rows, which):
        x = h_ref[rows, :]
        r = lax.rsqrt(jnp.mean(x * x, axis=-1, keepdims=True) + NORM_EPS)
        u_scr[rows, :] = ((x * r) * gs_scr[which] + mod_ref[3 * sub, pl.ds(which, 1), :]).astype(BF16)

    _row_chunks(row0, tm, n_lat, body)


def _ffn_kernel(h_ref, mod_ref, g_ref, wg_ref, wu_ref, wo_ref, fg_ref, *rest,
                sub, tm, row0, n_lat, final_norm, emit16, aliased):
    rest = rest[1:] if aliased else rest
    o_ref = rest[0]
    u_scr, gs_scr = rest[-2:]
    i, j = pl.program_id(0), pl.program_id(1)
    first_row = row0 + i * tm

    @pl.when(j == 0)
    def _():
        _adaln_rows(h_ref, mod_ref, g_ref, u_scr, gs_scr, sub=sub, row0=first_row, tm=tm, n_lat=n_lat)
        o_ref[...] = jnp.zeros_like(o_ref)

    if emit16:
        wg16_ref, wu16_ref, wo16_ref = rest[1:4]
        wg16_ref[...] = wg_ref[...].astype(BF16)
        wu16_ref[...] = wu_ref[...].astype(BF16)
        wo16_ref[...] = wo_ref[...].astype(BF16)
        wg_ref, wu_ref, wo_ref = wg16_ref, wu16_ref, wo16_ref
    u = u_scr[...]
    a = (_silu(_dot(u, wg_ref[...])) * _dot(u, wu_ref[...])).astype(BF16)
    d = o_ref.shape[1]
    for c0 in range(0, d, FFN_OUT_COLS):
        o_ref[:, c0:c0 + FFN_OUT_COLS] += _dot(a, wo_ref[:, c0:c0 + FFN_OUT_COLS])

    @pl.when(j == pl.num_programs(1) - 1)
    def _():
        def body(rows, which):
            h = h_ref[rows, :] + (0.5 * mod_ref[3 * sub + 2, pl.ds(which, 1), :]) * o_ref[rows, :]
            if final_norm:
                h = h * lax.rsqrt(jnp.mean(h * h, axis=-1, keepdims=True) + NORM_EPS) * fg_ref[...]
            o_ref[rows, :] = h

        _row_chunks(first_row, tm, n_lat, body)


def _ffn(h, mod, g, w_in, w_out, fg, *, layer, half, sub, rows, tm, n_lat, final_norm=False, h_tail=None):
    d = h.shape[1]
    f = w_out.shape[2]
    tail = rows % tm or tm
    n_main = (rows - tail) // tm
    assert (rows - tail) % tail == 0
    tail_src, tail_blk = (h, (rows - tail) // tail) if h_tail is None else (h_tail, 0)
    common = dict(sub=sub, n_lat=n_lat, final_norm=final_norm)
    small = [
        pl.BlockSpec((9, 8, d), lambda i, j: (0, 0, 0)),
        pl.BlockSpec((1, d), lambda i, j: (0, 0)),
    ]
    fg_spec = pl.BlockSpec((1, d), lambda i, j: (0, 0))

    tf, t0 = FFN_TAIL_COLS, (rows - tail) // tail
    nf = f // tf
    out, wg16, wu16, wo16 = pl.pallas_call(
        functools.partial(_ffn_kernel, tm=tail, row0=rows - tail, emit16=True, aliased=False, **common),
        grid=(1, nf),
        in_specs=[pl.BlockSpec((tail, d), lambda i, j: (tail_blk, 0))] + small + [
            pl.BlockSpec((None, None, d, tf), lambda i, j: (layer, half, 0, j)),
            pl.BlockSpec((None, None, d, tf), lambda i, j: (layer, half, 0, nf + j)),
            pl.BlockSpec((None, None, tf, d), lambda i, j: (layer, half, j, 0)),
            fg_spec,
        ],
        out_specs=[
            pl.BlockSpec((tail, d), lambda i, j: (t0, 0)),
            pl.BlockSpec((d, tf), lambda i, j: (0, j)),
            pl.BlockSpec((d, tf), lambda i, j: (0, j)),
            pl.BlockSpec((tf, d), lambda i, j: (j, 0)),
        ],
        out_shape=[
            jax.ShapeDtypeStruct((rows, d), F32),
            jax.ShapeDtypeStruct((d, f), BF16),
            jax.ShapeDtypeStruct((d, f), BF16),
            jax.ShapeDtypeStruct((f, d), BF16),
        ],
        scratch_shapes=[pltpu.VMEM((tail, d), BF16), pltpu.VMEM((2, 1, d), F32)],
        compiler_params=_params("arbitrary", "arbitrary"),
        name="ffn_tail",
    )(tail_src, mod, g, w_in, w_in, w_out, fg)

    tf = FFN_COLS
    return pl.pallas_call(
        functools.partial(_ffn_kernel, tm=tm, row0=0, emit16=False, aliased=True, **common),
        grid=(n_main, f // tf),
        in_specs=[pl.BlockSpec((tm, d), lambda i, j: (i, 0))] + small + [
            pl.BlockSpec((d, tf), lambda i, j: (0, j)),
            pl.BlockSpec((d, tf), lambda i, j: (0, j)),
            pl.BlockSpec((tf, d), lambda i, j: (j, 0)),
            fg_spec,
            pl.BlockSpec(memory_space=pl.ANY),
        ],
        out_specs=pl.BlockSpec((tm, d), lambda i, j: (i, 0)),
        out_shape=jax.ShapeDtypeStruct((rows, d), F32),
        input_output_aliases={7: 0},
        scratch_shapes=[pltpu.VMEM((tm, d), BF16), pltpu.VMEM((2, 1, d), F32)],
        compiler_params=_params("arbitrary", "arbitrary"),
        name="ffn",
    )(h, mod, g, wg16, wu16, wo16, fg, out)


def _rope(x, cos, sin):
    lane = lax.broadcasted_iota(jnp.int32, x.shape, 1)
    first = (lane % 64) < 32
    swapped = jnp.where(first, pltpu.roll(x, 96, 1), pltpu.roll(x, 32, 1))
    return x * cos + swapped * sin


def _proj_kernel(h_ref, mod_ref, g_ref, w_ref, o16_ref, o32_ref, u_scr, gs_scr, *, tm, n_lat, n16):
    i, j = pl.program_id(0), pl.program_id(1)

    @pl.when(j == 0)
    def _():
        _adaln_rows(h_ref, mod_ref, g_ref, u_scr, gs_scr, sub=1, row0=i * tm, tm=tm, n_lat=n_lat)

    y = _dot(u_scr[...], w_ref[...])

    @pl.when(j < n16)
    def _():
        o16_ref[...] = y.astype(BF16)

    @pl.when(j >= n16)
    def _():
        o32_ref[...] = y


def _proj(h, mod, g, w, *, layer, tm, tn, n_lat, cols16, col_blocks):
    rows, d = h.shape
    n = w.shape[2]
    n16 = cols16 // tn

    def src_block(j):
        blk = j
        for dst, src in enumerate(col_blocks):
            if dst != src:
                blk = jnp.where(j == dst, src, blk)
        return blk

    return pl.pallas_call(
        functools.partial(_proj_kernel, tm=tm, n_lat=n_lat, n16=n16),
        grid=(rows // tm, n // tn),
        in_specs=[
            pl.BlockSpec((tm, d), lambda i, j: (i, 0)),
            pl.BlockSpec((9, 8, d), lambda i, j: (0, 0, 0)),
            pl.BlockSpec((1, d), lambda i, j: (0, 0)),
            pl.BlockSpec((None, d, tn), lambda i, j: (layer, 0, src_block(j))),
        ],
        out_specs=[
            pl.BlockSpec((tm, tn), lambda i, j: (i, jnp.minimum(j, n16 - 1))),
            pl.BlockSpec((tm, tn), lambda i, j: (i, jnp.maximum(j - n16, 0))),
        ],
        out_shape=[jax.ShapeDtypeStruct((rows, cols16), BF16), jax.ShapeDtypeStruct((rows, n - cols16), F32)],
        scratch_shapes=[pltpu.VMEM((tm, d), BF16), pltpu.VMEM((2, 1, d), F32)],
        compiler_params=_params("arbitrary", "arbitrary"),
        name="proj",
    )(h, mod, g, w)


def _qkv_kernel(h_ref, mod_ref, g_ref, w_ref, cos_ref, sin_ref, o_ref, u_scr, gs_scr, *, tm, n_lat, rope_cols):
    _adaln_rows(h_ref, mod_ref, g_ref, u_scr, gs_scr, sub=1, row0=pl.program_id(0) * tm, tm=tm, n_lat=n_lat)
    u = u_scr[...]
    cos, sin = cos_ref[...], sin_ref[...]
    for c0 in range(0, o_ref.shape[1], QKV_SLAB):
        y = _dot(u, w_ref[:, c0:c0 + QKV_SLAB])
        for hd in range(0, QKV_SLAB, HEAD_DIM):
            x = y[:, hd:hd + HEAD_DIM]
            if c0 + hd < rope_cols:
                x = _rope(x, cos, sin)
            o_ref[:, c0 + hd:c0 + hd + HEAD_DIM] = x.astype(BF16)


def _qkv_proj(h, mod, g, w, cos, sin, *, layer, tm, n_lat, rope_cols):
    rows, d = h.shape
    n = w.shape[2]
    return pl.pallas_call(
        functools.partial(_qkv_kernel, tm=tm, n_lat=n_lat, rope_cols=rope_cols),
        grid=(rows // tm,),
        in_specs=[
            pl.BlockSpec((tm, d), lambda i: (i, 0)),
            pl.BlockSpec((9, 8, d), lambda i: (0, 0, 0)),
            pl.BlockSpec((1, d), lambda i: (0, 0)),
            pl.BlockSpec((None, d, n), lambda i: (layer, 0, 0), pipeline_mode=pl.Buffered(1)),
            pl.BlockSpec((tm, HEAD_DIM), lambda i: (i, 0)),
            pl.BlockSpec((tm, HEAD_DIM), lambda i: (i, 0)),
        ],
        out_specs=pl.BlockSpec((tm, n), lambda i: (i, 0)),
        out_shape=jax.ShapeDtypeStruct((rows, n), BF16),
        scratch_shapes=[pltpu.VMEM((tm, d), BF16), pltpu.VMEM((2, 1, d), F32)],
        compiler_params=_params("arbitrary"),
        name="qkv_proj",
    )(h, mod, g, w, cos, sin)


def _rope_tables(n_lat, n_ctx):
    t = np.arange(n_lat)
    inv = (ROPE_THETA ** (-np.arange(0, 64, 2, dtype=np.float32) / 64)).astype(np.float32)
    ang_r = (t // GRID_W).astype(np.float32)[:, None] * inv[None, :]
    ang_c = (t % GRID_W).astype(np.float32)[:, None] * inv[None, :]
    cr, sr, cc, sc = np.cos(ang_r), np.sin(ang_r), np.cos(ang_c), np.sin(ang_c)
    cos = np.concatenate([cr, cr, cc, cc], axis=1)
    sin = np.concatenate([-sr, sr, -sc, sc], axis=1)
    cos = np.concatenate([cos, np.ones((n_ctx, HEAD_DIM), np.float32)], axis=0)
    sin = np.concatenate([sin, np.zeros((n_ctx, HEAD_DIM), np.float32)], axis=0)
    return jnp.asarray(cos, F32), jnp.asarray(sin, F32)


def _oproj_kernel(a1_ref, a2_ref, w_ref, h_ref, mod_ref, o_ref, *, tm, n_lat, k1):
    is_ctx = _is_ctx(pl.program_id(0), tm, n_lat)
    y = _dot(a1_ref[...], w_ref[:k1, :]) + _dot(a2_ref[...], w_ref[k1:, :])
    o_ref[...] = h_ref[...] + _mod_rows(mod_ref, 1, 2, is_ctx) * y


def _oproj(a1, a2, col2, w, h, mod, *, layer, tm, n_lat):
    rows, d = h.shape
    k1 = w.shape[1] // 2
    return pl.pallas_call(
        functools.partial(_oproj_kernel, tm=tm, n_lat=n_lat, k1=k1),
        grid=(rows // tm,),
        in_specs=[
            pl.BlockSpec((tm, k1), lambda i: (i, 0)),
            pl.BlockSpec((tm, k1), lambda i: (i, col2)),
            pl.BlockSpec((None,) + w.shape[1:], lambda i: (layer, 0, 0)),
            pl.BlockSpec((tm, d), lambda i: (i, 0)),
            pl.BlockSpec((9, 8, d), lambda i: (0, 0, 0)),
        ],
        out_specs=pl.BlockSpec((tm, d), lambda i: (i, 0)),
        out_shape=jax.ShapeDtypeStruct((rows, d), F32),
        compiler_params=_params("arbitrary"),
        name="oproj",
    )(a1, a2, w, h, mod)


def _oproj_hgrn_kernel(a1_ref, of_ref, ob_ref, g_ref, ng_ref, w_ref, h_ref, mod_ref, o_ref, yb_scr,
                       *, tm, n_lat, k1):
    for c0 in range(0, k1, HEAD_DIM):
        sl = slice(c0, c0 + HEAD_DIM)
        o = of_ref[:, sl] + ob_ref[:, sl]
        o = o * lax.rsqrt(jnp.mean(o * o, axis=-1, keepdims=True) + NORM_EPS)
        yb_scr[:, sl] = (o * ng_ref[:, sl] * _silu(g_ref[:, sl])).astype(BF16)
    is_ctx = _is_ctx(pl.program_id(0), tm, n_lat)
    y = _dot(a1_ref[...], w_ref[:k1, :]) + _dot(yb_scr[...], w_ref[k1:, :])
    o_ref[...] = h_ref[...] + _mod_rows(mod_ref, 1, 2, is_ctx) * y


def _oproj_hgrn(a1, o_f, o_b, p32, gcol0, norm_g, w, h, mod, *, layer, tm, n_lat):
    rows, d = h.shape
    k1 = w.shape[1] // 2
    half = lambda i: (i, 0)
    return pl.pallas_call(
        functools.partial(_oproj_hgrn_kernel, tm=tm, n_lat=n_lat, k1=k1),
        grid=(rows // tm,),
        in_specs=[
            pl.BlockSpec((tm, k1), half),
            pl.BlockSpec((tm, k1), half),
            pl.BlockSpec((tm, k1), half),
            pl.BlockSpec((tm, k1), lambda i: (i, gcol0 // k1)),
            pl.BlockSpec((1, k1), lambda i: (0, 0)),
            pl.BlockSpec((None,) + w.shape[1:], lambda i: (layer, 0, 0)),
            pl.BlockSpec((tm, d), half),
            pl.BlockSpec((9, 8, d), lambda i: (0, 0, 0)),
        ],
        out_specs=pl.BlockSpec((tm, d), half),
        out_shape=jax.ShapeDtypeStruct((rows, d), F32),
        scratch_shapes=[pltpu.VMEM((tm, k1), BF16)],
        compiler_params=_params("arbitrary"),
        name="oproj_hgrn",
    )(a1, o_f, o_b, p32, norm_g, w, h, mod)


def _na_bias(rpb, n_rows):
    nh, n_a, n_b = rpb.shape
    nb = n_rows // NA_Q_ROWS
    w = jnp.full((nh, n_a, 128), MASKED, F32)
    w = w.at[..., :NA_COLS].set(rpb[..., NA_COLS - 1:]).at[..., 128 - (NA_COLS - 1):].set(rpb[..., :NA_COLS - 1])
    toep = jnp.tile(w, (1, 1, GRID_W))[..., :GRID_W * 127].reshape(nh, n_a, GRID_W, 127)[..., :GRID_W]
    toep = jnp.pad(toep.transpose(0, 2, 1, 3), ((0, 0), (0, 0), (NA_K_ROWS, NA_K_ROWS), (0, 0)),
                   constant_values=MASKED).reshape(nh, GRID_W, -1)

    q = np.arange(ATT_BLOCK)
    k = np.arange(NA_K_ROWS * GRID_W)
    dr, c = q // GRID_W, q % GRID_W
    kr, kc = k // GRID_W, k % GRID_W
    c0 = np.clip(c - NA_COLS // 2, 0, GRID_W - NA_COLS)
    col_ok = (kc[None, :] >= c0[:, None]) & (kc[None, :] < c0[:, None] + NA_COLS)
    tables, ok = [], []
    for b in (0, 1, nb - 1):
        r = b * NA_Q_ROWS + dr
        r0 = np.clip(r - NA_ROWS // 2, 0, n_rows - NA_ROWS)
        base = int(np.clip(b * NA_Q_ROWS - NA_ROWS // 2, 0, n_rows - NA_K_ROWS))
        kabs = base + kr
        ok.append((kabs[None, :] >= r0[:, None]) & (kabs[None, :] < r0[:, None] + NA_ROWS) & col_ok)
        per_row = []
        for j in range(NA_Q_ROWS):
            lo = base - (b * NA_Q_ROWS + j) + NA_ROWS - 1 + NA_K_ROWS
            per_row.append(toep[:, :, lo * GRID_W:(lo + NA_K_ROWS) * GRID_W])
        tables.append(jnp.concatenate(per_row, axis=1))
    return jnp.where(np.stack(ok)[:, None], jnp.stack(tables), MASKED)


def _na_kernel(q_ref, k_ref, v_ref, bias_ref, o_ref, *, n_lat, n_ctx, nb):
    b = pl.program_id(1)
    scale = HEAD_DIM ** -0.5
    nk = NA_K_ROWS * GRID_W
    start = pl.multiple_of(jnp.clip(b - 1, 0, nb - 3) * ATT_BLOCK, ATT_BLOCK)

    def one_head(hh, local):
        cols = slice(hh * HEAD_DIM, (hh + 1) * HEAD_DIM)
        q = q_ref[:, cols]
        s_ctx = _dot_nt(q, k_ref[n_lat:n_lat + n_ctx, cols]) * scale
        vc1 = _with_ones(v_ref[n_lat:n_lat + n_ctx, cols])
        if local:
            kl = k_ref[pl.ds(start, nk), cols]
            vl1 = _with_ones(v_ref[pl.ds(start, nk), cols])
            s_loc = _dot_nt(q, kl) * scale + bias_ref[0, hh]
            m = _row_max(s_loc, s_ctx)
            acc = _dot(jnp.exp(s_loc - m).astype(BF16), vl1) + _dot(jnp.exp(s_ctx - m).astype(BF16), vc1)
        else:
            acc = _dot(jnp.exp(s_ctx - _row_max(s_ctx)).astype(BF16), vc1)
        o_ref[:, cols] = (acc[:, :HEAD_DIM] / acc[:, HEAD_DIM:]).astype(o_ref.dtype)

    @pl.when(b < nb)
    def _():
        for hh in range(NA_HEADS_PER_STEP):
            one_head(hh, True)

    @pl.when(b >= nb)
    def _():
        for hh in range(NA_HEADS_PER_STEP):
            one_head(hh, False)


def _na_attention(p, bias, *, layer, n_lat, n_ctx, heads):
    rows = n_lat + n_ctx
    nb = n_lat // ATT_BLOCK
    nblk = rows // ATT_BLOCK
    hps = NA_HEADS_PER_STEP
    width = hps * HEAD_DIM
    groups = heads // hps

    def pattern(b):
        return jnp.where(b == 0, 0, jnp.where(b == nb - 1, 2, 1))

    resident = dict(pipeline_mode=pl.Buffered(1)) if groups == 1 else {}
    return pl.pallas_call(
        functools.partial(_na_kernel, n_lat=n_lat, n_ctx=n_ctx, nb=nb),
        grid=(groups, nblk),
        in_specs=[
            pl.BlockSpec((ATT_BLOCK, width), lambda h, b: (b, h)),
            pl.BlockSpec((rows, width), lambda h, b: (0, groups + h), **resident),
            pl.BlockSpec((rows, width), lambda h, b: (0, 2 * groups + h), **resident),
            pl.BlockSpec((1, hps, ATT_BLOCK, NA_K_ROWS * GRID_W),
                         lambda h, b: (pattern(b), layer * groups + h, 0, 0)),
        ],
        out_specs=pl.BlockSpec((ATT_BLOCK, width), lambda h, b: (b, h)),
        out_shape=jax.ShapeDtypeStruct((rows, heads * HEAD_DIM), BF16),
        compiler_params=_params("arbitrary", "arbitrary"),
        name="na_attention",
    )(p, p, p, bias)


def _hgrn_levels():
    levels, m = [], 1
    while m < HG_CHUNK:
        levels.append(m)
        m *= 2
    return levels


def _hgrn_consts():
    c = HG_CHUNK
    levels = _hgrn_levels()
    fine = [m for m in levels if m < HG_ROW_LEVEL]
    idx = np.arange(c)
    t, j = idx[:, None], idx[None, :]
    n = np.zeros((2, 1 + len(fine), c, c), np.float32)
    msk = np.zeros((2, 1 + len(levels), c, c), np.float32)
    n[0, 0], n[1, 0] = j <= t, j >= t
    msk[0, 0] = msk[1, 0] = np.eye(c)
    for li, m in enumerate(levels):
        seg = idx // (2 * m)
        right = ((idx % (2 * m)) >= m)[:, None]
        last_left = (seg * 2 * m + m - 1)[:, None]
        first_right = last_left + 1
        if m < HG_ROW_LEVEL:
            n[0, 1 + li] = np.where(right, (j > last_left) & (j <= t), (j > t) & (j <= last_left))
            n[1, 1 + li] = np.where(right, (j >= first_right) & (j < t), (j >= t) & (j < first_right))
        same = seg[:, None] == seg[None, :]
        msk[0, 1 + li] = same & right & ~right.T
        msk[1, 1 + li] = same & ~right & right.T
    return n.reshape(2, -1, c), msk


def _hgrn_chunk(d, q, fx, v, lb, n_ref, msk_ref, s_scr, b_scr):
    c = HG_CHUNK
    t = jnp.exp(-jnp.abs(fx))
    r = 1.0 / (1.0 + t)
    log_sig = jnp.minimum(fx, 0.0) + jnp.log(r)
    sig_neg = jnp.where(fx >= 0.0, t * r, r)
    if lb is None:
        log_f, kk = log_sig, sig_neg
    else:
        la = jnp.log(lb)
        lc = jnp.log1p(-lb) + log_sig
        log_f = jnp.maximum(la, lc) + jnp.log(1.0 + jnp.exp(-jnp.abs(la - lc)))
        kk = (1.0 - lb) * sig_neg
    qf = q.astype(F32)

    hi = log_f.astype(BF16)
    mid = (log_f - hi.astype(F32)).astype(BF16)
    dd = _dot(n_ref[d], jnp.concatenate([hi, mid], axis=1))
    dd = dd[:, :HEAD_DIM] + dd[:, HEAD_DIM:]
    b = dd[:c]
    b_scr[d] = b

    attn = _dot_nt(q, kk.astype(BF16)) * msk_ref[d, 0]
    for li, m in enumerate(_hgrn_levels()):
        if m < HG_ROW_LEVEL:
            ex = dd[(1 + li) * c:(2 + li) * c]
        else:
            row = m - 1 + d
            ref = jnp.concatenate(
                [jnp.broadcast_to(b_scr[d, s + row:s + row + 1, :], (2 * m, HEAD_DIM)) for s in range(0, c, 2 * m)],
                axis=0)
            ex = -jnp.abs(b - ref)
        e = jnp.exp(ex)
        attn += _dot_nt((qf * e).astype(BF16), (kk * e).astype(BF16)) * msk_ref[d, 1 + li]

    total = b[c - 1:c] if d == 0 else b[0:1]
    state_t = s_scr[d]
    o = _dot_nt((qf * jnp.exp(b)).astype(BF16), state_t.astype(BF16)) + _dot(attn.astype(BF16), v)
    v_t = v.astype(F32).T.astype(BF16)
    s_scr[d] = jnp.exp(total) * state_t + _dot(v_t, (kk * jnp.exp(total - b)).astype(BF16))
    return o


def _hgrn_kernel(qf_ref, qb_ref, ff_ref, fb_ref, if_ref, ib_ref, lbl_ref, n_ref, msk_ref,
                 of_ref, ob_ref, s_scr, b_scr, *, layer_e):
    @pl.when(pl.program_id(1) == 0)
    def _():
        s_scr[...] = jnp.zeros_like(s_scr)

    refs = ((qf_ref, ff_ref, if_ref, of_ref), (qb_ref, fb_ref, ib_ref, ob_ref))
    lbs = [None, None]
    if layer_e > 0:
        for d in range(2):
            lg = lbl_ref[d]
            ex = jnp.exp(lg - jnp.max(lg, axis=0, keepdims=True))
            num = ex[1:2]
            for e in range(2, layer_e + 1):
                num = num + ex[e:e + 1]
            lbs[d] = num / jnp.sum(ex, axis=0, keepdims=True)

    n_sub = HG_STEP // HG_CHUNK
    for k in range(n_sub):
        for hh in range(HG_HEADS_PER_STEP):
            cols = slice(hh * HEAD_DIM, (hh + 1) * HEAD_DIM)
            for d, (q_ref, f_ref, i_ref, o_ref) in enumerate(refs):
                sub = k if d == 0 else n_sub - 1 - k
                rows = slice(sub * HG_CHUNK, (sub + 1) * HG_CHUNK)
                lb = None if lbs[d] is None else lbs[d][:, cols]
                o_ref[rows, cols] = _hgrn_chunk(d, q_ref[rows, cols], f_ref[rows, cols], i_ref[rows, cols], lb,
                                                n_ref, msk_ref, s_scr.at[hh], b_scr.at[hh, k])


def _hgrn_scan(p16, p32, lb_logits, consts, *, layer_e, n_lat, n_ctx, heads, qcol, icol):
    rows = n_lat + n_ctx
    n_mat, msk = consts
    lat_chunks = n_lat // HG_STEP
    qc, ic = qcol // HEAD_DIM, icol // HEAD_DIM

    def fwd(s):
        return jnp.where(s == 0, lat_chunks, s - 1)

    def bwd(s):
        return lat_chunks - s

    hps = HG_HEADS_PER_STEP
    width = hps * HEAD_DIM

    def spec(blk, col):
        return pl.BlockSpec((HG_STEP, width), lambda h, s: (blk(s), col // hps + h))

    out = jax.ShapeDtypeStruct((rows, heads * HEAD_DIM), F32)
    return pl.pallas_call(
        functools.partial(_hgrn_kernel, layer_e=layer_e),
        grid=(heads // hps, 1 + lat_chunks),
        in_specs=[
            spec(fwd, qc), spec(bwd, qc),
            spec(fwd, 0), spec(bwd, heads),
            spec(fwd, ic), spec(bwd, ic),
            pl.BlockSpec((2, lb_logits.shape[1], width), lambda h, s: (0, 0, h)),
            pl.BlockSpec(n_mat.shape, lambda h, s: (0, 0, 0)),
            pl.BlockSpec(msk.shape, lambda h, s: (0, 0, 0, 0)),
        ],
        out_specs=[spec(fwd, 0), spec(bwd, 0)],
        out_shape=[out, out],
        scratch_shapes=[pltpu.VMEM((hps, 2, HEAD_DIM, HEAD_DIM), F32),
                        pltpu.VMEM((hps, HG_STEP // HG_CHUNK, 2, HG_CHUNK, HEAD_DIM), F32)],
        compiler_params=_params("arbitrary", "arbitrary"),
        name="hgrn_scan",
    )(p16, p16, p32, p32, p16, p16, lb_logits, n_mat, msk)


def _swa_kernel(q_ref, k_ref, v_ref, sink_ref, bias_ref, o_ref, *, n_lat, n_ctx, nb, group, span):
    b = pl.program_id(1)
    scale = HEAD_DIM ** -0.5
    start = pl.multiple_of(jnp.clip(b * ATT_BLOCK - WINDOW, 0, n_lat - span), WINDOW)

    def one_kv_head(kv, local):
        kcols = slice(kv * HEAD_DIM, (kv + 1) * HEAD_DIM)
        kc = k_ref[n_lat:n_lat + n_ctx, kcols]
        vc1 = _with_ones(v_ref[n_lat:n_lat + n_ctx, kcols])
        if local:
            kl = k_ref[pl.ds(start, span), kcols]
            vl1 = _with_ones(v_ref[pl.ds(start, span), kcols])
        for g in range(group):
            qcols = slice((kv * group + g) * HEAD_DIM, (kv * group + g + 1) * HEAD_DIM)
            q = q_ref[:, qcols]
            sink = sink_ref[kv, g:g + 1, 0:1]
            s_ctx = _dot_nt(q, kc) * scale
            if local:
                s_loc = _dot_nt(q, kl) * scale + bias_ref[0]
                m = jnp.maximum(_row_max(s_ctx, s_loc), sink)
                acc = _dot(jnp.exp(s_ctx - m).astype(BF16), vc1) + _dot(jnp.exp(s_loc - m).astype(BF16), vl1)
            else:
                m = jnp.maximum(_row_max(s_ctx), sink)
                acc = _dot(jnp.exp(s_ctx - m).astype(BF16), vc1)
            den = acc[:, HEAD_DIM:] + jnp.exp(sink - m)
            o_ref[:, qcols] = (acc[:, :HEAD_DIM] / den).astype(o_ref.dtype)

    @pl.when(b < nb)
    def _():
        for kv in range(SWA_KV_PER_STEP):
            one_kv_head(kv, True)

    @pl.when(b >= nb)
    def _():
        for kv in range(SWA_KV_PER_STEP):
            one_kv_head(kv, False)


def _swa_bias(n_lat):
    span = ATT_BLOCK + 2 * WINDOW
    nb = n_lat // ATT_BLOCK
    i = np.arange(ATT_BLOCK)[:, None]
    j = np.arange(span)[None, :]
    tables = []
    for b in (0, 1, nb - 1):
        start = int(np.clip(b * ATT_BLOCK - WINDOW, 0, n_lat - span))
        rel = (start + j) - (b * ATT_BLOCK + i)
        tables.append(np.where(np.abs(rel) <= WINDOW, 0.0, MASKED))
    return jnp.asarray(np.stack(tables), F32)


def _swa_attention(p, sink, *, n_lat, n_ctx, heads, kv_heads):
    rows = n_lat + n_ctx
    group = heads // kv_heads
    nb = n_lat // ATT_BLOCK
    span = ATT_BLOCK + 2 * WINDOW
    kps = SWA_KV_PER_STEP
    sink_tab = jnp.broadcast_to(sink.astype(F32).reshape(kv_heads, group, 1), (kv_heads, group, HEAD_DIM))

    def pattern(b):
        return jnp.where(b == 0, 0, jnp.where(b == nb - 1, 2, 1))

    return pl.pallas_call(
        functools.partial(_swa_kernel, n_lat=n_lat, n_ctx=n_ctx, nb=nb, group=group, span=span),
        grid=(kv_heads // kps, rows // ATT_BLOCK),
        in_specs=[
            pl.BlockSpec((ATT_BLOCK, kps * group * HEAD_DIM), lambda k, b: (b, k)),
            pl.BlockSpec((rows, kps * HEAD_DIM), lambda k, b: (0, heads // kps + k)),
            pl.BlockSpec((rows, kps * HEAD_DIM), lambda k, b: (0, (heads + kv_heads) // kps + k)),
            pl.BlockSpec((kps, group, HEAD_DIM), lambda k, b: (k, 0, 0)),
            pl.BlockSpec((1, ATT_BLOCK, span), lambda k, b: (pattern(b), 0, 0)),
        ],
        out_specs=pl.BlockSpec((ATT_BLOCK, kps * group * HEAD_DIM), lambda k, b: (b, k)),
        out_shape=jax.ShapeDtypeStruct((rows, heads * HEAD_DIM), BF16),
        compiler_params=_params("arbitrary", "arbitrary"),
        name="swa_attention",
    )(p, p, p, sink_tab, _swa_bias(n_lat))


def kernel(x, c, ctx, c_ctx, w_mod, b_mod, norm_g, w_ff_in, w_ff_out, w_in_even, w_out_even,
           na_rpb, hg_lb_logits, hg_norm_g, w_qkv_odd, w_o_odd, sink_odd, final_norm_g):
    assert x.shape[0] == 1
    depth, d = w_mod.shape[0], x.shape[2]
    n_lat, n_ctx = x.shape[1], ctx.shape[1]
    rows = n_lat + n_ctx
    tm = 768
    assert rows % tm == 0 and n_lat % 1024 == 0 and n_ctx == ATT_BLOCK
    na_heads = na_rpb.shape[1]
    a_w = na_heads * HEAD_DIM
    hg_heads = hg_norm_g.shape[1] // HEAD_DIM
    heads = sink_odd.shape[1]
    kv_heads = (w_qkv_odd.shape[2] // HEAD_DIM - heads) // 2

    cc = jnp.zeros((8, d), F32).at[0].set(c[0]).at[1].set(c_ctx)
    mod_all = _modulation(cc, w_mod, b_mod)

    cos, sin = _rope_tables(n_lat, n_ctx)
    na_bias = _na_bias(na_rpb.reshape((-1,) + na_rpb.shape[2:]), n_lat // GRID_W)
    hg_consts = _hgrn_consts()
    hg_consts = (jnp.asarray(hg_consts[0], BF16), jnp.asarray(hg_consts[1], F32))
    fg = final_norm_g.reshape(1, d)
    b_w = hg_heads * HEAD_DIM
    tn = 1024
    c_f, c_i, c_g, c_end = (3 * a_w + b_w) // tn, (3 * a_w + 3 * b_w) // tn, (3 * a_w + 4 * b_w) // tn, w_in_even.shape[2] // tn
    even_blocks = tuple(range(c_f)) + tuple(range(c_i, c_g)) + tuple(range(c_f, c_i)) + tuple(range(c_g, c_end))
    w_in_even, w_out_even = w_in_even.astype(BF16), w_out_even.astype(BF16)
    w_qkv_odd, w_o_odd = w_qkv_odd.astype(BF16), w_o_odd.astype(BF16)

    h = x.reshape(n_lat, d)
    ffn_tm = FFN_ROWS
    ffn_tail = rows % ffn_tm or ffn_tm
    h_tail = jnp.concatenate([h[rows - ffn_tail:], ctx.reshape(n_ctx, d)], axis=0)
    for l in range(depth):
        mod = mod_all[l]
        last = l == depth - 1
        h = _ffn(h, mod, norm_g[l, 0].reshape(1, d), w_ff_in, w_ff_out, fg, layer=l, half=0,
                 sub=0, rows=rows, tm=ffn_tm, n_lat=n_lat, h_tail=h_tail if l == 0 else None)
        g1 = norm_g[l, 1].reshape(1, d)
        if l % 2 == 0:
            e = l // 2
            p16, p32 = _proj(h, mod, g1, w_in_even, layer=e, tm=tm, tn=tn, n_lat=n_lat, cols16=3 * a_w + 2 * b_w,
                             col_blocks=even_blocks)
            ya = _na_attention(p16, na_bias, layer=e, n_lat=n_lat, n_ctx=n_ctx, heads=na_heads)
            o_f, o_b = _hgrn_scan(p16, p32, hg_lb_logits, hg_consts, layer_e=e, n_lat=n_lat, n_ctx=n_ctx,
                                  heads=hg_heads, qcol=3 * a_w, icol=3 * a_w + b_w)
            h = _oproj_hgrn(ya, o_f, o_b, p32, 2 * b_w, hg_norm_g[e].reshape(1, -1), w_out_even, h, mod,
                            layer=e, tm=tm // 2, n_lat=n_lat)
        else:
            o = l // 2
            p16 = _qkv_proj(h, mod, g1, w_qkv_odd, cos, sin, layer=o, tm=tm, n_lat=n_lat,
                            rope_cols=(heads + kv_heads) * HEAD_DIM)
            y = _swa_attention(p16, sink_odd[o], n_lat=n_lat, n_ctx=n_ctx, heads=heads, kv_heads=kv_heads)
            h = _oproj(y, y, 1, w_o_odd, h, mod, layer=o, tm=tm, n_lat=n_lat)
        g2 = norm_g[l, 2].reshape(1, d)
        if last:
            h = _ffn(h, mod, g2, w_ff_in, w_ff_out, fg, layer=l, half=1,
                     sub=2, rows=n_lat, tm=ffn_tm, n_lat=n_lat, final_norm=True)
        else:
            h = _ffn(h, mod, g2, w_ff_in, w_ff_out, fg, layer=l, half=1,
                     sub=2, rows=rows, tm=ffn_tm, n_lat=n_lat)
    return h.reshape(1, n_lat, d)
```
